```python
import math
import jax, jax.numpy as jnp
from jax import lax
import numpy as np

D_MODEL = 1024
BATCH = 4
SEQ = 4096
DEPTH = 2

N_META = 16
BLOCK = 128
PAD = BLOCK - N_META

SSM_GROUPS = 16
SSM_GROUP_CH = 16
SSM_WIDTH = SSM_GROUPS * SSM_GROUP_CH
SSM_STATE = 64
SWA_HEADS = 8
SWA_KV_HEADS = 2
HEAD_DIM = 64
WINDOW = 128
SWA_WIDTH = SWA_HEADS * HEAD_DIM
SWA_KV_WIDTH = SWA_KV_HEADS * HEAD_DIM
MLA_HEADS = 4
MLA_Q_RANK = 192
MLA_KV_RANK = 128
MLA_NOPE = 64
MLA_ROPE = 32
MLA_V = 64
MLA_WIDTH = MLA_HEADS * MLA_V
ROPE_THETA = 10000.0

MIX_WIDTH = SSM_WIDTH + SWA_WIDTH + MLA_WIDTH
IN_WIDTH = SSM_WIDTH + SWA_WIDTH + 2 * SWA_KV_WIDTH + MLA_Q_RANK + MLA_KV_RANK + MLA_ROPE
IN_SPLITS = (SSM_WIDTH,
             SSM_WIDTH + SWA_WIDTH,
             SSM_WIDTH + SWA_WIDTH + SWA_KV_WIDTH,
             SSM_WIDTH + SWA_WIDTH + 2 * SWA_KV_WIDTH,
             SSM_WIDTH + SWA_WIDTH + 2 * SWA_KV_WIDTH + MLA_Q_RANK,
             SSM_WIDTH + SWA_WIDTH + 2 * SWA_KV_WIDTH + MLA_Q_RANK + MLA_KV_RANK)
D_FF = 4 * D_MODEL
EPS = 1e-6
NEG = -1e30

kernel_name = 'hymba_s5_swa_mla_hybrid'


def rms_norm(x, g):
    xf = x.astype(jnp.float32)
    y = xf * lax.rsqrt(jnp.mean(xf * xf, axis=-1, keepdims=True) + EPS)
    return (y * g.astype(jnp.float32)).astype(x.dtype)


def rope(x, pos):
    half = x.shape[-1] // 2
    inv_freq = ROPE_THETA ** (-jnp.arange(half, dtype=jnp.float32) / half)
    ang = pos.astype(jnp.float32)[:, None] * inv_freq[None, :]
    cos = jnp.cos(ang)[None, :, None, :]
    sin = jnp.sin(ang)[None, :, None, :]
    xf = x.astype(jnp.float32)
    x1, x2 = xf[..., :half], xf[..., half:]
    return jnp.concatenate([x1 * cos - x2 * sin, x1 * sin + x2 * cos], axis=-1).astype(x.dtype)


def _complex_affine_combine(e1, e2):
    a1r, a1i, b1r, b1i = e1
    a2r, a2i, b2r, b2i = e2
    return (a2r * a1r - a2i * a1i,
            a2r * a1i + a2i * a1r,
            a2r * b1r - a2i * b1i + b2r,
            a2r * b1i + a2i * b1r + b2i)


def s5_mixer(u, valid, a_re, a_im, log_step, b_re, b_im, c_re, c_im, d_skip, w_glu, b_glu):
    f32 = jnp.float32
    bsz, L, _ = u.shape
    uf = jnp.where(valid[None, :, None], u, 0).astype(f32).reshape(bsz, L, SSM_GROUPS, SSM_GROUP_CH)
    step = jnp.exp(log_step.astype(f32))[:, None]
    ar, ai = a_re.astype(f32), a_im.astype(f32)
    mag = jnp.exp(ar * step)
    lb_re, lb_im = mag * jnp.cos(ai * step), mag * jnp.sin(ai * step)
    den = ar * ar + ai * ai
    nr, ni = lb_re - 1.0, lb_im
    coef_re = (nr * ar + ni * ai) / den
    coef_im = (ni * ar - nr * ai) / den
    br, bi = b_re.astype(f32), b_im.astype(f32)
    bb_re = coef_re[..., None] * br - coef_im[..., None] * bi
    bb_im = coef_re[..., None] * bi + coef_im[..., None] * br
    bu_re = jnp.einsum('blgc,gpc->blgp', uf, bb_re)
    bu_im = jnp.einsum('blgc,gpc->blgp', uf, bb_im)
    a_t_re = jnp.broadcast_to(lb_re, bu_re.shape)
    a_t_im = jnp.broadcast_to(lb_im, bu_im.shape)
    _, _, xr, xi = lax.associative_scan(_complex_affine_combine, (a_t_re, a_t_im, bu_re, bu_im), axis=1)
    y = (jnp.einsum('gcp,blgp->blgc', c_re.astype(f32), xr)
         - jnp.einsum('gcp,blgp->blgc', c_im.astype(f32), xi)
         + d_skip.astype(f32).reshape(SSM_GROUPS, SSM_GROUP_CH) * uf)
    z = jax.nn.gelu(y.reshape(bsz, L, SSM_WIDTH))
    out = z * jax.nn.sigmoid(z @ w_glu.astype(f32) + b_glu.astype(f32))
    return out.astype(u.dtype)


def swa_attention(q, k, v, sinks):
    f32 = jnp.float32
    bsz, L = q.shape[:2]
    nb = L // BLOCK
    grp = SWA_HEADS // SWA_KV_HEADS
    qb = q.reshape(bsz, nb, BLOCK, SWA_KV_HEADS, grp, HEAD_DIM)
    kb = k.reshape(bsz, nb, BLOCK, SWA_KV_HEADS, HEAD_DIM)
    vb = v.reshape(bsz, nb, BLOCK, SWA_KV_HEADS, HEAD_DIM)

    def with_prev(t):
        prev = jnp.pad(t, ((0, 0), (1, 0), (0, 0), (0, 0), (0, 0)))[:, :-1]
        return jnp.concatenate([prev, t], axis=2)

    kk, vv = with_prev(kb), with_prev(vb)
    s = jnp.einsum('bnqhgd,bnkhd->bnhgqk', qb, kk).astype(f32) * (HEAD_DIM ** -0.5)
    blk = jnp.arange(nb)
    q_idx = blk[:, None] * BLOCK + jnp.arange(BLOCK)[None, :]
    k_idx = (blk[:, None] - 1) * BLOCK + jnp.arange(2 * BLOCK)[None, :]
    dist = q_idx[:, :, None] - k_idx[:, None, :]
    allowed = (dist >= 0) & (dist < WINDOW) & (k_idx[:, None, :] >= PAD)
    slopes = (2.0 ** (-8.0 * jnp.arange(1, SWA_HEADS + 1, dtype=f32) / SWA_HEADS)).reshape(SWA_KV_HEADS, grp)
    s = s - slopes[None, None, :, :, None, None] * dist.astype(f32)[None, :, None, None]
    s = jnp.where(allowed[None, :, None, None], s, NEG)
    sink = sinks.astype(f32).reshape(SWA_KV_HEADS, grp)[None, None, :, :, None, None]
    m = jnp.maximum(jnp.max(s, axis=-1, keepdims=True), sink)
    p = jnp.exp(s - m)
    p = p / (jnp.sum(p, axis=-1, keepdims=True) + jnp.exp(sink - m))
    o = jnp.einsum('bnhgqk,bnkhd->bnqhgd', p.astype(v.dtype), vv)
    return o.reshape(bsz, L, SWA_WIDTH)


def mla_mixer(cq, ckv, kr, pos, q_norm_g, kv_norm_g, w_uq, w_uk, w_uv):
    f32 = jnp.float32
    bsz, L, _ = cq.shape
    nb = L // BLOCK
    q = (rms_norm(cq, q_norm_g) @ w_uq).reshape(bsz, L, MLA_HEADS, MLA_NOPE + MLA_ROPE)
    q_nope, q_rope = q[..., :MLA_NOPE], rope(q[..., MLA_NOPE:], pos)
    c = rms_norm(ckv, kv_norm_g)
    k_nope = (c @ w_uk).reshape(bsz, L, MLA_HEADS, MLA_NOPE)
    v = (c @ w_uv).reshape(bsz, L, MLA_HEADS, MLA_V)
    k_rope = rope(kr[:, :, None, :], pos)[:, :, 0, :]
    scale = (MLA_NOPE + MLA_ROPE) ** -0.5
    k_idx = jnp.arange(L)
    key_valid = k_idx >= PAD
    qn_b = q_nope.reshape(bsz, nb, BLOCK, MLA_HEADS, MLA_NOPE).swapaxes(0, 1)
    qr_b = q_rope.reshape(bsz, nb, BLOCK, MLA_HEADS, MLA_ROPE).swapaxes(0, 1)

    def one_block(args):
        n, qn, qr = args
        s = (jnp.einsum('bqhd,bkhd->bhqk', qn, k_nope)
             + jnp.einsum('bqhr,bkr->bhqk', qr, k_rope)).astype(f32) * scale
        q_idx = n * BLOCK + jnp.arange(BLOCK)
        allowed = (k_idx[None, :] <= q_idx[:, None]) & key_valid[None, :]
        s = jnp.where(allowed[None, None], s, NEG)
        p = jax.nn.softmax(s, axis=-1)
        return jnp.einsum('bhqk,bkhd->bqhd', p.astype(v.dtype), v)

    o = lax.map(one_block, (jnp.arange(nb), qn_b, qr_b))
    return o.swapaxes(0, 1).reshape(bsz, L, MLA_WIDTH)


def setup_inputs(seed: int = 0) -> dict:
    key = jax.random.key(seed)
    ks = jax.random.split(key, 32)
    f32 = jnp.float32
    nrm = lambda k, shape, s: jax.random.normal(k, shape, f32) * s
    gain = lambda k, shape: 1.0 + 0.02 * jax.random.normal(k, shape, f32)
    n_idx = jnp.arange(SSM_STATE, dtype=f32)
    return {
        'x': nrm(ks[0], (BATCH, SEQ, D_MODEL), 1.0),
        'meta_tokens': nrm(ks[1], (N_META, D_MODEL), 1.0),
        'norm_pre_mix': gain(ks[2], (DEPTH, D_MODEL)),
        'norm_post_mix': gain(ks[3], (DEPTH, D_MODEL)),
        'norm_pre_mlp': gain(ks[4], (DEPTH, D_MODEL)),
        'norm_post_mlp': gain(ks[5], (DEPTH, D_MODEL)),
        'w_in': nrm(ks[6], (DEPTH, D_MODEL, IN_WIDTH), D_MODEL ** -0.5),
        'w_out': nrm(ks[7], (DEPTH, MIX_WIDTH, D_MODEL), MIX_WIDTH ** -0.5),
        'norm_heads': gain(ks[8], (DEPTH, MIX_WIDTH)),
        'ssm_a_re': -0.5 + 0.01 * jax.random.normal(ks[9], (DEPTH, SSM_GROUPS, SSM_STATE), f32),
        'ssm_a_im': math.pi * n_idx[None, None, :] + 0.01 * jax.random.normal(ks[10], (DEPTH, SSM_GROUPS, SSM_STATE), f32),
        'ssm_log_step': jax.random.uniform(ks[11], (DEPTH, SSM_GROUPS), f32, math.log(1e-3), math.log(1e-1)),
        'ssm_b_re': nrm(ks[12], (DEPTH, SSM_GROUPS, SSM_STATE, SSM_GROUP_CH), (2 * SSM_GROUP_CH) ** -0.5),
        'ssm_b_im': nrm(ks[13], (DEPTH, SSM_GROUPS, SSM_STATE, SSM_GROUP_CH), (2 * SSM_GROUP_CH) ** -0.5),
        'ssm_c_re': nrm(ks[14], (DEPTH, SSM_GROUPS, SSM_GROUP_CH, SSM_STATE), SSM_STATE ** -0.5),
        'ssm_c_im': nrm(ks[15], (DEPTH, SSM_GROUPS, SSM_GROUP_CH, SSM_STATE), SSM_STATE ** -0.5),
        'ssm_d': nrm(ks[16], (DEPTH, SSM_WIDTH), 1.0),
        'ssm_w_glu': nrm(ks[17], (DEPTH, SSM_WIDTH, SSM_WIDTH), SSM_WIDTH ** -0.5),
        'ssm_b_glu': nrm(ks[18], (DEPTH, SSM_WIDTH), 0.01),
        'swa_sinks': nrm(ks[19], (DEPTH, SWA_HEADS), 0.5),
        'mla_q_norm': gain(ks[20], (DEPTH, MLA_Q_RANK)),
        'mla_kv_norm': gain(ks[21], (DEPTH, MLA_KV_RANK)),
        'mla_w_uq': nrm(ks[22], (DEPTH, MLA_Q_RANK, MLA_HEADS * (MLA_NOPE + MLA_ROPE)), MLA_Q_RANK ** -0.5),
        'mla_w_uk': nrm(ks[23], (DEPTH, MLA_KV_RANK, MLA_HEADS * MLA_NOPE), MLA_KV_RANK ** -0.5),
        'mla_w_uv': nrm(ks[24], (DEPTH, MLA_KV_RANK, MLA_HEADS * MLA_V), MLA_KV_RANK ** -0.5),
        'w_mlp_up': nrm(ks[25], (DEPTH, D_MODEL, D_FF), D_MODEL ** -0.5),
        'w_mlp_down': nrm(ks[26], (DEPTH, D_FF, D_MODEL), D_FF ** -0.5),
    }


def reference(x, meta_tokens, norm_pre_mix, norm_post_mix, norm_pre_mlp, norm_post_mlp, w_in, w_out,
              norm_heads, ssm_a_re, ssm_a_im, ssm_log_step, ssm_b_re, ssm_b_im, ssm_c_re, ssm_c_im,
              ssm_d, ssm_w_glu, ssm_b_glu, swa_sinks, mla_q_norm, mla_kv_norm, mla_w_uq, mla_w_uk,
              mla_w_uv, w_mlp_up, w_mlp_down):
    bsz, _, d = x.shape
    dt = x.dtype
    meta = jnp.broadcast_to(meta_tokens.astype(dt)[None], (bsz, N_META, d))
    h = jnp.concatenate([jnp.zeros((bsz, PAD, d), dt), meta, x], axis=1)
    L = h.shape[1]
    idx = jnp.arange(L)
    valid = idx >= PAD
    pos = idx - PAD
    for l in range(DEPTH):
        a = rms_norm(h, norm_pre_mix[l])
        proj = a @ w_in[l]
        u, q, k, v, cq, ckv, kr = jnp.split(proj, IN_SPLITS, axis=-1)
        y_ssm = s5_mixer(u, valid, ssm_a_re[l], ssm_a_im[l], ssm_log_step[l], ssm_b_re[l], ssm_b_im[l],
                         ssm_c_re[l], ssm_c_im[l], ssm_d[l], ssm_w_glu[l], ssm_b_glu[l])
        y_swa = swa_attention(q.reshape(bsz, L, SWA_HEADS, HEAD_DIM),
                              k.reshape(bsz, L, SWA_KV_HEADS, HEAD_DIM),
                              v.reshape(bsz, L, SWA_KV_HEADS, HEAD_DIM), swa_sinks[l])
        y_mla = mla_mixer(cq, ckv, kr, pos, mla_q_norm[l], mla_kv_norm[l], mla_w_uq[l], mla_w_uk[l], mla_w_uv[l])
        g = norm_heads[l]
        mixed = jnp.concatenate([
            rms_norm(y_ssm, g[:SSM_WIDTH]),
            rms_norm(y_swa, g[SSM_WIDTH:SSM_WIDTH + SWA_WIDTH]),
            rms_norm(y_mla, g[SSM_WIDTH + SWA_WIDTH:]),
        ], axis=-1)
        h = h + rms_norm(mixed @ w_out[l], norm_post_mix[l])
        a = rms_norm(h, norm_pre_mlp[l])
        f = jnp.square(jax.nn.relu(a @ w_mlp_up[l])) @ w_mlp_down[l]
        h = h + rms_norm(f, norm_post_mlp[l])
    return h[:, PAD + N_META:, :]
```

```python
import functools
import math

import jax
import jax.numpy as jnp
from jax import lax
from jax.experimental import pallas as pl
from jax.experimental.pallas import tpu as pltpu

F32 = jnp.float32
BF16 = jnp.bfloat16

D_MODEL = 1024
N_META = 16
BLOCK = 128
PAD = BLOCK - N_META
SSM_GROUPS = 16
SSM_GROUP_CH = 16
SSM_WIDTH = SSM_GROUPS * SSM_GROUP_CH
SSM_STATE = 64
SWA_HEADS = 8
SWA_KV_HEADS = 2
SWA_GROUP = SWA_HEADS // SWA_KV_HEADS
HEAD_DIM = 64
SWA_WIDTH = SWA_HEADS * HEAD_DIM
SWA_KV_WIDTH = SWA_KV_HEADS * HEAD_DIM
MLA_HEADS = 4
MLA_Q_RANK = 192
MLA_KV_RANK = 128
MLA_NOPE = 64
MLA_ROPE = 32
MLA_V = 64
MLA_WIDTH = MLA_HEADS * MLA_V
MLA_QK = MLA_KV_RANK + MLA_ROPE
ROPE_THETA = 10000.0
D_FF = 4 * D_MODEL
EPS = 1e-6
NEG = -1e30

SSM_CHUNK = 8
SSM_FOLD = SSM_CHUNK * SSM_WIDTH
SSM_MODES = SSM_GROUPS * SSM_STATE

C_U, C_Q, C_K, C_CQ, C_CKV, C_KR, C_END = 0, 256, 768, 1024, 1280, 1408, 1536

VMEM_LIMIT = 56 * 1024 * 1024


def _rms(x, g, n=None):
    n = x.shape[-1] if n is None else n
    ms = jnp.sum(x * x, axis=-1, keepdims=True) * (1.0 / n)
    return x * lax.rsqrt(ms + EPS) * g


def _const_spec(shape):
    nd = len(shape)
    return pl.BlockSpec(shape, lambda *_: (0,) * nd, pipeline_mode=pl.Buffered(1))


def _mix_in_kernel(h_ref, g_ref, w1_ref, gq_ref, gkv_ref, wuq_ref, wuk_ref, cos_ref, sin_ref,
                   u_ref, q_ref, kv_ref, qm_ref, kc_ref, *, tm):
    j = pl.program_id(1)
    a = _rms(h_ref[0], g_ref[...]).astype(BF16)
    proj = jnp.dot(a, w1_ref[...], preferred_element_type=F32)
    row = j * tm + lax.broadcasted_iota(jnp.int32, (tm, 1), 0)
    u_ref[0] = jnp.where(row >= PAD, proj[:, C_U:C_Q], 0.0)
    q_ref[0] = (proj[:, C_Q:C_K] * (HEAD_DIM ** -0.5)).astype(BF16)
    kv_ref[0] = proj[:, C_K:C_CQ].astype(BF16)

    scale = (MLA_NOPE + MLA_ROPE) ** -0.5
    cos = cos_ref[...]
    sin = sin_ref[...]
    qn = _rms(proj[:, C_CQ:C_CKV], gq_ref[...], n=MLA_Q_RANK).astype(BF16)
    q2 = jnp.dot(qn, wuq_ref[...], preferred_element_type=F32)
    qr = (q2[:, 256:384] * cos + q2[:, 384:512] * sin) * scale
    qa = jnp.dot(q2[:, 0:256].astype(BF16), wuk_ref[...], preferred_element_type=F32) * scale
    c = _rms(proj[:, C_CKV:C_KR], gkv_ref[...])
    krr = proj[:, C_KR:C_END]
    kr = krr[:, 0:MLA_ROPE] * cos[:, 0:MLA_ROPE] + krr[:, MLA_ROPE:2 * MLA_ROPE] * sin[:, 0:MLA_ROPE]
    for hd in range(MLA_HEADS):
        qm_ref[0, hd, :, 0:MLA_KV_RANK] = qa[:, hd * MLA_KV_RANK:(hd + 1) * MLA_KV_RANK].astype(BF16)
        qm_ref[0, hd, :, MLA_KV_RANK:MLA_QK] = qr[:, hd * MLA_ROPE:(hd + 1) * MLA_ROPE].astype(BF16)
    kc_ref[0, :, 0:MLA_KV_RANK] = c.astype(BF16)
    kc_ref[0, :, MLA_KV_RANK:MLA_QK] = kr.astype(BF16)


def _mix_in(h, g, w1, gq, gkv, wuq, wuk, cos_t, sin_t, *, tm=384):
    bsz, seq, _ = h.shape
    grid = (bsz, seq // tm)
    row3 = lambda b, j: (b, j, 0)
    return pl.pallas_call(
        functools.partial(_mix_in_kernel, tm=tm),
        grid=grid,
        in_specs=[
            pl.BlockSpec((1, tm, D_MODEL), row3),
            _const_spec((1, D_MODEL)),
            _const_spec((D_MODEL, C_END)),
            _const_spec((1, 256)),
            _const_spec((1, MLA_KV_RANK)),
            _const_spec((256, 512)),
            _const_spec((256, 512)),
            pl.BlockSpec((tm, 128), lambda b, j: (j, 0)),
            pl.BlockSpec((tm, 128), lambda b, j: (j, 0)),
        ],
        out_specs=[
            pl.BlockSpec((1, tm, SSM_WIDTH), row3),
            pl.BlockSpec((1, tm, SWA_WIDTH), row3),
            pl.BlockSpec((1, tm, 2 * SWA_KV_WIDTH), row3),
            pl.BlockSpec((1, MLA_HEADS, tm, MLA_QK), lambda b, j: (b, 0, j, 0)),
            pl.BlockSpec((1, tm, MLA_QK), row3),
        ],
        out_shape=[
            jax.ShapeDtypeStruct((bsz, seq, SSM_WIDTH), F32),
            jax.ShapeDtypeStruct((bsz, seq, SWA_WIDTH), BF16),
            jax.ShapeDtypeStruct((bsz, seq, 2 * SWA_KV_WIDTH), BF16),
            jax.ShapeDtypeStruct((bsz, MLA_HEADS, seq, MLA_QK), BF16),
            jax.ShapeDtypeStruct((bsz, seq, MLA_QK), BF16),
        ],
        compiler_params=pltpu.CompilerParams(
            dimension_semantics=("parallel", "parallel"), vmem_limit_bytes=VMEM_LIMIT),
        name="mix_in",
    )(h, g, w1, gq, gkv, wuq, wuk, cos_t, sin_t)


def _split_bf16(x):
    hi = x.astype(BF16)
    lo = (x - hi.astype(F32)).astype(BF16)
    return hi, lo


def _bdot3(a, b):
    ah, al = _split_bf16(a)
    bh, bl = _split_bf16(b)
    dn = (((2,), (1,)), ((0,), (0,)))
    f = lambda x, y: lax.dot_general(x, y, dn, preferred_element_type=F32)
    return f(ah, bh) + f(ah, bl) + f(al, bh)


def _s5_param_kernel(arp_ref, aip_ref, lsp_ref, arc_ref, aic_ref, lsc_ref, br_ref, bi_ref, cr_ref, ci_ref,
                     kk_ref, pr_ref, pi_ref, clr_ref, cli_ref, l8r_ref, l8i_ref):
    def lam_pow(ar, ai, step, k):
        mag = jnp.exp((k * ar) * step)
        ang = (k * ai) * step
        return mag * jnp.cos(ang), mag * jnp.sin(ang)

    ar, ai = arp_ref[0], aip_ref[0]
    step = jnp.exp(lsp_ref[0])
    lr, li = lam_pow(ar, ai, step, 1.0)
    den = ar * ar + ai * ai
    nr, ni = lr - 1.0, li
    coef_re = (nr * ar + ni * ai) / den
    coef_im = (ni * ar - nr * ai) / den
    br, bi = br_ref[0], bi_ref[0]
    bb_re = coef_re * br - coef_im * bi
    bb_im = coef_re * bi + coef_im * br
    for s in range(SSM_CHUNK):
        pr_, pi_ = lam_pow(ar, ai, step, float(SSM_CHUNK - 1 - s))
        pr_ref[0, s] = pr_ * bb_re - pi_ * bb_im
        pi_ref[0, s] = pr_ * bb_im + pi_ * bb_re

    arc, aic = arc_ref[0], aic_ref[0]
    stepc = jnp.exp(lsc_ref[0])
    cr, ci = cr_ref[0], ci_ref[0]
    for k in range(SSM_CHUNK + 1):
        qr_, qi_ = lam_pow(arc, aic, stepc, float(k))
        clr = cr * qr_ - ci * qi_
        cli = cr * qi_ + ci * qr_
        if k < SSM_CHUNK:
            kk_ref[0, k] = _bdot3(clr, bb_re) - _bdot3(cli, bb_im)
        if k >= 1:
            clr_ref[0, k - 1] = clr
            cli_ref[0, k - 1] = cli
    l8r, l8i = lam_pow(arc, aic, stepc, float(SSM_CHUNK))
    l8r_ref[0] = l8r
    l8i_ref[0] = l8i


def _s5_params(a_re, a_im, log_step, b_re, b_im, c_re, c_im):
    depth = a_re.shape[0]
    G, P, C, T = SSM_GROUPS, SSM_STATE, SSM_GROUP_CH, SSM_CHUNK
    lay = lambda *shape: pl.BlockSpec((1,) + shape, lambda l: (l,) + (0,) * len(shape))
    ls = jnp.broadcast_to(log_step[:, :, None], (depth, G, P))
    args = (a_re[..., None], a_im[..., None], ls[..., None],
            a_re[:, :, None, :], a_im[:, :, None, :], ls[:, :, None, :],
            b_re, b_im, c_re, c_im)
    return pl.pallas_call(
        _s5_param_kernel,
        grid=(depth,),
        in_specs=[lay(G, P, 1)] * 3 + [lay(G, 1, P)] * 3 + [lay(G, P, C)] * 2 + [lay(G, C, P)] * 2,
        out_specs=[lay(T, G, C, C), lay(T, G, P, C), lay(T, G, P, C), lay(T, G, C, P), lay(T, G, C, P),
                   lay(G, 1, P), lay(G, 1, P)],
        out_shape=[
            jax.ShapeDtypeStruct((depth, T, G, C, C), F32),
            jax.ShapeDtypeStruct((depth, T, G, P, C), F32),
            jax.ShapeDtypeStruct((depth, T, G, P, C), F32),
            jax.ShapeDtypeStruct((depth, T, G, C, P), F32),
            jax.ShapeDtypeStruct((depth, T, G, C, P), F32),
            jax.ShapeDtypeStruct((depth, G, 1, P), F32),
            jax.ShapeDtypeStruct((depth, G, 1, P), F32),
        ],
        compiler_params=pltpu.CompilerParams(dimension_semantics=("parallel",)),
        name="s5_params",
    )(*args)


def _group_block_diag(x):
    *lead, G, r, c = x.shape
    eye = jnp.eye(G, dtype=x.dtype)
    y = x[..., :, :, None, :] * eye[:, None, :, None]
    return y.reshape(*lead, G * r, G * c)


def _s5_matrices(kk, pr, pi, clr, cli, l8r, l8i):
    T = SSM_CHUNK
    kb = _group_block_diag(jnp.swapaxes(kk, -1, -2))
    kb = jnp.concatenate([kb, jnp.zeros_like(kb[:1])], axis=0)
    lag = jnp.arange(T)[None, :] - jnp.arange(T)[:, None]
    r4 = kb[jnp.where(lag >= 0, lag, T)]
    r_mat = r4.transpose(0, 2, 1, 3).reshape(SSM_FOLD, SSM_FOLD)
    p_re = _group_block_diag(jnp.swapaxes(pr, -1, -2)).reshape(SSM_FOLD, SSM_MODES)
    p_im = _group_block_diag(jnp.swapaxes(pi, -1, -2)).reshape(SSM_FOLD, SSM_MODES)
    p_mat = jnp.concatenate([p_re, p_im], axis=1)
    q_re = _group_block_diag(jnp.swapaxes(clr, -1, -2)).transpose(1, 0, 2).reshape(SSM_MODES, SSM_FOLD)
    q_im = _group_block_diag(jnp.swapaxes(cli, -1, -2)).transpose(1, 0, 2).reshape(SSM_MODES, SSM_FOLD)
    q_mat = jnp.concatenate([q_re, -q_im], axis=0)
    lam = jnp.concatenate([l8r.reshape(1, SSM_MODES), l8i.reshape(1, SSM_MODES)], axis=1)
    return r_mat.astype(BF16), p_mat.astype(BF16), q_mat.astype(BF16), lam


def _s5_state_kernel(u_ref, p_ref, s_ref):
    s_ref[0] = jnp.dot(u_ref[0].astype(BF16), p_ref[...], preferred_element_type=F32)


def _s5_scan_kernel(s_ref, lam_ref, x_ref, *, n_chunks):
    lr = lam_ref[:, 0:SSM_MODES]
    li = lam_ref[:, SSM_MODES:2 * SSM_MODES]

    def body(n, carry):
        xr, xi = carry
        x_ref[0, pl.ds(n, 1), 0:SSM_MODES] = xr
        x_ref[0, pl.ds(n, 1), SSM_MODES:2 * SSM_MODES] = xi
        sr = s_ref[0, pl.ds(n, 1), 0:SSM_MODES]
        si = s_ref[0, pl.ds(n, 1), SSM_MODES:2 * SSM_MODES]
        return lr * xr - li * xi + sr, lr * xi + li * xr + si

    z = jnp.zeros((1, SSM_MODES), F32)
    lax.fori_loop(0, n_chunks, body, (z, z))


def _s5_out_kernel(u_ref, x_ref, ucol_ref, r_ref, q_ref, d_ref, y_ref):
    y = jnp.dot(u_ref[0].astype(BF16), r_ref[...], preferred_element_type=F32)
    y += jnp.dot(x_ref[0].astype(BF16), q_ref[...], preferred_element_type=F32)
    y_ref[0] = y + d_ref[...] * ucol_ref[0]


def _s5_sequence(u_fold, r_mat, p_mat, q_mat, lam, d_fold, *, tn=1024):
    bsz, n_chunks, _ = u_fold.shape
    nt = SSM_FOLD // tn
    params = pltpu.CompilerParams(dimension_semantics=("parallel", "parallel"), vmem_limit_bytes=VMEM_LIMIT)
    rows = pl.BlockSpec((1, n_chunks, SSM_FOLD), lambda b, n: (b, 0, 0))
    cols = pl.BlockSpec((1, n_chunks, tn), lambda b, n: (b, 0, n))
    wcol = pl.BlockSpec((SSM_FOLD, tn), lambda b, n: (0, n))
    s = pl.pallas_call(
        _s5_state_kernel, grid=(bsz, nt), in_specs=[rows, wcol], out_specs=cols,
        out_shape=jax.ShapeDtypeStruct((bsz, n_chunks, 2 * SSM_MODES), F32),
        compiler_params=params, name="s5_state",
    )(u_fold, p_mat)
    x_prev = pl.pallas_call(
        functools.partial(_s5_scan_kernel, n_chunks=n_chunks), grid=(bsz,),
        in_specs=[pl.BlockSpec((1, n_chunks, 2 * SSM_MODES), lambda b: (b, 0, 0)),
                  pl.BlockSpec((1, 2 * SSM_MODES), lambda b: (0, 0))],
        out_specs=pl.BlockSpec((1, n_chunks, 2 * SSM_MODES), lambda b: (b, 0, 0)),
        out_shape=jax.ShapeDtypeStruct((bsz, n_chunks, 2 * SSM_MODES), F32),
        compiler_params=pltpu.CompilerParams(dimension_semantics=("parallel",), vmem_limit_bytes=VMEM_LIMIT),
        name="s5_scan",
    )(s, lam)
    return pl.pallas_call(
        _s5_out_kernel, grid=(bsz, nt),
        in_specs=[rows, rows, cols, wcol, wcol, pl.BlockSpec((1, tn), lambda b, n: (0, n))],
        out_specs=cols,
        out_shape=jax.ShapeDtypeStruct((bsz, n_chunks, SSM_FOLD), F32),
        compiler_params=params, name="s5_out",
    )(u_fold, x_prev, u_fold, r_mat, q_mat, d_fold)


def _swa_kernel(q_ref, kvp_ref, kvc_ref, slope_ref, sink_ref, o_ref):
    i = pl.program_id(1)
    q = q_ref[0]
    kvp = kvp_ref[0]
    kvc = kvc_ref[0]
    rows = SWA_GROUP * BLOCK
    t = lax.broadcasted_iota(jnp.int32, (rows, BLOCK), 0) & (BLOCK - 1)
    jl = lax.broadcasted_iota(jnp.int32, (rows, BLOCK), 1)
    cur = jl <= t
    d0 = t - jl
    dist = jnp.where(cur, d0, d0 + BLOCK).astype(F32)
    kabs = jnp.where(cur, i * BLOCK + jl, (i - 1) * BLOCK + jl)
    valid = kabs >= PAD
    for hk in range(SWA_KV_HEADS):
        ks = slice(hk * HEAD_DIM, (hk + 1) * HEAD_DIM)
        vs = slice(SWA_KV_WIDTH + hk * HEAD_DIM, SWA_KV_WIDTH + (hk + 1) * HEAD_DIM)
        kcat = jnp.concatenate([kvp[:, ks], kvc[:, ks]], axis=0)
        vcat = jnp.concatenate([kvp[:, vs], kvc[:, vs]], axis=0)
        qs = jnp.concatenate(
            [q[:, (hk * SWA_GROUP + g) * HEAD_DIM:(hk * SWA_GROUP + g + 1) * HEAD_DIM] for g in range(SWA_GROUP)],
            axis=0)
        sf = lax.dot_general(qs, kcat, (((1,), (1,)), ((), ())), preferred_element_type=F32)
        s = jnp.where(cur, sf[:, BLOCK:2 * BLOCK], sf[:, 0:BLOCK])
        slope = slope_ref[hk]
        sink = sink_ref[hk]
        s = jnp.where(valid, s - slope * dist, NEG)
        m = jnp.maximum(jnp.max(s, axis=-1, keepdims=True), sink)
        p = jnp.exp(s - m)
        den = jnp.sum(p, axis=-1, keepdims=True) + jnp.exp(sink - m)
        pf = jnp.concatenate([jnp.where(cur, 0.0, p), jnp.where(cur, p, 0.0)], axis=1).astype(BF16)
        o = jnp.dot(pf, vcat, preferred_element_type=F32) / den
        for g in range(SWA_GROUP):
            hh = hk * SWA_GROUP + g
            o_ref[0, :, hh * HEAD_DIM:(hh + 1) * HEAD_DIM] = o[g * BLOCK:(g + 1) * BLOCK]


def _swa(q, kv, slope_rows, sink_rows):
    bsz, seq, _ = q.shape
    nb = seq // BLOCK
    rows = SWA_GROUP * BLOCK
    return pl.pallas_call(
        _swa_kernel,
        grid=(bsz, nb),
        in_specs=[
            pl.BlockSpec((1, BLOCK, SWA_WIDTH), lambda b, i: (b, i, 0)),
            pl.BlockSpec((1, BLOCK, 2 * SWA_KV_WIDTH), lambda b, i: (b, jnp.maximum(i - 1, 0), 0)),
            pl.BlockSpec((1, BLOCK, 2 * SWA_KV_WIDTH), lambda b, i: (b, i, 0)),
            _const_spec((SWA_KV_HEADS, rows, 1)),
            _const_spec((SWA_KV_HEADS, rows, 1)),
        ],
        out_specs=pl.BlockSpec((1, BLOCK, SWA_WIDTH), lambda b, i: (b, i, 0)),
        out_shape=jax.ShapeDtypeStruct((bsz, seq, SWA_WIDTH), F32),
        compiler_params=pltpu.CompilerParams(dimension_semantics=("parallel", "parallel")),
        name="swa",
    )(q, kv, kv, slope_rows, sink_rows)


def _mla_kernel(q_ref, k_ref, wuv_ref, o_ref, m_sc, l_sc, acc_sc, *, tq, tk):
    i = pl.program_id(1)
    rows = MLA_HEADS * tq
    q = q_ref[0].reshape(rows, MLA_QK)
    m_sc[...] = jnp.full((rows, 1), NEG, F32)
    l_sc[...] = jnp.zeros((rows, 1), F32)
    acc_sc[...] = jnp.zeros((rows, MLA_KV_RANK), F32)
    qpos = i * tq + (lax.broadcasted_iota(jnp.int32, (rows, tk), 0) & (tq - 1))
    col = lax.broadcasted_iota(jnp.int32, (rows, tk), 1)
    n_tiles = ((i + 1) * tq + tk - 1) // tk

    def body(j, carry):
        kt = k_ref[0, pl.ds(pl.multiple_of(j * tk, tk), tk), :]
        s = lax.dot_general(q, kt, (((1,), (1,)), ((), ())), preferred_element_type=F32)
        kidx = col + j * tk
        s = jnp.where((kidx <= qpos) & (kidx >= PAD), s, NEG)
        m_prev = m_sc[...]
        m_new = jnp.maximum(m_prev, jnp.max(s, axis=-1, keepdims=True))
        alpha = jnp.exp(m_prev - m_new)
        p = jnp.exp(s - m_new)
        l_sc[...] = alpha * l_sc[...] + jnp.sum(p, axis=-1, keepdims=True)
        acc_sc[...] = alpha * acc_sc[...] + jnp.dot(
            p.astype(BF16), kt[:, 0:MLA_KV_RANK], preferred_element_type=F32)
        m_sc[...] = m_new
        return carry

    lax.fori_loop(0, n_tiles, body, 0)
    o = acc_sc[...] / l_sc[...]
    ocat = jnp.concatenate([o[hd * tq:(hd + 1) * tq] for hd in range(MLA_HEADS)], axis=1).astype(BF16)
    o_ref[0] = jnp.dot(ocat, wuv_ref[...], preferred_element_type=F32)


def _mla(qm, kc_padded, wuv_bd, *, tq=128, tk=256):
    bsz, _, seq, _ = qm.shape
    seq_k = kc_padded.shape[1]
    rows = MLA_HEADS * tq
    return pl.pallas_call(
        functools.partial(_mla_kernel, tq=tq, tk=tk),
        grid=(bsz, seq // tq),
        in_specs=[
            pl.BlockSpec((1, MLA_HEADS, tq, MLA_QK), lambda b, i: (b, 0, i, 0)),
            pl.BlockSpec((1, seq_k, MLA_QK), lambda b, i: (b, 0, 0)),
            _const_spec((MLA_HEADS * MLA_KV_RANK, MLA_WIDTH)),
        ],
        out_specs=pl.BlockSpec((1, tq, MLA_WIDTH), lambda b, i: (b, i, 0)),
        out_shape=jax.ShapeDtypeStruct((bsz, seq, MLA_WIDTH), F32),
        scratch_shapes=[pltpu.VMEM((rows, 1), F32), pltpu.VMEM((rows, 1), F32),
                        pltpu.VMEM((rows, MLA_KV_RANK), F32)],
        compiler_params=pltpu.CompilerParams(
            dimension_semantics=("parallel", "arbitrary"), vmem_limit_bytes=VMEM_LIMIT),
        name="mla",
    )(qm, kc_padded, wuv_bd)


def _gelu_tanh(x):
    return 0.5 * x * (1.0 + jnp.tanh(math.sqrt(2.0 / math.pi) * (x + 0.044715 * (x * x * x))))


def _mix_out_kernel(h_ref, ys_ref, yw_ref, ym_ref, wglu_ref, bglu_ref, gh_ref, wout_ref, gpm_ref, gpre_ref,
                    wup_ref, wdn_ref, gpost_ref, o_ref, *, ff_chunk):
    z = _gelu_tanh(ys_ref[...])
    gate = jnp.dot(z.astype(BF16), wglu_ref[...], preferred_element_type=F32) + bglu_ref[...]
    y_ssm = z * (1.0 / (1.0 + jnp.exp(-gate)))
    gh = gh_ref[...]
    mixed = jnp.concatenate([
        _rms(y_ssm, gh[:, 0:SSM_WIDTH]),
        _rms(yw_ref[...], gh[:, SSM_WIDTH:SSM_WIDTH + SWA_WIDTH]),
        _rms(ym_ref[...], gh[:, SSM_WIDTH + SWA_WIDTH:]),
    ], axis=1).astype(BF16)
    mo = jnp.dot(mixed, wout_ref[...], preferred_element_type=F32)
    h1 = h_ref[...] + _rms(mo, gpm_ref[...])
    a = _rms(h1, gpre_ref[...]).astype(BF16)
    f = jnp.zeros_like(h1)
    for c in range(D_FF // ff_chunk):
        up = jnp.dot(a, wup_ref[:, c * ff_chunk:(c + 1) * ff_chunk], preferred_element_type=F32)
        hid = jnp.square(jnp.maximum(up, 0.0)).astype(BF16)
        f += jnp.dot(hid, wdn_ref[c * ff_chunk:(c + 1) * ff_chunk, :], preferred_element_type=F32)
    o_ref[...] = h1 + _rms(f, gpost_ref[...])


def _mix_out(h, ys, yw, ym, wglu, bglu, gh, wout, gpm, gpre, wup, wdn, gpost, *, tm=512, ff_chunk=1024):
    n, _ = h.shape
    row = lambda w: pl.BlockSpec((tm, w), lambda i: (i, 0))
    return pl.pallas_call(
        functools.partial(_mix_out_kernel, ff_chunk=ff_chunk),
        grid=(n // tm,),
        in_specs=[
            row(D_MODEL), row(SSM_WIDTH), row(SWA_WIDTH), row(MLA_WIDTH),
            _const_spec((SSM_WIDTH, SSM_WIDTH)), _const_spec((1, SSM_WIDTH)), _const_spec((1, D_MODEL)),
            _const_spec((D_MODEL, D_MODEL)), _const_spec((1, D_MODEL)), _const_spec((1, D_MODEL)),
            _const_spec((D_MODEL, D_FF)), _const_spec((D_FF, D_MODEL)), _const_spec((1, D_MODEL)),
        ],
        out_specs=row(D_MODEL),
        out_shape=jax.ShapeDtypeStruct((n, D_MODEL), F32),
        compiler_params=pltpu.CompilerParams(dimension_semantics=("parallel",), vmem_limit_bytes=VMEM_LIMIT),
        name="mix_out",
    )(h, ys, yw, ym, wglu, bglu, gh, wout, gpm, gpre, wup, wdn, gpost)


def _rot_half_cols(w):
    half = w.shape[-1] // 2
    return jnp.concatenate([-w[..., half:], w[..., :half]], axis=-1)


def _layer_weights(w_in, w_uq, w_uk, w_uv):
    z = lambda n: jnp.zeros((D_MODEL, n), w_in.dtype)
    s_cq = SSM_WIDTH + SWA_WIDTH + 2 * SWA_KV_WIDTH
    s_ckv = s_cq + MLA_Q_RANK
    s_kr = s_ckv + MLA_KV_RANK
    kr = w_in[:, s_kr:s_kr + MLA_ROPE]
    w1 = jnp.concatenate([
        w_in[:, :s_cq], w_in[:, s_cq:s_ckv], z(C_CKV - C_CQ - MLA_Q_RANK),
        w_in[:, s_ckv:s_kr], kr, _rot_half_cols(kr), z(C_END - C_KR - 2 * MLA_ROPE)], axis=1).astype(BF16)
    uq = w_uq.reshape(MLA_Q_RANK, MLA_HEADS, MLA_NOPE + MLA_ROPE)
    rope = uq[:, :, MLA_NOPE:]
    wuq = jnp.concatenate([
        uq[:, :, :MLA_NOPE].reshape(MLA_Q_RANK, -1), rope.reshape(MLA_Q_RANK, -1),
        _rot_half_cols(rope).reshape(MLA_Q_RANK, -1)], axis=1)
    wuq = jnp.concatenate([wuq, jnp.zeros((256 - MLA_Q_RANK, wuq.shape[1]), wuq.dtype)], axis=0).astype(BF16)
    uk = w_uk.reshape(MLA_KV_RANK, MLA_HEADS, MLA_NOPE).transpose(1, 2, 0)
    wuk_bd = _group_block_diag(uk).astype(BF16)
    uv = w_uv.reshape(MLA_KV_RANK, MLA_HEADS, MLA_V).transpose(1, 0, 2)
    wuv_bd = _group_block_diag(uv).astype(BF16)
    return w1, wuq, wuk_bd, wuv_bd


def kernel(x, meta_tokens, norm_pre_mix, norm_post_mix, norm_pre_mlp, norm_post_mlp, w_in, w_out, norm_heads,
           ssm_a_re, ssm_a_im, ssm_log_step, ssm_b_re, ssm_b_im, ssm_c_re, ssm_c_im, ssm_d, ssm_w_glu,
           ssm_b_glu, swa_sinks, mla_q_norm, mla_kv_norm, mla_w_uq, mla_w_uk, mla_w_uv, w_mlp_up, w_mlp_down):
    bsz, seq_real, d = x.shape
    depth = w_in.shape[0]
    meta = jnp.broadcast_to(meta_tokens.astype(x.dtype)[None], (bsz, N_META, d))
    h = jnp.concatenate([jnp.zeros((bsz, PAD, d), x.dtype), meta, x], axis=1)
    seq = h.shape[1]
    n_chunks = seq // SSM_CHUNK

    half = MLA_ROPE // 2
    inv_freq = ROPE_THETA ** (-jnp.arange(half, dtype=F32) / half)
    ang = (jnp.arange(seq) - PAD).astype(F32)[:, None] * inv_freq[None, :]
    cos_t = jnp.tile(jnp.cos(ang), (1, 2 * MLA_HEADS))
    sin_t = jnp.tile(jnp.sin(ang), (1, 2 * MLA_HEADS))
    slopes = 2.0 ** (-8.0 * jnp.arange(1, SWA_HEADS + 1, dtype=F32) / SWA_HEADS)
    per_row = lambda v: jnp.repeat(v.reshape(SWA_KV_HEADS, SWA_GROUP), BLOCK, axis=1)[..., None]
    slope_rows = per_row(slopes)

    s5p = _s5_params(ssm_a_re, ssm_a_im, ssm_log_step, ssm_b_re, ssm_b_im, ssm_c_re, ssm_c_im)
    row = lambda v: v.reshape(1, -1).astype(F32)

    for l in range(depth):
        w1, wuq, wuk_bd, wuv_bd = _layer_weights(w_in[l], mla_w_uq[l], mla_w_uk[l], mla_w_uv[l])
        gq = jnp.concatenate([mla_q_norm[l], jnp.zeros((256 - MLA_Q_RANK,), F32)]).reshape(1, 256)
        u, q, kv, qm, kc = _mix_in(h, row(norm_pre_mix[l]), w1, gq, row(mla_kv_norm[l]), wuq, wuk_bd,
                                   cos_t, sin_t)
        r_mat, p_mat, q_mat, lam = _s5_matrices(*[t[l] for t in s5p])
        d_fold = jnp.tile(ssm_d[l].astype(F32), SSM_CHUNK).reshape(1, SSM_FOLD)
        y_ssm = _s5_sequence(u.reshape(bsz, n_chunks, SSM_FOLD), r_mat, p_mat, q_mat, lam, d_fold)
        y_swa = _swa(q, kv, slope_rows, per_row(swa_sinks[l].astype(F32)))
        kc_padded = jnp.pad(kc, ((0, 0), (0, (-seq) % 256), (0, 0)))
        y_mla = _mla(qm, kc_padded, wuv_bd)
        n = bsz * seq
        h = _mix_out(
            h.reshape(n, d), y_ssm.reshape(n, SSM_WIDTH), y_swa.reshape(n, SWA_WIDTH), y_mla.reshape(n, MLA_WIDTH),
            ssm_w_glu[l].astype(BF16), row(ssm_b_glu[l]), row(norm_heads[l]), w_out[l].astype(BF16),
            row(norm_post_mix[l]), row(norm_pre_mlp[l]), w_mlp_up[l].astype(BF16), w_mlp_down[l].astype(BF16),
            row(norm_post_mlp[l])).reshape(bsz, seq, d)
    return h[:, PAD + N_META:, :]
```

```python
import functools
import math

import jax
import jax.numpy as jnp
from jax import lax
from jax.experimental import pallas as pl
from jax.experimental.pallas import tpu as pltpu

F32 = jnp.float32
BF16 = jnp.bfloat16

D_MODEL = 1024
N_META = 16
BLOCK = 128
PAD = BLOCK - N_META
SSM_GROUPS = 16
SSM_GROUP_CH = 16
SSM_WIDTH = SSM_GROUPS * SSM_GROUP_CH
SSM_STATE = 64
SWA_HEADS = 8
SWA_KV_HEADS = 2
SWA_GROUP = SWA_HEADS // SWA_KV_HEADS
HEAD_DIM = 64
SWA_WIDTH = SWA_HEADS * HEAD_DIM
SWA_KV_WIDTH = SWA_KV_HEADS * HEAD_DIM
MLA_HEADS = 4
MLA_Q_RANK = 192
MLA_KV_RANK = 128
MLA_NOPE = 64
MLA_ROPE = 32
MLA_V = 64
MLA_WIDTH = MLA_HEADS * MLA_V
MLA_QK = MLA_KV_RANK + MLA_ROPE
ROPE_THETA = 10000.0
D_FF = 4 * D_MODEL
EPS = 1e-6
NEG = -1e30

SSM_CHUNK = 8
SSM_FOLD = SSM_CHUNK * SSM_WIDTH
SSM_MODES = SSM_GROUPS * SSM_STATE

C_U, C_Q, C_K, C_CQ, C_CKV, C_KR, C_END = 0, 256, 768, 1024, 1280, 1408, 1536

VMEM_LIMIT = 56 * 1024 * 1024


def _rms(x, g, n=None):
    n = x.shape[-1] if n is None else n
    ms = jnp.sum(x * x, axis=-1, keepdims=True) * (1.0 / n)
    return x * lax.rsqrt(ms + EPS) * g


def _const_spec(shape):
    nd = len(shape)
    return pl.BlockSpec(shape, lambda *_: (0,) * nd, pipeline_mode=pl.Buffered(1))


def _mix_in_kernel(h_ref, g_ref, w1_ref, gq_ref, gkv_ref, wuq_ref, wuk_ref, cos_ref, sin_ref,
                   u_ref, q_ref, kv_ref, qm_ref, kc_ref, *, tm):
    j = pl.program_id(1)
    a = _rms(h_ref[0], g_ref[...]).astype(BF16)
    proj = jnp.dot(a, w1_ref[...], preferred_element_type=F32)
    row = j * tm + lax.broadcasted_iota(jnp.int32, (tm, 1), 0)
    u_ref[0] = jnp.where(row >= PAD, proj[:, C_U:C_Q], 0.0)
    q_ref[0] = (proj[:, C_Q:C_K] * (HEAD_DIM ** -0.5)).astype(BF16)
    kv_ref[0] = proj[:, C_K:C_CQ].astype(BF16)

    scale = (MLA_NOPE + MLA_ROPE) ** -0.5
    cos = cos_ref[...]
    sin = sin_ref[...]
    qn = _rms(proj[:, C_CQ:C_CKV], gq_ref[...], n=MLA_Q_RANK).astype(BF16)
    q2 = jnp.dot(qn, wuq_ref[...], preferred_element_type=F32)
    qr = (q2[:, 256:384] * cos + q2[:, 384:512] * sin) * scale
    qa = jnp.dot(q2[:, 0:256].astype(BF16), wuk_ref[...], preferred_element_type=F32) * scale
    c = _rms(proj[:, C_CKV:C_KR], gkv_ref[...])
    krr = proj[:, C_KR:C_END]
    kr = krr[:, 0:MLA_ROPE] * cos[:, 0:MLA_ROPE] + krr[:, MLA_ROPE:2 * MLA_ROPE] * sin[:, 0:MLA_ROPE]
    for hd in range(MLA_HEADS):
        qm_ref[0, hd, :, 0:MLA_KV_RANK] = qa[:, hd * MLA_KV_RANK:(hd + 1) * MLA_KV_RANK].astype(BF16)
        qm_ref[0, hd, :, MLA_KV_RANK:MLA_QK] = qr[:, hd * MLA_ROPE:(hd + 1) * MLA_ROPE].astype(BF16)
    kc_ref[0, :, 0:MLA_KV_RANK] = c.astype(BF16)
    kc_ref[0, :, MLA_KV_RANK:MLA_QK] = kr.astype(BF16)


def _mix_in(h, g, w1, gq, gkv, wuq, wuk, cos_t, sin_t, *, tm=384):
    bsz, seq, _ = h.shape
    grid = (bsz, seq // tm)
    row3 = lambda b, j: (b, j, 0)
    return pl.pallas_call(
        functools.partial(_mix_in_kernel, tm=tm),
        grid=grid,
        in_specs=[
            pl.BlockSpec((1, tm, D_MODEL), row3),
            _const_spec((1, D_MODEL)),
            _const_spec((D_MODEL, C_END)),
            _const_spec((1, 256)),
            _const_spec((1, MLA_KV_RANK)),
            _const_spec((256, 512)),
            _const_spec((256, 512)),
            pl.BlockSpec((tm, 128), lambda b, j: (j, 0)),
            pl.BlockSpec((tm, 128), lambda b, j: (j, 0)),
        ],
        out_specs=[
            pl.BlockSpec((1, tm, SSM_WIDTH), row3),
            pl.BlockSpec((1, tm, SWA_WIDTH), row3),
            pl.BlockSpec((1, tm, 2 * SWA_KV_WIDTH), row3),
            pl.BlockSpec((1, MLA_HEADS, tm, MLA_QK), lambda b, j: (b, 0, j, 0)),
            pl.BlockSpec((1, tm, MLA_QK), row3),
        ],
        out_shape=[
            jax.ShapeDtypeStruct((bsz, seq, SSM_WIDTH), F32),
            jax.ShapeDtypeStruct((bsz, seq, SWA_WIDTH), BF16),
            jax.ShapeDtypeStruct((bsz, seq, 2 * SWA_KV_WIDTH), BF16),
            jax.ShapeDtypeStruct((bsz, MLA_HEADS, seq, MLA_QK), BF16),
            jax.ShapeDtypeStruct((bsz, seq, MLA_QK), BF16),
        ],
        compiler_params=pltpu.CompilerParams(
            dimension_semantics=("parallel", "parallel"), vmem_limit_bytes=VMEM_LIMIT),
        name="mix_in",
    )(h, g, w1, gq, gkv, wuq, wuk, cos_t, sin_t)


def _split_bf16(x):
    hi = x.astype(BF16)
    lo = (x - hi.astype(F32)).astype(BF16)
    return hi, lo


def _bdot3(a, b):
    ah, al = _split_bf16(a)
    bh, bl = _split_bf16(b)
    dn = (((2,), (1,)), ((0,), (0,)))
    f = lambda x, y: lax.dot_general(x, y, dn, preferred_element_type=F32)
    return f(ah, bh) + f(ah, bl) + f(al, bh)


def _s5_param_kernel(arp_ref, aip_ref, lsp_ref, arc_ref, aic_ref, lsc_ref, br_ref, bi_ref, cr_ref, ci_ref,
                     kk_ref, pr_ref, pi_ref, clr_ref, cli_ref, l8r_ref, l8i_ref):
    def lam_pow(ar, ai, step, k):
        mag = jnp.exp((k * ar) * step)
        ang = (k * ai) * step
        return mag * jnp.cos(ang), mag * jnp.sin(ang)

    ar, ai = arp_ref[0], aip_ref[0]
    step = jnp.exp(lsp_ref[0])
    lr, li = lam_pow(ar, ai, step, 1.0)
    den = ar * ar + ai * ai
    nr, ni = lr - 1.0, li
    coef_re = (nr * ar + ni * ai) / den
    coef_im = (ni * ar - nr * ai) / den
    br, bi = br_ref[0], bi_ref[0]
    bb_re = coef_re * br - coef_im * bi
    bb_im = coef_re * bi + coef_im * br
    for s in range(SSM_CHUNK):
        pr_, pi_ = lam_pow(ar, ai, step, float(SSM_CHUNK - 1 - s))
        pr_ref[0, s] = pr_ * bb_re - pi_ * bb_im
        pi_ref[0, s] = pr_ * bb_im + pi_ * bb_re

    arc, aic = arc_ref[0], aic_ref[0]
    stepc = jnp.exp(lsc_ref[0])
    cr, ci = cr_ref[0], ci_ref[0]
    for k in range(SSM_CHUNK + 1):
        qr_, qi_ = lam_pow(arc, aic, stepc, float(k))
        clr = cr * qr_ - ci * qi_
        cli = cr * qi_ + ci * qr_
        if k < SSM_CHUNK:
            kk_ref[0, k] = _bdot3(clr, bb_re) - _bdot3(cli, bb_im)
        if k >= 1:
            clr_ref[0, k - 1] = clr
            cli_ref[0, k - 1] = cli
    l8r, l8i = lam_pow(arc, aic, stepc, float(SSM_CHUNK))
    l8r_ref[0] = l8r
    l8i_ref[0] = l8i


def _s5_params(a_re, a_im, log_step, b_re, b_im, c_re, c_im):
    depth = a_re.shape[0]
    G, P, C, T = SSM_GROUPS, SSM_STATE, SSM_GROUP_CH, SSM_CHUNK
    lay = lambda *shape: pl.BlockSpec((1,) + shape, lambda l: (l,) + (0,) * len(shape))
    ls = jnp.broadcast_to(log_step[:, :, None], (depth, G, P))
    args = (a_re[..., None], a_im[..., None], ls[..., None],
            a_re[:, :, None, :], a_im[:, :, None, :], ls[:, :, None, :],
            b_re, b_im, c_re, c_im)
    return pl.pallas_call(
        _s5_param_kernel,
        grid=(depth,),
        in_specs=[lay(G, P, 1)] * 3 + [lay(G, 1, P)] * 3 + [lay(G, P, C)] * 2 + [lay(G, C, P)] * 2,
        out_specs=[lay(T, G, C, C), lay(T, G, P, C), lay(T, G, P, C), lay(T, G, C, P), lay(T, G, C, P),
                   lay(G, 1, P), lay(G, 1, P)],
        out_shape=[
            jax.ShapeDtypeStruct((depth, T, G, C, C), F32),
            jax.ShapeDtypeStruct((depth, T, G, P, C), F32),
            jax.ShapeDtypeStruct((depth, T, G, P, C), F32),
            jax.ShapeDtypeStruct((depth, T, G, C, P), F32),
            jax.ShapeDtypeStruct((depth, T, G, C, P), F32),
            jax.ShapeDtypeStruct((depth, G, 1, P), F32),
            jax.ShapeDtypeStruct((depth, G, 1, P), F32),
        ],
        compiler_params=pltpu.CompilerParams(dimension_semantics=("parallel",)),
        name="s5_params",
    )(*args)


def _group_block_diag(x):
    *lead, G, r, c = x.shape
    eye = jnp.eye(G, dtype=x.dtype)
    y = x[..., :, :, None, :] * eye[:, None, :, None]
    return y.reshape(*lead, G * r, G * c)


def _s5_matrices(kk, pr, pi, clr, cli, l8r, l8i):
    T = SSM_CHUNK
    kb = _group_block_diag(jnp.swapaxes(kk, -1, -2))
    kb = jnp.concatenate([kb, jnp.zeros_like(kb[:1])], axis=0)
    lag = jnp.arange(T)[None, :] - jnp.arange(T)[:, None]
    r4 = kb[jnp.where(lag >= 0, lag, T)]
    r_mat = r4.transpose(0, 2, 1, 3).reshape(SSM_FOLD, SSM_FOLD)
    p_re = _group_block_diag(jnp.swapaxes(pr, -1, -2)).reshape(SSM_FOLD, SSM_MODES)
    p_im = _group_block_diag(jnp.swapaxes(pi, -1, -2)).reshape(SSM_FOLD, SSM_MODES)
    p_mat = jnp.concatenate([p_re, p_im], axis=1)
    q_re = _group_block_diag(jnp.swapaxes(clr, -1, -2)).transpose(1, 0, 2).reshape(SSM_MODES, SSM_FOLD)
    q_im = _group_block_diag(jnp.swapaxes(cli, -1, -2)).transpose(1, 0, 2).reshape(SSM_MODES, SSM_FOLD)
    q_mat = jnp.concatenate([q_re, -q_im], axis=0)
    lam = jnp.concatenate([l8r.reshape(1, SSM_MODES), l8i.reshape(1, SSM_MODES)], axis=1)
    return r_mat.astype(BF16), p_mat.astype(BF16), q_mat.astype(BF16), lam


def _s5_state_kernel(u_ref, p_ref, s_ref):
    s_ref[0] = jnp.dot(u_ref[0].astype(BF16), p_ref[...], preferred_element_type=F32)


def _s5_scan_kernel(s_ref, lam_ref, x_ref, *, n_chunks):
    lr = lam_ref[:, 0:SSM_MODES]
    li = lam_ref[:, SSM_MODES:2 * SSM_MODES]

    def body(n, carry):
        xr, xi = carry
        x_ref[0, pl.ds(n, 1), 0:SSM_MODES] = xr
        x_ref[0, pl.ds(n, 1), SSM_MODES:2 * SSM_MODES] = xi
        sr = s_ref[0, pl.ds(n, 1), 0:SSM_MODES]
        si = s_ref[0, pl.ds(n, 1), SSM_MODES:2 * SSM_MODES]
        return lr * xr - li * xi + sr, lr * xi + li * xr + si

    z = jnp.zeros((1, SSM_MODES), F32)
    lax.fori_loop(0, n_chunks, body, (z, z))


def _s5_out_kernel(u_ref, x_ref, ucol_ref, r_ref, q_ref, d_ref, y_ref):
    y = jnp.dot(u_ref[0].astype(BF16), r_ref[...], preferred_element_type=F32)
    y += jnp.dot(x_ref[0].astype(BF16), q_ref[...], preferred_element_type=F32)
    y_ref[0] = y + d_ref[...] * ucol_ref[0]


def _s5_sequence(u_fold, r_mat, p_mat, q_mat, lam, d_fold, *, tn=1024):
    bsz, n_chunks, _ = u_fold.shape
    nt = SSM_FOLD // tn
    params = pltpu.CompilerParams(dimension_semantics=("parallel", "parallel"), vmem_limit_bytes=VMEM_LIMIT)
    rows = pl.BlockSpec((1, n_chunks, SSM_FOLD), lambda b, n: (b, 0, 0))
    cols = pl.BlockSpec((1, n_chunks, tn), lambda b, n: (b, 0, n))
    wcol = pl.BlockSpec((SSM_FOLD, tn), lambda b, n: (0, n))
    s = pl.pallas_call(
        _s5_state_kernel, grid=(bsz, nt), in_specs=[rows, wcol], out_specs=cols,
        out_shape=jax.ShapeDtypeStruct((bsz, n_chunks, 2 * SSM_MODES), F32),
        compiler_params=params, name="s5_state",
    )(u_fold, p_mat)
    x_prev = pl.pallas_call(
        functools.partial(_s5_scan_kernel, n_chunks=n_chunks), grid=(bsz,),
        in_specs=[pl.BlockSpec((1, n_chunks, 2 * SSM_MODES), lambda b: (b, 0, 0)),
                  pl.BlockSpec((1, 2 * SSM_MODES), lambda b: (0, 0))],
        out_specs=pl.BlockSpec((1, n_chunks, 2 * SSM_MODES), lambda b: (b, 0, 0)),
        out_shape=jax.ShapeDtypeStruct((bsz, n_chunks, 2 * SSM_MODES), F32),
        compiler_params=pltpu.CompilerParams(dimension_semantics=("parallel",), vmem_limit_bytes=VMEM_LIMIT),
        name="s5_scan",
    )(s, lam)
    return pl.pallas_call(
        _s5_out_kernel, grid=(bsz, nt),
        in_specs=[rows, rows, cols, wcol, wcol, pl.BlockSpec((1, tn), lambda b, n: (0, n))],
        out_specs=cols,
        out_shape=jax.ShapeDtypeStruct((bsz, n_chunks, SSM_FOLD), F32),
        compiler_params=params, name="s5_out",
    )(u_fold, x_prev, u_fold, r_mat, q_mat, d_fold)


def _swa_kernel(q_ref, kvp_ref, kvc_ref, slope_ref, sink_ref, o_ref):
    i = pl.program_id(1)
    q = q_ref[0]
    kvp = kvp_ref[0]
    kvc = kvc_ref[0]
    cols = SWA_GROUP * BLOCK
    jl = lax.broadcasted_iota(jnp.int32, (BLOCK, cols), 0)
    t = lax.broadcasted_iota(jnp.int32, (BLOCK, cols), 1) & (BLOCK - 1)
    cur = jl <= t
    d0 = t - jl
    dist = jnp.where(cur, d0, d0 + BLOCK).astype(F32)
    kabs = jnp.where(cur, i * BLOCK + jl, (i - 1) * BLOCK + jl)
    valid = kabs >= PAD
    outs = []
    for hk in range(SWA_KV_HEADS):
        ks = slice(hk * HEAD_DIM, (hk + 1) * HEAD_DIM)
        vs = slice(SWA_KV_WIDTH + hk * HEAD_DIM, SWA_KV_WIDTH + (hk + 1) * HEAD_DIM)
        kcat = jnp.concatenate([kvp[:, ks], kvc[:, ks]], axis=0)
        vcat = jnp.concatenate([kvp[:, vs], kvc[:, vs]], axis=0)
        qs = jnp.concatenate(
            [q[:, (hk * SWA_GROUP + g) * HEAD_DIM:(hk * SWA_GROUP + g + 1) * HEAD_DIM] for g in range(SWA_GROUP)],
            axis=0)
        sf = lax.dot_general(kcat, qs, (((1,), (1,)), ((), ())), preferred_element_type=F32)
        s = jnp.where(cur, sf[BLOCK:2 * BLOCK], sf[0:BLOCK])
        slope = slope_ref[hk]
        sink = sink_ref[hk]
        s = jnp.where(valid, s - slope * dist, NEG)
        m = jnp.maximum(jnp.max(s, axis=0, keepdims=True), sink)
        p = jnp.exp(s - m)
        den = jnp.sum(p, axis=0, keepdims=True) + jnp.exp(sink - m)
        pf = jnp.concatenate([jnp.where(cur, 0.0, p), jnp.where(cur, p, 0.0)], axis=0).astype(BF16)
        o_t = lax.dot_general(vcat, pf, (((0,), (0,)), ((), ())), preferred_element_type=F32) / den
        outs += [o_t[:, g * BLOCK:(g + 1) * BLOCK] for g in range(SWA_GROUP)]
    o_ref[0] = jnp.concatenate(outs, axis=0).T


def _swa(q, kv, slope_rows, sink_rows):
    bsz, seq, _ = q.shape
    nb = seq // BLOCK
    cols = SWA_GROUP * BLOCK
    return pl.pallas_call(
        _swa_kernel,
        grid=(bsz, nb),
        in_specs=[
            pl.BlockSpec((1, BLOCK, SWA_WIDTH), lambda b, i: (b, i, 0)),
            pl.BlockSpec((1, BLOCK, 2 * SWA_KV_WIDTH), lambda b, i: (b, jnp.maximum(i - 1, 0), 0)),
            pl.BlockSpec((1, BLOCK, 2 * SWA_KV_WIDTH), lambda b, i: (b, i, 0)),
            _const_spec((SWA_KV_HEADS, 1, cols)),
            _const_spec((SWA_KV_HEADS, 1, cols)),
        ],
        out_specs=pl.BlockSpec((1, BLOCK, SWA_WIDTH), lambda b, i: (b, i, 0)),
        out_shape=jax.ShapeDtypeStruct((bsz, seq, SWA_WIDTH), F32),
        compiler_params=pltpu.CompilerParams(dimension_semantics=("parallel", "parallel")),
        name="swa",
    )(q, kv, kv, slope_rows, sink_rows)


def _mla_kernel(q_ref, k_ref, wuv_ref, o_ref, m_sc, l_sc, acc_sc, *, tq, tk):
    i = pl.program_id(1)
    cols = MLA_HEADS * tq
    q = q_ref[0].reshape(cols, MLA_QK)
    m_sc[...] = jnp.full((1, cols), NEG, F32)
    l_sc[...] = jnp.zeros((1, cols), F32)
    acc_sc[...] = jnp.zeros((MLA_KV_RANK, cols), F32)
    qpos = i * tq + (lax.broadcasted_iota(jnp.int32, (tk, cols), 1) & (tq - 1))
    krow = lax.broadcasted_iota(jnp.int32, (tk, cols), 0)
    n_tiles = ((i + 1) * tq + tk - 1) // tk

    def tile(j, masked):
        kt = k_ref[0, pl.ds(pl.multiple_of(j * tk, tk), tk), :]
        s = lax.dot_general(kt, q, (((1,), (1,)), ((), ())), preferred_element_type=F32)
        if masked:
            kidx = krow + j * tk
            s = jnp.where((kidx <= qpos) & (kidx >= PAD), s, NEG)
        m_prev = m_sc[...]
        m_new = jnp.maximum(m_prev, jnp.max(s, axis=0, keepdims=True))
        alpha = jnp.exp(m_prev - m_new)
        p = jnp.exp(s - m_new)
        l_sc[...] = alpha * l_sc[...] + jnp.sum(p, axis=0, keepdims=True)
        pv = lax.dot_general(kt[:, 0:MLA_KV_RANK], p.astype(BF16), (((0,), (0,)), ((), ())),
                             preferred_element_type=F32)
        acc_sc[...] = alpha * acc_sc[...] + pv
        m_sc[...] = m_new

    tile(0, True)

    def body(j, carry):
        tile(j, False)
        return carry

    lax.fori_loop(1, n_tiles - 1, body, 0)

    @pl.when(n_tiles > 1)
    def _():
        tile(n_tiles - 1, True)

    o = (acc_sc[...] / l_sc[...]).astype(BF16)
    y = jnp.zeros((tq, MLA_WIDTH), F32)
    for hd in range(MLA_HEADS):
        y += lax.dot_general(o[:, hd * tq:(hd + 1) * tq], wuv_ref[hd], (((0,), (0,)), ((), ())),
                             preferred_element_type=F32)
    o_ref[0] = y


def _mla(qm, kc_padded, wuv_heads, *, tq=128, tk=256):
    bsz, _, seq, _ = qm.shape
    seq_k = kc_padded.shape[1]
    cols = MLA_HEADS * tq
    return pl.pallas_call(
        functools.partial(_mla_kernel, tq=tq, tk=tk),
        grid=(bsz, seq // tq),
        in_specs=[
            pl.BlockSpec((1, MLA_HEADS, tq, MLA_QK), lambda b, i: (b, 0, i, 0)),
            pl.BlockSpec((1, seq_k, MLA_QK), lambda b, i: (b, 0, 0)),
            _const_spec((MLA_HEADS, MLA_KV_RANK, MLA_WIDTH)),
        ],
        out_specs=pl.BlockSpec((1, tq, MLA_WIDTH), lambda b, i: (b, i, 0)),
        out_shape=jax.ShapeDtypeStruct((bsz, seq, MLA_WIDTH), F32),
        scratch_shapes=[pltpu.VMEM((1, cols), F32), pltpu.VMEM((1, cols), F32),
                        pltpu.VMEM((MLA_KV_RANK, cols), F32)],
        compiler_params=pltpu.CompilerParams(
            dimension_semantics=("parallel", "arbitrary"), vmem_limit_bytes=VMEM_LIMIT),
        name="mla",
    )(qm, kc_padded, wuv_heads)


def _gelu_tanh(x):
    return 0.5 * x * (1.0 + jnp.tanh(math.sqrt(2.0 / math.pi) * (x + 0.044715 * (x * x * x))))


def _mix_out_kernel(h_ref, ys_ref, yw_ref, ym_ref, wglu_ref, bglu_ref, gh_ref, wout_ref, gpm_ref, gpre_ref,
                    wup_ref, wdn_ref, gpost_ref, o_ref, *, ff_chunk):
    z = _gelu_tanh(ys_ref[...])
    gate = jnp.dot(z.astype(BF16), wglu_ref[...], preferred_element_type=F32) + bglu_ref[...]
    y_ssm = z * (1.0 / (1.0 + jnp.exp(-gate)))
    gh = gh_ref[...]
    mixed = jnp.concatenate([
        _rms(y_ssm, gh[:, 0:SSM_WIDTH]),
        _rms(yw_ref[...], gh[:, SSM_WIDTH:SSM_WIDTH + SWA_WIDTH]),
        _rms(ym_ref[...], gh[:, SSM_WIDTH + SWA_WIDTH:]),
    ], axis=1).astype(BF16)
    mo = jnp.dot(mixed, wout_ref[...], preferred_element_type=F32)
    h1 = h_ref[...] + _rms(mo, gpm_ref[...])
    a = _rms(h1, gpre_ref[...]).astype(BF16)
    f = jnp.zeros_like(h1)
    for c in range(D_FF // ff_chunk):
        up = jnp.dot(a, wup_ref[:, c * ff_chunk:(c + 1) * ff_chunk], preferred_element_type=F32)
        hid = jnp.square(jnp.maximum(up, 0.0)).astype(BF16)
        f += jnp.dot(hid, wdn_ref[c * ff_chunk:(c + 1) * ff_chunk, :], preferred_element_type=F32)
    o_ref[...] = h1 + _rms(f, gpost_ref[...])


def _mix_out(h, ys, yw, ym, wglu, bglu, gh, wout, gpm, gpre, wup, wdn, gpost, *, tm=512, ff_chunk=1024):
    n, _ = h.shape
    row = lambda w: pl.BlockSpec((tm, w), lambda i: (i, 0))
    return pl.pallas_call(
        functools.partial(_mix_out_kernel, ff_chunk=ff_chunk),
        grid=(n // tm,),
        in_specs=[
            row(D_MODEL), row(SSM_WIDTH), row(SWA_WIDTH), row(MLA_WIDTH),
            _const_spec((SSM_WIDTH, SSM_WIDTH)), _const_spec((1, SSM_WIDTH)), _const_spec((1, D_MODEL)),
            _const_spec((D_MODEL, D_MODEL)), _const_spec((1, D_MODEL)), _const_spec((1, D_MODEL)),
            _const_spec((D_MODEL, D_FF)), _const_spec((D_FF, D_MODEL)), _const_spec((1, D_MODEL)),
        ],
        out_specs=row(D_MODEL),
        out_shape=jax.ShapeDtypeStruct((n, D_MODEL), F32),
        compiler_params=pltpu.CompilerParams(dimension_semantics=("parallel",), vmem_limit_bytes=VMEM_LIMIT),
        name="mix_out",
    )(h, ys, yw, ym, wglu, bglu, gh, wout, gpm, gpre, wup, wdn, gpost)


def _rot_half_cols(w):
    half = w.shape[-1] // 2
    return jnp.concatenate([-w[..., half:], w[..., :half]], axis=-1)


def _layer_weights(w_in, w_uq, w_uk, w_uv):
    z = lambda n: jnp.zeros((D_MODEL, n), w_in.dtype)
    s_cq = SSM_WIDTH + SWA_WIDTH + 2 * SWA_KV_WIDTH
    s_ckv = s_cq + MLA_Q_RANK
    s_kr = s_ckv + MLA_KV_RANK
    kr = w_in[:, s_kr:s_kr + MLA_ROPE]
    w1 = jnp.concatenate([
        w_in[:, :s_cq], w_in[:, s_cq:s_ckv], z(C_CKV - C_CQ - MLA_Q_RANK),
        w_in[:, s_ckv:s_kr], kr, _rot_half_cols(kr), z(C_END - C_KR - 2 * MLA_ROPE)], axis=1).astype(BF16)
    uq = w_uq.reshape(MLA_Q_RANK, MLA_HEADS, MLA_NOPE + MLA_ROPE)
    rope = uq[:, :, MLA_NOPE:]
    wuq = jnp.concatenate([
        uq[:, :, :MLA_NOPE].reshape(MLA_Q_RANK, -1), rope.reshape(MLA_Q_RANK, -1),
        _rot_half_cols(rope).reshape(MLA_Q_RANK, -1)], axis=1)
    wuq = jnp.concatenate([wuq, jnp.zeros((256 - MLA_Q_RANK, wuq.shape[1]), wuq.dtype)], axis=0).astype(BF16)
    uk = w_uk.reshape(MLA_KV_RANK, MLA_HEADS, MLA_NOPE).transpose(1, 2, 0)
    wuk_bd = _group_block_diag(uk).astype(BF16)
    uv = w_uv.reshape(MLA_KV_RANK, MLA_HEADS, MLA_V).transpose(1, 0, 2)
    wuv_heads = _group_block_diag(uv).reshape(MLA_HEADS, MLA_KV_RANK, MLA_WIDTH).astype(BF16)
    return w1, wuq, wuk_bd, wuv_heads


def kernel(x, meta_tokens, norm_pre_mix, norm_post_mix, norm_pre_mlp, norm_post_mlp, w_in, w_out, norm_heads,
           ssm_a_re, ssm_a_im, ssm_log_step, ssm_b_re, ssm_b_im, ssm_c_re, ssm_c_im, ssm_d, ssm_w_glu,
           ssm_b_glu, swa_sinks, mla_q_norm, mla_kv_norm, mla_w_uq, mla_w_uk, mla_w_uv, w_mlp_up, w_mlp_down):
    bsz, seq_real, d = x.shape
    depth = w_in.shape[0]
    meta = jnp.broadcast_to(meta_tokens.astype(x.dtype)[None], (bsz, N_META, d))
    h = jnp.concatenate([jnp.zeros((bsz, PAD, d), x.dtype), meta, x], axis=1)
    seq = h.shape[1]
    n_chunks = seq // SSM_CHUNK

    half = MLA_ROPE // 2
    inv_freq = ROPE_THETA ** (-jnp.arange(half, dtype=F32) / half)
    ang = (jnp.arange(seq) - PAD).astype(F32)[:, None] * inv_freq[None, :]
    cos_t = jnp.tile(jnp.cos(ang), (1, 2 * MLA_HEADS))
    sin_t = jnp.tile(jnp.sin(ang), (1, 2 * MLA_HEADS))
    slopes = 2.0 ** (-8.0 * jnp.arange(1, SWA_HEADS + 1, dtype=F32) / SWA_HEADS)
    per_row = lambda v: jnp.repeat(v.reshape(SWA_KV_HEADS, SWA_GROUP), BLOCK, axis=1)[:, None, :]
    slope_rows = per_row(slopes)

    s5p = _s5_params(ssm_a_re, ssm_a_im, ssm_log_step, ssm_b_re, ssm_b_im, ssm_c_re, ssm_c_im)
    row = lambda v: v.reshape(1, -1).astype(F32)

    for l in range(depth):
        w1, wuq, wuk_bd, wuv_heads = _layer_weights(w_in[l], mla_w_uq[l], mla_w_uk[l], mla_w_uv[l])
        gq = jnp.concatenate([mla_q_norm[l], jnp.zeros((256 - MLA_Q_RANK,), F32)]).reshape(1, 256)
        u, q, kv, qm, kc = _mix_in(h, row(norm_pre_mix[l]), w1, gq, row(mla_kv_norm[l]), wuq, wuk_bd,
                                   cos_t, sin_t)
        r_mat, p_mat, q_mat, lam = _s5_matrices(*[t[l] for t in s5p])
        d_fold = jnp.tile(ssm_d[l].astype(F32), SSM_CHUNK).reshape(1, SSM_FOLD)
        y_ssm = _s5_sequence(u.reshape(bsz, n_chunks, SSM_FOLD), r_mat, p_mat, q_mat, lam, d_fold)
        y_swa = _swa(q, kv, slope_rows, per_row(swa_sinks[l].astype(F32)))
        kc_padded = jnp.pad(kc, ((0, 0), (0, (-seq) % 256), (0, 0)))
        y_mla = _mla(qm, kc_padded, wuv_heads)
        n = bsz * seq
        h = _mix_out(
            h.reshape(n, d), y_ssm.reshape(n, SSM_WIDTH), y_swa.reshape(n, SWA_WIDTH), y_mla.reshape(n, MLA_WIDTH),
            ssm_w_glu[l].astype(BF16), row(ssm_b_glu[l]), row(norm_heads[l]), w_out[l].astype(BF16),
            row(norm_post_mix[l]), row(norm_pre_mlp[l]), w_mlp_up[l].astype(BF16), w_mlp_down[l].astype(BF16),
            row(norm_post_mlp[l])).reshape(bsz, seq, d)
    return h[:, PAD + N_META:, :]
```

```python
import functools
import math

import jax
import jax.numpy as jnp
from jax import lax
from jax.experimental import pallas as pl
from jax.experimental.pallas import tpu as pltpu

F32 = jnp.float32
BF16 = jnp.bfloat16

D_MODEL = 1024
N_META = 16
BLOCK = 128
PAD = BLOCK - N_META
SSM_GROUPS = 16
SSM_GROUP_CH = 16
SSM_WIDTH = SSM_GROUPS * SSM_GROUP_CH
SSM_STATE = 64
SWA_HEADS = 8
SWA_KV_HEADS = 2
SWA_GROUP = SWA_HEADS // SWA_KV_HEADS
HEAD_DIM = 64
SWA_WIDTH = SWA_HEADS * HEAD_DIM
SWA_KV_WIDTH = SWA_KV_HEADS * HEAD_DIM
MLA_HEADS = 4
MLA_Q_RANK = 192
MLA_KV_RANK = 128
MLA_NOPE = 64
MLA_ROPE = 32
MLA_V = 64
MLA_WIDTH = MLA_HEADS * MLA_V
MLA_QK = MLA_KV_RANK + MLA_ROPE
ROPE_THETA = 10000.0
D_FF = 4 * D_MODEL
EPS = 1e-6
NEG = -1e30

SSM_CHUNK = 8
SSM_FOLD = SSM_CHUNK * SSM_WIDTH
SSM_MODES = SSM_GROUPS * SSM_STATE

C_U, C_Q, C_K, C_CQ, C_CKV, C_KR, C_END = 0, 256, 768, 1024, 1280, 1408, 1536

VMEM_LIMIT = 56 * 1024 * 1024


def _rms(x, g, n=None):
    n = x.shape[-1] if n is None else n
    ms = jnp.sum(x * x, axis=-1, keepdims=True) * (1.0 / n)
    return x * lax.rsqrt(ms + EPS) * g


def _const_spec(shape):
    nd = len(shape)
    return pl.BlockSpec(shape, lambda *_: (0,) * nd, pipeline_mode=pl.Buffered(1))


def _mix_in_kernel(h_ref, g_ref, w1_ref, gq_ref, gkv_ref, wuq_ref, wuk_ref, cos_ref, sin_ref,
                   u_ref, q_ref, kv_ref, qm_ref, kc_ref, *, tm):
    j = pl.program_id(1)
    a = _rms(h_ref[0], g_ref[...]).astype(BF16)
    proj = jnp.dot(a, w1_ref[...], preferred_element_type=F32)
    row = j * tm + lax.broadcasted_iota(jnp.int32, (tm, 1), 0)
    for c in range(2):
        u_ref[0, c] = jnp.where(row >= PAD, proj[:, C_U + c * 128:C_U + (c + 1) * 128], 0.0)
    q_ref[0] = (proj[:, C_Q:C_K] * (HEAD_DIM ** -0.5)).astype(BF16)
    kv_ref[0] = proj[:, C_K:C_CQ].astype(BF16)

    scale = (MLA_NOPE + MLA_ROPE) ** -0.5
    cos = cos_ref[...]
    sin = sin_ref[...]
    qn = _rms(proj[:, C_CQ:C_CKV], gq_ref[...], n=MLA_Q_RANK).astype(BF16)
    q2 = jnp.dot(qn, wuq_ref[...], preferred_element_type=F32)
    qr = (q2[:, 256:384] * cos + q2[:, 384:512] * sin) * scale
    qa = jnp.dot(q2[:, 0:256].astype(BF16), wuk_ref[...], preferred_element_type=F32) * scale
    c = _rms(proj[:, C_CKV:C_KR], gkv_ref[...])
    krr = proj[:, C_KR:C_END]
    kr = krr[:, 0:MLA_ROPE] * cos[:, 0:MLA_ROPE] + krr[:, MLA_ROPE:2 * MLA_ROPE] * sin[:, 0:MLA_ROPE]
    for hd in range(MLA_HEADS):
        qm_ref[0, hd, :, 0:MLA_KV_RANK] = qa[:, hd * MLA_KV_RANK:(hd + 1) * MLA_KV_RANK].astype(BF16)
        qm_ref[0, hd, :, MLA_KV_RANK:MLA_QK] = qr[:, hd * MLA_ROPE:(hd + 1) * MLA_ROPE].astype(BF16)
    kc_ref[0, :, 0:MLA_KV_RANK] = c.astype(BF16)
    kc_ref[0, :, MLA_KV_RANK:MLA_QK] = kr.astype(BF16)


def _mix_in(h, g, w1, gq, gkv, wuq, wuk, cos_t, sin_t, *, tm=384):
    bsz, seq, _ = h.shape
    grid = (bsz, seq // tm)
    row3 = lambda b, j: (b, j, 0)
    return pl.pallas_call(
        functools.partial(_mix_in_kernel, tm=tm),
        grid=grid,
        in_specs=[
            pl.BlockSpec((1, tm, D_MODEL), row3),
            _const_spec((1, D_MODEL)),
            _const_spec((D_MODEL, C_END)),
            _const_spec((1, 256)),
            _const_spec((1, MLA_KV_RANK)),
            _const_spec((256, 512)),
            _const_spec((256, 512)),
            pl.BlockSpec((tm, 128), lambda b, j: (j, 0)),
            pl.BlockSpec((tm, 128), lambda b, j: (j, 0)),
        ],
        out_specs=[
            pl.BlockSpec((1, 2, tm, 128), lambda b, j: (b, 0, j, 0)),
            pl.BlockSpec((1, tm, SWA_WIDTH), row3),
            pl.BlockSpec((1, tm, 2 * SWA_KV_WIDTH), row3),
            pl.BlockSpec((1, MLA_HEADS, tm, MLA_QK), lambda b, j: (b, 0, j, 0)),
            pl.BlockSpec((1, tm, MLA_QK), row3),
        ],
        out_shape=[
            jax.ShapeDtypeStruct((bsz, 2, seq, 128), F32),
            jax.ShapeDtypeStruct((bsz, seq, SWA_WIDTH), BF16),
            jax.ShapeDtypeStruct((bsz, seq, 2 * SWA_KV_WIDTH), BF16),
            jax.ShapeDtypeStruct((bsz, MLA_HEADS, seq, MLA_QK), BF16),
            jax.ShapeDtypeStruct((bsz, seq, MLA_QK), BF16),
        ],
        compiler_params=pltpu.CompilerParams(
            dimension_semantics=("parallel", "parallel"), vmem_limit_bytes=VMEM_LIMIT),
        name="mix_in",
    )(h, g, w1, gq, gkv, wuq, wuk, cos_t, sin_t)


def _split_bf16(x):
    hi = x.astype(BF16)
    lo = (x - hi.astype(F32)).astype(BF16)
    return hi, lo


def _bdot3(a, b):
    ah, al = _split_bf16(a)
    bh, bl = _split_bf16(b)
    dn = (((2,), (1,)), ((0,), (0,)))
    f = lambda x, y: lax.dot_general(x, y, dn, preferred_element_type=F32)
    return f(ah, bh) + f(ah, bl) + f(al, bh)


def _expand_groups(a2, row_shift, col_shift, n_cols):
    rows, w = a2.shape
    tile = (lax.broadcasted_iota(jnp.int32, (w, n_cols), 1) & (w - 1)) == lax.broadcasted_iota(
        jnp.int32, (w, n_cols), 0)
    wide = jnp.dot(a2.astype(BF16), jnp.where(tile, 1.0, 0.0).astype(BF16), preferred_element_type=F32)
    same = (lax.broadcasted_iota(jnp.int32, (rows, n_cols), 0) >> row_shift) == (
        lax.broadcasted_iota(jnp.int32, (rows, n_cols), 1) >> col_shift)
    return jnp.where(same, wide, 0.0)


def _s5_operator_kernel(arp_ref, aip_ref, lsp_ref, arc_ref, aic_ref, lsc_ref, btr_ref, bti_ref, ctr_ref, cti_ref,
                        r_ref, p_ref, q_ref, l8r_ref, l8i_ref):
    s = pl.program_id(1)

    def lam_pow(ar, ai, step, k):
        mag = jnp.exp((k * ar) * step)
        ang = (k * ai) * step
        return mag * jnp.cos(ang), mag * jnp.sin(ang)

    arc, aic = arc_ref[0], aic_ref[0]
    stepc = jnp.exp(lsc_ref[0])
    lr, li = lam_pow(arc, aic, stepc, 1.0)
    den = arc * arc + aic * aic
    nr, ni = lr - 1.0, li
    coef_re = (nr * arc + ni * aic) / den
    coef_im = (ni * arc - nr * aic) / den
    btr, bti = btr_ref[0], bti_ref[0]
    bb_re = coef_re * btr - coef_im * bti
    bb_im = coef_re * bti + coef_im * btr

    arp, aip = arp_ref[0], aip_ref[0]
    stepp = jnp.exp(lsp_ref[0])
    ctr, cti = ctr_ref[0], cti_ref[0]

    def c_lam(k):
        qr_, qi_ = lam_pow(arp, aip, stepp, k)
        return ctr * qr_ - cti * qi_, ctr * qi_ + cti * qr_

    gc = SSM_GROUPS * SSM_GROUP_CH
    for t in range(SSM_CHUNK):
        lag = t - s
        clr, cli = c_lam(jnp.maximum(lag, 0).astype(F32))
        kt = _bdot3(bb_re, clr) - _bdot3(bb_im, cli)
        blk = _expand_groups(kt.reshape(gc, SSM_GROUP_CH), 4, 4, gc)
        r_ref[0, :, t * gc:(t + 1) * gc] = jnp.where(lag >= 0, blk, 0.0).astype(BF16)

    pw_r, pw_i = lam_pow(arc, aic, stepc, (SSM_CHUNK - 1 - s).astype(F32))
    p_re = (pw_r * bb_re - pw_i * bb_im).reshape(gc, SSM_STATE)
    p_im = (pw_r * bb_im + pw_i * bb_re).reshape(gc, SSM_STATE)
    p_ref[0, :, 0:SSM_MODES] = _expand_groups(p_re, 4, 6, SSM_MODES).astype(BF16)
    p_ref[0, :, SSM_MODES:2 * SSM_MODES] = _expand_groups(p_im, 4, 6, SSM_MODES).astype(BF16)

    clr, cli = c_lam((s + 1).astype(F32))
    q_ref[0, 0:SSM_MODES, :] = _expand_groups(clr.reshape(SSM_MODES, SSM_GROUP_CH), 6, 4, gc).astype(BF16)
    q_ref[0, SSM_MODES:2 * SSM_MODES, :] = (
        -_expand_groups(cli.reshape(SSM_MODES, SSM_GROUP_CH), 6, 4, gc)).astype(BF16)

    l8r, l8i = lam_pow(arc, aic, stepc, float(SSM_CHUNK))
    l8r_ref[0] = l8r
    l8i_ref[0] = l8i


def _s5_operators(a_re, a_im, log_step, b_re, b_im, c_re, c_im):
    depth = a_re.shape[0]
    G, P, C, T = SSM_GROUPS, SSM_STATE, SSM_GROUP_CH, SSM_CHUNK
    lay = lambda *shape: pl.BlockSpec((1,) + shape, lambda l, s: (l,) + (0,) * len(shape))
    ls = jnp.broadcast_to(log_step[:, :, None], (depth, G, P))
    args = (a_re[..., None], a_im[..., None], ls[..., None],
            a_re[:, :, None, :], a_im[:, :, None, :], ls[:, :, None, :],
            b_re.transpose(0, 1, 3, 2), b_im.transpose(0, 1, 3, 2),
            c_re.transpose(0, 1, 3, 2), c_im.transpose(0, 1, 3, 2))
    r_mat, p_mat, q_mat, l8r, l8i = pl.pallas_call(
        _s5_operator_kernel,
        grid=(depth, T),
        in_specs=[lay(G, P, 1)] * 3 + [lay(G, 1, P)] * 3 + [lay(G, C, P)] * 2 + [lay(G, P, C)] * 2,
        out_specs=[
            pl.BlockSpec((1, SSM_WIDTH, SSM_FOLD), lambda l, s: (l, s, 0)),
            pl.BlockSpec((1, SSM_WIDTH, 2 * SSM_MODES), lambda l, s: (l, s, 0)),
            pl.BlockSpec((1, 2 * SSM_MODES, SSM_WIDTH), lambda l, s: (l, 0, s)),
            lay(G, 1, P), lay(G, 1, P)],
        out_shape=[
            jax.ShapeDtypeStruct((depth, SSM_FOLD, SSM_FOLD), BF16),
            jax.ShapeDtypeStruct((depth, SSM_FOLD, 2 * SSM_MODES), BF16),
            jax.ShapeDtypeStruct((depth, 2 * SSM_MODES, SSM_FOLD), BF16),
            jax.ShapeDtypeStruct((depth, G, 1, P), F32),
            jax.ShapeDtypeStruct((depth, G, 1, P), F32),
        ],
        compiler_params=pltpu.CompilerParams(dimension_semantics=("parallel", "arbitrary")),
        name="s5_operators",
    )(*args)
    lam = jnp.concatenate([l8r.reshape(depth, 1, SSM_MODES), l8i.reshape(depth, 1, SSM_MODES)], axis=2)
    return r_mat, p_mat, q_mat, lam


def _group_block_diag(x):
    *lead, G, r, c = x.shape
    eye = jnp.eye(G, dtype=x.dtype)
    y = x[..., :, :, None, :] * eye[:, None, :, None]
    return y.reshape(*lead, G * r, G * c)


def _fold_tokens(u_ref, n_chunks):
    return jnp.concatenate(
        [u_ref[0, c, pl.ds(s, n_chunks, stride=SSM_CHUNK), :] for s in range(SSM_CHUNK) for c in range(2)], axis=1)


def _s5_state_kernel(u_ref, p_ref, lam_ref, x_ref, s_sc, x_sc, *, n_chunks):
    u = _fold_tokens(u_ref, n_chunks).astype(BF16)
    s_sc[...] = jnp.dot(u, p_ref[...], preferred_element_type=F32)
    lr = lam_ref[:, 0:SSM_MODES]
    li = lam_ref[:, SSM_MODES:2 * SSM_MODES]

    def body(n, carry):
        xr, xi = carry
        x_sc[pl.ds(n, 1), 0:SSM_MODES] = xr
        x_sc[pl.ds(n, 1), SSM_MODES:2 * SSM_MODES] = xi
        sr = s_sc[pl.ds(n, 1), 0:SSM_MODES]
        si = s_sc[pl.ds(n, 1), SSM_MODES:2 * SSM_MODES]
        return lr * xr - li * xi + sr, lr * xi + li * xr + si

    z = jnp.zeros((1, SSM_MODES), F32)
    lax.fori_loop(0, n_chunks, body, (z, z))
    x_ref[0] = x_sc[...].astype(BF16)


def _s5_out_kernel(u_ref, x_ref, r_ref, q_ref, d_ref, y_ref, *, n_chunks):
    u = _fold_tokens(u_ref, n_chunks)
    y = jnp.dot(u.astype(BF16), r_ref[...], preferred_element_type=F32)
    y += jnp.dot(x_ref[0], q_ref[...], preferred_element_type=F32)
    y += d_ref[...] * u
    for s in range(SSM_CHUNK):
        for c in range(2):
            lo = (2 * s + c) * 128
            y_ref[0, c, pl.ds(s, n_chunks, stride=SSM_CHUNK), :] = y[:, lo:lo + 128]


def _s5_sequence(u, r_mat, p_mat, q_mat, lam, d_fold):
    bsz, _, seq, _ = u.shape
    n_chunks = seq // SSM_CHUNK
    params = pltpu.CompilerParams(dimension_semantics=("parallel",), vmem_limit_bytes=VMEM_LIMIT)
    slabs = pl.BlockSpec((1, 2, seq, 128), lambda b: (b, 0, 0, 0))
    states = pl.BlockSpec((1, n_chunks, 2 * SSM_MODES), lambda b: (b, 0, 0))
    x_prev = pl.pallas_call(
        functools.partial(_s5_state_kernel, n_chunks=n_chunks), grid=(bsz,),
        in_specs=[slabs, _const_spec((SSM_FOLD, 2 * SSM_MODES)), _const_spec((1, 2 * SSM_MODES))],
        out_specs=states,
        out_shape=jax.ShapeDtypeStruct((bsz, n_chunks, 2 * SSM_MODES), BF16),
        scratch_shapes=[pltpu.VMEM((n_chunks, 2 * SSM_MODES), F32), pltpu.VMEM((n_chunks, 2 * SSM_MODES), F32)],
        compiler_params=params, name="s5_state",
    )(u, p_mat, lam)
    return pl.pallas_call(
        functools.partial(_s5_out_kernel, n_chunks=n_chunks), grid=(bsz,),
        in_specs=[slabs, states, _const_spec((SSM_FOLD, SSM_FOLD)), _const_spec((2 * SSM_MODES, SSM_FOLD)),
                  _const_spec((1, SSM_FOLD))],
        out_specs=slabs,
        out_shape=jax.ShapeDtypeStruct((bsz, 2, seq, 128), F32),
        compiler_params=params, name="s5_out",
    )(u, x_prev, r_mat, q_mat, d_fold)


def _swa_kernel(q_ref, kvp_ref, kvc_ref, slope_ref, sink_ref, o_ref):
    i = pl.program_id(1)
    q = q_ref[0]
    kvp = kvp_ref[0]
    kvc = kvc_ref[0]
    cols = SWA_GROUP * BLOCK
    jl = lax.broadcasted_iota(jnp.int32, (BLOCK, cols), 0)
    t = lax.broadcasted_iota(jnp.int32, (BLOCK, cols), 1) & (BLOCK - 1)
    cur = jl <= t
    d0 = t - jl
    dist = jnp.where(cur, d0, d0 + BLOCK).astype(F32)
    kabs = jnp.where(cur, i * BLOCK + jl, (i - 1) * BLOCK + jl)
    valid = kabs >= PAD
    outs = []
    for hk in range(SWA_KV_HEADS):
        ks = slice(hk * HEAD_DIM, (hk + 1) * HEAD_DIM)
        vs = slice(SWA_KV_WIDTH + hk * HEAD_DIM, SWA_KV_WIDTH + (hk + 1) * HEAD_DIM)
        kcat = jnp.concatenate([kvp[:, ks], kvc[:, ks]], axis=0)
        vcat = jnp.concatenate([kvp[:, vs], kvc[:, vs]], axis=0)
        qs = jnp.concatenate(
            [q[:, (hk * SWA_GROUP + g) * HEAD_DIM:(hk * SWA_GROUP + g + 1) * HEAD_DIM] for g in range(SWA_GROUP)],
            axis=0)
        sf = lax.dot_general(kcat, qs, (((1,), (1,)), ((), ())), preferred_element_type=F32)
        s = jnp.where(cur, sf[BLOCK:2 * BLOCK], sf[0:BLOCK])
        slope = slope_ref[hk]
        sink = sink_ref[hk]
        s = jnp.where(valid, s - slope * dist, NEG)
        m = jnp.maximum(jnp.max(s, axis=0, keepdims=True), sink)
        p = jnp.exp(s - m)
        den = jnp.sum(p, axis=0, keepdims=True) + jnp.exp(sink - m)
        pf = jnp.concatenate([jnp.where(cur, 0.0, p), jnp.where(cur, p, 0.0)], axis=0).astype(BF16)
        o_t = lax.dot_general(vcat, pf, (((0,), (0,)), ((), ())), preferred_element_type=F32) / den
        outs += [o_t[:, g * BLOCK:(g + 1) * BLOCK] for g in range(SWA_GROUP)]
    o_ref[0] = jnp.concatenate(outs, axis=0).T


def _swa(q, kv, slope_rows, sink_rows):
    bsz, seq, _ = q.shape
    nb = seq // BLOCK
    cols = SWA_GROUP * BLOCK
    return pl.pallas_call(
        _swa_kernel,
        grid=(bsz, nb),
        in_specs=[
            pl.BlockSpec((1, BLOCK, SWA_WIDTH), lambda b, i: (b, i, 0)),
            pl.BlockSpec((1, BLOCK, 2 * SWA_KV_WIDTH), lambda b, i: (b, jnp.maximum(i - 1, 0), 0)),
            pl.BlockSpec((1, BLOCK, 2 * SWA_KV_WIDTH), lambda b, i: (b, i, 0)),
            _const_spec((SWA_KV_HEADS, 1, cols)),
            _const_spec((SWA_KV_HEADS, 1, cols)),
        ],
        out_specs=pl.BlockSpec((1, BLOCK, SWA_WIDTH), lambda b, i: (b, i, 0)),
        out_shape=jax.ShapeDtypeStruct((bsz, seq, SWA_WIDTH), F32),
        compiler_params=pltpu.CompilerParams(dimension_semantics=("parallel", "parallel")),
        name="swa",
    )(q, kv, kv, slope_rows, sink_rows)


def _mla_kernel(q_ref, k_ref, wuv_ref, o_ref, s_sc, acc_sc, *, tq, tk, group):
    i = pl.program_id(1)
    cols = MLA_HEADS * tq
    q = q_ref[0].reshape(cols, MLA_QK)
    n_tiles = ((i + 1) * tq + tk - 1) // tk
    nt_dims = (((1,), (1,)), ((), ()))
    tn_dims = (((0,), (0,)), ((), ()))

    def fold(x, op):
        return op(x.reshape(x.shape[0] // 8, 8, cols), axis=0)

    def scores(j, rows, masked):
        start = j * tk if isinstance(j, int) else pl.multiple_of(j * tk, tk)
        s = lax.dot_general(k_ref[0, pl.ds(start, rows), :], q, nt_dims, preferred_element_type=F32)
        if masked:
            kidx = lax.broadcasted_iota(jnp.int32, (rows, cols), 0) + j * tk
            qpos = i * tq + (lax.broadcasted_iota(jnp.int32, (rows, cols), 1) & (tq - 1))
            s = jnp.where((kidx <= qpos) & (kidx >= PAD), s, NEG)
        s_sc[pl.ds(start, rows), :] = s
        return fold(s, jnp.max)

    m8 = scores(0, tk, True)
    n_inner = jnp.maximum(n_tiles - 2, 0)
    n_groups = n_inner // group
    m8 = lax.fori_loop(
        0, n_groups, lambda g, m: jnp.maximum(m, scores(1 + g * group, group * tk, False)), m8)
    m8 = lax.fori_loop(
        1 + n_groups * group, n_tiles - 1, lambda j, m: jnp.maximum(m, scores(j, tk, False)), m8)
    m8 = lax.cond(n_tiles > 1, lambda m: jnp.maximum(m, scores(n_tiles - 1, tk, True)), lambda m: m, m8)
    m = jnp.max(m8, axis=0, keepdims=True)

    acc_sc[...] = jnp.zeros((MLA_KV_RANK, cols), F32)

    def weights(j, rows, l8):
        start = j * tk if isinstance(j, int) else pl.multiple_of(j * tk, tk)
        p = jnp.exp(s_sc[pl.ds(start, rows), :] - m)
        v = k_ref[0, pl.ds(start, rows), 0:MLA_KV_RANK]
        acc_sc[...] += lax.dot_general(v, p.astype(BF16), tn_dims, preferred_element_type=F32)
        return l8 + fold(p, jnp.sum)

    n_groups2 = n_tiles // group
    l8 = lax.fori_loop(0, n_groups2, lambda g, l: weights(g * group, group * tk, l), jnp.zeros((8, cols), F32))
    l8 = lax.fori_loop(n_groups2 * group, n_tiles, lambda j, l: weights(j, tk, l), l8)
    l_sum = jnp.sum(l8, axis=0, keepdims=True)

    o = (acc_sc[...] / l_sum).astype(BF16)
    y = jnp.zeros((tq, MLA_WIDTH), F32)
    for hd in range(MLA_HEADS):
        y += lax.dot_general(o[:, hd * tq:(hd + 1) * tq], wuv_ref[hd], (((0,), (0,)), ((), ())),
                             preferred_element_type=F32)
    o_ref[0] = y


def _mla(qm, kc_padded, wuv_heads, *, tq=128, tk=256, group=4):
    bsz, _, seq, _ = qm.shape
    seq_k = kc_padded.shape[1]
    cols = MLA_HEADS * tq
    return pl.pallas_call(
        functools.partial(_mla_kernel, tq=tq, tk=tk, group=group),
        grid=(bsz, seq // tq),
        in_specs=[
            pl.BlockSpec((1, MLA_HEADS, tq, MLA_QK), lambda b, i: (b, 0, i, 0)),
            pl.BlockSpec((1, seq_k, MLA_QK), lambda b, i: (b, 0, 0)),
            _const_spec((MLA_HEADS, MLA_KV_RANK, MLA_WIDTH)),
        ],
        out_specs=pl.BlockSpec((1, tq, MLA_WIDTH), lambda b, i: (b, i, 0)),
        out_shape=jax.ShapeDtypeStruct((bsz, seq, MLA_WIDTH), F32),
        scratch_shapes=[pltpu.VMEM((seq_k, cols), F32), pltpu.VMEM((MLA_KV_RANK, cols), F32)],
        compiler_params=pltpu.CompilerParams(
            dimension_semantics=("parallel", "arbitrary"), vmem_limit_bytes=VMEM_LIMIT),
        name="mla",
    )(qm, kc_padded, wuv_heads)


def _gelu_tanh(x):
    return 0.5 * x * (1.0 + jnp.tanh(math.sqrt(2.0 / math.pi) * (x + 0.044715 * (x * x * x))))


def _mix_out_kernel(h_ref, ys_ref, yw_ref, ym_ref, wglu_ref, bglu_ref, gh_ref, wout_ref, gpm_ref, gpre_ref,
                    wup_ref, wdn_ref, gpost_ref, o_ref, *, ff_chunk):
    z = _gelu_tanh(jnp.concatenate([ys_ref[0, 0], ys_ref[0, 1]], axis=1))
    gate = jnp.dot(z.astype(BF16), wglu_ref[...], preferred_element_type=F32) + bglu_ref[...]
    y_ssm = z * (1.0 / (1.0 + jnp.exp(-gate)))
    gh = gh_ref[...]
    mixed = jnp.concatenate([
        _rms(y_ssm, gh[:, 0:SSM_WIDTH]),
        _rms(yw_ref[0], gh[:, SSM_WIDTH:SSM_WIDTH + SWA_WIDTH]),
        _rms(ym_ref[0], gh[:, SSM_WIDTH + SWA_WIDTH:]),
    ], axis=1).astype(BF16)
    mo = jnp.dot(mixed, wout_ref[...], preferred_element_type=F32)
    h1 = h_ref[0] + _rms(mo, gpm_ref[...])
    a = _rms(h1, gpre_ref[...]).astype(BF16)
    f = jnp.zeros_like(h1)
    for c in range(D_FF // ff_chunk):
        up = jnp.dot(a, wup_ref[:, c * ff_chunk:(c + 1) * ff_chunk], preferred_element_type=F32)
        hid = jnp.square(jnp.maximum(up, 0.0)).astype(BF16)
        f += jnp.dot(hid, wdn_ref[c * ff_chunk:(c + 1) * ff_chunk, :], preferred_element_type=F32)
    o_ref[0] = h1 + _rms(f, gpost_ref[...])


def _mix_out(h, ys, yw, ym, wglu, bglu, gh, wout, gpm, gpre, wup, wdn, gpost, *, tm=384, ff_chunk=1024):
    bsz, seq, _ = h.shape
    row = lambda w: pl.BlockSpec((1, tm, w), lambda b, i: (b, i, 0))
    return pl.pallas_call(
        functools.partial(_mix_out_kernel, ff_chunk=ff_chunk),
        grid=(bsz, seq // tm),
        in_specs=[
            row(D_MODEL), pl.BlockSpec((1, 2, tm, 128), lambda b, i: (b, 0, i, 0)), row(SWA_WIDTH), row(MLA_WIDTH),
            _const_spec((SSM_WIDTH, SSM_WIDTH)), _const_spec((1, SSM_WIDTH)), _const_spec((1, D_MODEL)),
            _const_spec((D_MODEL, D_MODEL)), _const_spec((1, D_MODEL)), _const_spec((1, D_MODEL)),
            _const_spec((D_MODEL, D_FF)), _const_spec((D_FF, D_MODEL)), _const_spec((1, D_MODEL)),
        ],
        out_specs=row(D_MODEL),
        out_shape=jax.ShapeDtypeStruct((bsz, seq, D_MODEL), F32),
        compiler_params=pltpu.CompilerParams(
            dimension_semantics=("parallel", "parallel"), vmem_limit_bytes=VMEM_LIMIT),
        name="mix_out",
    )(h, ys, yw, ym, wglu, bglu, gh, wout, gpm, gpre, wup, wdn, gpost)


def _rot_half_cols(w):
    half = w.shape[-1] // 2
    return jnp.concatenate([-w[..., half:], w[..., :half]], axis=-1)


def _layer_weights(w_in, w_uq, w_uk, w_uv):
    z = lambda n: jnp.zeros((D_MODEL, n), w_in.dtype)
    s_cq = SSM_WIDTH + SWA_WIDTH + 2 * SWA_KV_WIDTH
    s_ckv = s_cq + MLA_Q_RANK
    s_kr = s_ckv + MLA_KV_RANK
    kr = w_in[:, s_kr:s_kr + MLA_ROPE]
    w1 = jnp.concatenate([
        w_in[:, :s_cq], w_in[:, s_cq:s_ckv], z(C_CKV - C_CQ - MLA_Q_RANK),
        w_in[:, s_ckv:s_kr], kr, _rot_half_cols(kr), z(C_END - C_KR - 2 * MLA_ROPE)], axis=1).astype(BF16)
    uq = w_uq.reshape(MLA_Q_RANK, MLA_HEADS, MLA_NOPE + MLA_ROPE)
    rope = uq[:, :, MLA_NOPE:]
    wuq = jnp.concatenate([
        uq[:, :, :MLA_NOPE].reshape(MLA_Q_RANK, -1), rope.reshape(MLA_Q_RANK, -1),
        _rot_half_cols(rope).reshape(MLA_Q_RANK, -1)], axis=1)
    wuq = jnp.concatenate([wuq, jnp.zeros((256 - MLA_Q_RANK, wuq.shape[1]), wuq.dtype)], axis=0).astype(BF16)
    uk = w_uk.reshape(MLA_KV_RANK, MLA_HEADS, MLA_NOPE).transpose(1, 2, 0)
    wuk_bd = _group_block_diag(uk).astype(BF16)
    uv = w_uv.reshape(MLA_KV_RANK, MLA_HEADS, MLA_V).transpose(1, 0, 2)
    wuv_heads = _group_block_diag(uv).reshape(MLA_HEADS, MLA_KV_RANK, MLA_WIDTH).astype(BF16)
    return w1, wuq, wuk_bd, wuv_heads


def kernel(x, meta_tokens, norm_pre_mix, norm_post_mix, norm_pre_mlp, norm_post_mlp, w_in, w_out, norm_heads,
           ssm_a_re, ssm_a_im, ssm_log_step, ssm_b_re, ssm_b_im, ssm_c_re, ssm_c_im, ssm_d, ssm_w_glu,
           ssm_b_glu, swa_sinks, mla_q_norm, mla_kv_norm, mla_w_uq, mla_w_uk, mla_w_uv, w_mlp_up, w_mlp_down):
    bsz, seq_real, d = x.shape
    depth = w_in.shape[0]
    meta = jnp.broadcast_to(meta_tokens.astype(x.dtype)[None], (bsz, N_META, d))
    h = jnp.concatenate([jnp.zeros((bsz, PAD, d), x.dtype), meta, x], axis=1)
    seq = h.shape[1]

    half = MLA_ROPE // 2
    inv_freq = ROPE_THETA ** (-jnp.arange(half, dtype=F32) / half)
    ang = (jnp.arange(seq) - PAD).astype(F32)[:, None] * inv_freq[None, :]
    cos_t = jnp.tile(jnp.cos(ang), (1, 2 * MLA_HEADS))
    sin_t = jnp.tile(jnp.sin(ang), (1, 2 * MLA_HEADS))
    slopes = 2.0 ** (-8.0 * jnp.arange(1, SWA_HEADS + 1, dtype=F32) / SWA_HEADS)
    per_row = lambda v: jnp.repeat(v.reshape(SWA_KV_HEADS, SWA_GROUP), BLOCK, axis=1)[:, None, :]
    slope_rows = per_row(slopes)

    r_mat, p_mat, q_mat, lam = _s5_operators(
        ssm_a_re, ssm_a_im, ssm_log_step, ssm_b_re, ssm_b_im, ssm_c_re, ssm_c_im)
    row = lambda v: v.reshape(1, -1).astype(F32)

    for l in range(depth):
        w1, wuq, wuk_bd, wuv_heads = _layer_weights(w_in[l], mla_w_uq[l], mla_w_uk[l], mla_w_uv[l])
        gq = jnp.concatenate([mla_q_norm[l], jnp.zeros((256 - MLA_Q_RANK,), F32)]).reshape(1, 256)
        u, q, kv, qm, kc = _mix_in(h, row(norm_pre_mix[l]), w1, gq, row(mla_kv_norm[l]), wuq, wuk_bd,
                                   cos_t, sin_t)
        d_fold = jnp.tile(ssm_d[l].astype(F32), SSM_CHUNK).reshape(1, SSM_FOLD)
        y_ssm = _s5_sequence(u, r_mat[l], p_mat[l], q_mat[l], lam[l], d_fold)
        y_swa = _swa(q, kv, slope_rows, per_row(swa_sinks[l].astype(F32)))
        kc_padded = jnp.pad(kc, ((0, 0), (0, (-seq) % 256), (0, 0)))
        y_mla = _mla(qm, kc_padded, wuv_heads)
        h = _mix_out(
            h, y_ssm, y_swa, y_mla,
            ssm_w_glu[l].astype(BF16), row(ssm_b_glu[l]), row(norm_heads[l]), w_out[l].astype(BF16),
            row(norm_post_mix[l]), row(norm_pre_mlp[l]), w_mlp_up[l].astype(BF16), w_mlp_down[l].astype(BF16),
            row(norm_post_mlp[l]))
    return h[:, PAD + N_META:, :]
```

```python
import functools
import math

import jax
import jax.numpy as jnp
from jax import lax
from jax.experimental import pallas as pl
from jax.experimental.pallas import tpu as pltpu

F32 = jnp.float32
BF16 = jnp.bfloat16

D_MODEL = 1024
N_META = 16
BLOCK = 128
PAD = BLOCK - N_META
SSM_GROUPS = 16
SSM_GROUP_CH = 16
SSM_WIDTH = SSM_GROUPS * SSM_GROUP_CH
SSM_STATE = 64
SWA_HEADS = 8
SWA_KV_HEADS = 2
SWA_GROUP = SWA_HEADS // SWA_KV_HEADS
HEAD_DIM = 64
SWA_WIDTH = SWA_HEADS * HEAD_DIM
SWA_KV_WIDTH = SWA_KV_HEADS * HEAD_DIM
MLA_HEADS = 4
MLA_Q_RANK = 192
MLA_KV_RANK = 128
MLA_NOPE = 64
MLA_ROPE = 32
MLA_V = 64
MLA_WIDTH = MLA_HEADS * MLA_V
MLA_BLOCKS = 40
MLA_EXT = 48
MLA_ACC = MLA_KV_RANK + 8
LOG2E = 1.4426950408889634
MLA_QK = MLA_KV_RANK + MLA_EXT + MLA_ROPE
ROPE_THETA = 10000.0
D_FF = 4 * D_MODEL
EPS = 1e-6
NEG = -1e30

SSM_CHUNK = 8
SSM_FOLD = SSM_CHUNK * SSM_WIDTH
SSM_MODES = SSM_GROUPS * SSM_STATE

C_U, C_Q, C_K, C_CQ, C_CKV, C_KR, C_END = 0, 256, 768, 1024, 1280, 1408, 1536

MLA_KEY_ROWS = 512

VMEM_LIMIT = 56 * 1024 * 1024


def _rms(x, g, n=None):
    n = x.shape[-1] if n is None else n
    ms = jnp.sum(x * x, axis=-1, keepdims=True) * (1.0 / n)
    return x * lax.rsqrt(ms + EPS) * g


def _const_spec(shape):
    nd = len(shape)
    return pl.BlockSpec(shape, lambda *_: (0,) * nd, pipeline_mode=pl.Buffered(1))


def _mix_in_kernel(h_ref, g_ref, w1_ref, gq_ref, gkv_ref, wuq_ref, wuk_ref, cos_ref, sin_ref, qx_ref, kx_ref,
                   u_ref, q_ref, kv_ref, qm_ref, kc_ref, *, tm):
    j = pl.program_id(1)
    a = _rms(h_ref[0], g_ref[...]).astype(BF16)
    proj = jnp.dot(a, w1_ref[...], preferred_element_type=F32)
    row = j * tm + lax.broadcasted_iota(jnp.int32, (tm, 1), 0)
    for c in range(2):
        u_ref[0, c] = jnp.where(row >= PAD, proj[:, C_U + c * 128:C_U + (c + 1) * 128], 0.0)
    q_ref[0] = (proj[:, C_Q:C_K] * (HEAD_DIM ** -0.5)).astype(BF16)
    kv_ref[0] = proj[:, C_K:C_CQ].astype(BF16)

    scale = LOG2E * (MLA_NOPE + MLA_ROPE) ** -0.5
    cos = cos_ref[...]
    sin = sin_ref[...]
    qn = _rms(proj[:, C_CQ:C_CKV], gq_ref[...], n=MLA_Q_RANK).astype(BF16)
    q2 = jnp.dot(qn, wuq_ref[...], preferred_element_type=F32)
    qr = (q2[:, 256:384] * cos + q2[:, 384:512] * sin) * scale
    qa = jnp.dot(q2[:, 0:256].astype(BF16), wuk_ref[...], preferred_element_type=F32) * scale
    c = _rms(proj[:, C_CKV:C_KR], gkv_ref[...])
    krr = proj[:, C_KR:C_END]
    kr = krr[:, 0:MLA_ROPE] * cos[:, 0:MLA_ROPE] + krr[:, MLA_ROPE:2 * MLA_ROPE] * sin[:, 0:MLA_ROPE]
    ext = slice(MLA_KV_RANK, MLA_KV_RANK + MLA_EXT)
    rot = slice(MLA_KV_RANK + MLA_EXT, MLA_QK)
    for hd in range(MLA_HEADS):
        qm_ref[0, hd, :, 0:MLA_KV_RANK] = qa[:, hd * MLA_KV_RANK:(hd + 1) * MLA_KV_RANK].astype(BF16)
        qm_ref[0, hd, :, ext] = qx_ref[:, 0:MLA_EXT]
        qm_ref[0, hd, :, rot] = qr[:, hd * MLA_ROPE:(hd + 1) * MLA_ROPE].astype(BF16)
    kc_ref[0, :, 0:MLA_KV_RANK] = c.astype(BF16)
    kc_ref[0, :, ext] = kx_ref[:, 0:MLA_EXT]
    kc_ref[0, :, rot] = kr.astype(BF16)


def _mix_in(h, g, w1, gq, gkv, wuq, wuk, cos_t, sin_t, qx_t, kx_t, *, tm=384):
    bsz, seq, _ = h.shape
    grid = (bsz, seq // tm)
    row3 = lambda b, j: (b, j, 0)
    return pl.pallas_call(
        functools.partial(_mix_in_kernel, tm=tm),
        grid=grid,
        in_specs=[
            pl.BlockSpec((1, tm, D_MODEL), row3),
            _const_spec((1, D_MODEL)),
            _const_spec((D_MODEL, C_END)),
            _const_spec((1, 256)),
            _const_spec((1, MLA_KV_RANK)),
            _const_spec((256, 512)),
            _const_spec((256, 512)),
            pl.BlockSpec((tm, 128), lambda b, j: (j, 0)),
            pl.BlockSpec((tm, 128), lambda b, j: (j, 0)),
            pl.BlockSpec((tm, 128), lambda b, j: (j, 0)),
            pl.BlockSpec((tm, 128), lambda b, j: (j, 0)),
        ],
        out_specs=[
            pl.BlockSpec((1, 2, tm, 128), lambda b, j: (b, 0, j, 0)),
            pl.BlockSpec((1, tm, SWA_WIDTH), row3),
            pl.BlockSpec((1, tm, 2 * SWA_KV_WIDTH), row3),
            pl.BlockSpec((1, MLA_HEADS, tm, MLA_QK), lambda b, j: (b, 0, j, 0)),
            pl.BlockSpec((1, tm, MLA_QK), row3),
        ],
        out_shape=[
            jax.ShapeDtypeStruct((bsz, 2, seq, 128), F32),
            jax.ShapeDtypeStruct((bsz, seq, SWA_WIDTH), BF16),
            jax.ShapeDtypeStruct((bsz, seq, 2 * SWA_KV_WIDTH), BF16),
            jax.ShapeDtypeStruct((bsz, MLA_HEADS, seq, MLA_QK), BF16),
            jax.ShapeDtypeStruct((bsz, seq, MLA_QK), BF16),
        ],
        compiler_params=pltpu.CompilerParams(
            dimension_semantics=("parallel", "parallel"), vmem_limit_bytes=VMEM_LIMIT),
        name="mix_in",
    )(h, g, w1, gq, gkv, wuq, wuk, cos_t, sin_t, qx_t, kx_t)


def _split_bf16(x):
    hi = x.astype(BF16)
    lo = (x - hi.astype(F32)).astype(BF16)
    return hi, lo


def _bdot3(a, b):
    ah, al = _split_bf16(a)
    bh, bl = _split_bf16(b)
    dn = (((2,), (2,)), ((0,), (0,)))
    f = lambda x, y: lax.dot_general(x, y, dn, preferred_element_type=F32)
    return f(ah, bh) + f(ah, bl) + f(al, bh)


def _expand_groups(a2, row_shift, col_shift, n_cols):
    rows, w = a2.shape
    tile = (lax.broadcasted_iota(jnp.int32, (w, n_cols), 1) & (w - 1)) == lax.broadcasted_iota(
        jnp.int32, (w, n_cols), 0)
    wide = jnp.dot(a2.astype(BF16), jnp.where(tile, 1.0, 0.0).astype(BF16), preferred_element_type=F32)
    same = (lax.broadcasted_iota(jnp.int32, (rows, n_cols), 0) >> row_shift) == (
        lax.broadcasted_iota(jnp.int32, (rows, n_cols), 1) >> col_shift)
    return jnp.where(same, wide, 0.0)


def _s5_operator_kernel(ar_ref, ai_ref, ls_ref, btr_ref, bti_ref, cr_ref, ci_ref,
                        r_ref, p_ref, q_ref, l8r_ref, l8i_ref, pwr_sc, pwi_sc, kk_sc):
    s = pl.program_id(1)
    ar, ai = ar_ref[0], ai_ref[0]
    step = jnp.exp(ls_ref[0])
    lr = jnp.exp(ar * step) * jnp.cos(ai * step)
    li = jnp.exp(ar * step) * jnp.sin(ai * step)

    den = ar * ar + ai * ai
    nr, ni = lr - 1.0, li
    coef_re = (nr * ar + ni * ai) / den
    coef_im = (ni * ar - nr * ai) / den
    btr, bti = btr_ref[0], bti_ref[0]
    bb_re = coef_re * btr - coef_im * bti
    bb_im = coef_re * bti + coef_im * btr
    cr, ci = cr_ref[0], ci_ref[0]

    def c_lam(k):
        qr_, qi_ = pwr_sc[k], pwi_sc[k]
        return cr * qr_ - ci * qi_, cr * qi_ + ci * qr_

    gc = SSM_GROUPS * SSM_GROUP_CH

    @pl.when(s == 0)
    def _():
        pr, pi = jnp.ones_like(lr), jnp.zeros_like(li)
        for k in range(SSM_CHUNK + 1):
            pwr_sc[k] = pr
            pwi_sc[k] = pi
            pr, pi = pr * lr - pi * li, pr * li + pi * lr
        for k in range(SSM_CHUNK):
            clr, cli = c_lam(k)
            kk_sc[k] = (_bdot3(bb_re, clr) - _bdot3(bb_im, cli)).reshape(gc, SSM_GROUP_CH)

    for t in range(SSM_CHUNK):
        lag = t - s
        blk = _expand_groups(kk_sc[jnp.maximum(lag, 0)], 4, 4, gc)
        r_ref[0, :, t * gc:(t + 1) * gc] = jnp.where(lag >= 0, blk, 0.0).astype(BF16)

    pw_r, pw_i = pwr_sc[SSM_CHUNK - 1 - s], pwi_sc[SSM_CHUNK - 1 - s]
    p_re = (pw_r * bb_re - pw_i * bb_im).reshape(gc, SSM_STATE)
    p_im = (pw_r * bb_im + pw_i * bb_re).reshape(gc, SSM_STATE)
    p_ref[0, :, 0:SSM_MODES] = _expand_groups(p_re, 4, 6, SSM_MODES).astype(BF16)
    p_ref[0, :, SSM_MODES:2 * SSM_MODES] = _expand_groups(p_im, 4, 6, SSM_MODES).astype(BF16)

    clr, cli = c_lam(s + 1)
    q_ref[0, 0:SSM_MODES, :] = _expand_groups(clr.reshape(gc, SSM_STATE), 4, 6, SSM_MODES).T.astype(BF16)
    q_ref[0, SSM_MODES:2 * SSM_MODES, :] = (
        -_expand_groups(cli.reshape(gc, SSM_STATE), 4, 6, SSM_MODES)).T.astype(BF16)

    l8r_ref[0] = pwr_sc[SSM_CHUNK]
    l8i_ref[0] = pwi_sc[SSM_CHUNK]


def _s5_operators(a_re, a_im, log_step, b_re, b_im, c_re, c_im):
    depth = a_re.shape[0]
    G, P, C, T = SSM_GROUPS, SSM_STATE, SSM_GROUP_CH, SSM_CHUNK
    lay = lambda *shape: pl.BlockSpec((1,) + shape, lambda l, s: (l,) + (0,) * len(shape))
    ls = jnp.broadcast_to(log_step[:, :, None, None], (depth, G, 1, P))
    args = (a_re[:, :, None, :], a_im[:, :, None, :], ls,
            b_re.transpose(0, 1, 3, 2), b_im.transpose(0, 1, 3, 2), c_re, c_im)
    r_mat, p_mat, q_mat, l8r, l8i = pl.pallas_call(
        _s5_operator_kernel,
        grid=(depth, T),
        in_specs=[lay(G, 1, P)] * 3 + [lay(G, C, P)] * 4,
        out_specs=[
            pl.BlockSpec((1, SSM_WIDTH, SSM_FOLD), lambda l, s: (l, s, 0)),
            pl.BlockSpec((1, SSM_WIDTH, 2 * SSM_MODES), lambda l, s: (l, s, 0)),
            pl.BlockSpec((1, 2 * SSM_MODES, SSM_WIDTH), lambda l, s: (l, 0, s)),
            lay(G, 1, P), lay(G, 1, P)],
        out_shape=[
            jax.ShapeDtypeStruct((depth, SSM_FOLD, SSM_FOLD), BF16),
            jax.ShapeDtypeStruct((depth, SSM_FOLD, 2 * SSM_MODES), BF16),
            jax.ShapeDtypeStruct((depth, 2 * SSM_MODES, SSM_FOLD), BF16),
            jax.ShapeDtypeStruct((depth, G, 1, P), F32),
            jax.ShapeDtypeStruct((depth, G, 1, P), F32),
        ],
        scratch_shapes=[pltpu.VMEM((T + 1, G, 1, P), F32), pltpu.VMEM((T + 1, G, 1, P), F32),
                        pltpu.VMEM((T, G * C, C), F32)],
        compiler_params=pltpu.CompilerParams(dimension_semantics=("parallel", "arbitrary")),
        name="s5_operators",
    )(*args)
    lam = jnp.concatenate([l8r.reshape(depth, 1, SSM_MODES), l8i.reshape(depth, 1, SSM_MODES)], axis=2)
    return r_mat, p_mat, q_mat, lam


def _group_block_diag(x):
    *lead, G, r, c = x.shape
    eye = jnp.eye(G, dtype=x.dtype)
    y = x[..., :, :, None, :] * eye[:, None, :, None]
    return y.reshape(*lead, G * r, G * c)


def _fold_tokens(u_ref, n_chunks):
    return jnp.concatenate(
        [u_ref[0, c, pl.ds(s, n_chunks, stride=SSM_CHUNK), :] for s in range(SSM_CHUNK) for c in range(2)], axis=1)


def _s5_state_kernel(u_ref, p_ref, lam_ref, x_ref, s_sc, x_sc, *, n_chunks):
    u = _fold_tokens(u_ref, n_chunks).astype(BF16)
    s_sc[...] = jnp.dot(u, p_ref[...], preferred_element_type=F32)
    lr = lam_ref[:, 0:SSM_MODES]
    li = lam_ref[:, SSM_MODES:2 * SSM_MODES]

    def body(n, carry):
        xr, xi = carry
        x_sc[pl.ds(n, 1), 0:SSM_MODES] = xr
        x_sc[pl.ds(n, 1), SSM_MODES:2 * SSM_MODES] = xi
        sr = s_sc[pl.ds(n, 1), 0:SSM_MODES]
        si = s_sc[pl.ds(n, 1), SSM_MODES:2 * SSM_MODES]
        return lr * xr - li * xi + sr, lr * xi + li * xr + si

    z = jnp.zeros((1, SSM_MODES), F32)
    lax.fori_loop(0, n_chunks, body, (z, z))
    x_ref[0] = x_sc[...].astype(BF16)


def _s5_out_kernel(u_ref, x_ref, r_ref, q_ref, d_ref, y_ref, *, n_chunks):
    u = _fold_tokens(u_ref, n_chunks)
    y = jnp.dot(u.astype(BF16), r_ref[...], preferred_element_type=F32)
    y += jnp.dot(x_ref[0], q_ref[...], preferred_element_type=F32)
    y += d_ref[...] * u
    for s in range(SSM_CHUNK):
        for c in range(2):
            lo = (2 * s + c) * 128
            y_ref[0, c, pl.ds(s, n_chunks, stride=SSM_CHUNK), :] = y[:, lo:lo + 128]


def _s5_sequence(u, r_mat, p_mat, q_mat, lam, d_fold):
    bsz, _, seq, _ = u.shape
    n_chunks = seq // SSM_CHUNK
    params = pltpu.CompilerParams(dimension_semantics=("parallel",), vmem_limit_bytes=VMEM_LIMIT)
    slabs = pl.BlockSpec((1, 2, seq, 128), lambda b: (b, 0, 0, 0))
    states = pl.BlockSpec((1, n_chunks, 2 * SSM_MODES), lambda b: (b, 0, 0))
    x_prev = pl.pallas_call(
        functools.partial(_s5_state_kernel, n_chunks=n_chunks), grid=(bsz,),
        in_specs=[slabs, _const_spec((SSM_FOLD, 2 * SSM_MODES)), _const_spec((1, 2 * SSM_MODES))],
        out_specs=states,
        out_shape=jax.ShapeDtypeStruct((bsz, n_chunks, 2 * SSM_MODES), BF16),
        scratch_shapes=[pltpu.VMEM((n_chunks, 2 * SSM_MODES), F32), pltpu.VMEM((n_chunks, 2 * SSM_MODES), F32)],
        compiler_params=params, name="s5_state",
    )(u, p_mat, lam)
    return pl.pallas_call(
        functools.partial(_s5_out_kernel, n_chunks=n_chunks), grid=(bsz,),
        in_specs=[slabs, states, _const_spec((SSM_FOLD, SSM_FOLD)), _const_spec((2 * SSM_MODES, SSM_FOLD)),
                  _const_spec((1, SSM_FOLD))],
        out_specs=slabs,
        out_shape=jax.ShapeDtypeStruct((bsz, 2, seq, 128), F32),
        compiler_params=params, name="s5_out",
    )(u, x_prev, r_mat, q_mat, d_fold)


def _swa_kernel(q_ref, kvp_ref, kvc_ref, bias_ref, sink_ref, o_ref, *, nsub):
    i = pl.program_id(1)
    cols = SWA_GROUP * BLOCK
    jl = lax.broadcasted_iota(jnp.int32, (BLOCK, cols), 0)
    t = lax.broadcasted_iota(jnp.int32, (BLOCK, cols), 1) & (BLOCK - 1)
    cur = jl <= t
    for r in range(nsub):
        blk = i * nsub + r
        rows = slice(r * BLOCK, (r + 1) * BLOCK)
        q = q_ref[0, rows, :]
        kvc = kvc_ref[0, rows, :]
        kvp = kvp_ref[0] if r == 0 else kvc_ref[0, (r - 1) * BLOCK:r * BLOCK, :]
        valid = jnp.where(cur, blk * BLOCK + jl, (blk - 1) * BLOCK + jl) >= PAD
        outs = []
        for hk in range(SWA_KV_HEADS):
            ks = slice(hk * HEAD_DIM, (hk + 1) * HEAD_DIM)
            vs = slice(SWA_KV_WIDTH + hk * HEAD_DIM, SWA_KV_WIDTH + (hk + 1) * HEAD_DIM)
            kcat = jnp.concatenate([kvp[:, ks], kvc[:, ks]], axis=0)
            vcat = jnp.concatenate([kvp[:, vs], kvc[:, vs]], axis=0)
            qs = jnp.concatenate(
                [q[:, (hk * SWA_GROUP + g) * HEAD_DIM:(hk * SWA_GROUP + g + 1) * HEAD_DIM]
                 for g in range(SWA_GROUP)], axis=0)
            sf = lax.dot_general(kcat, qs, (((1,), (1,)), ((), ())), preferred_element_type=F32)
            s = jnp.where(cur, sf[BLOCK:2 * BLOCK], sf[0:BLOCK]) + bias_ref[hk]
            s = jnp.where(valid, s, NEG)
            sink = sink_ref[hk]
            m = jnp.maximum(jnp.max(s, axis=0, keepdims=True), sink)
            p = jnp.exp(s - m)
            den = jnp.sum(p, axis=0, keepdims=True) + jnp.exp(sink - m)
            pf = jnp.concatenate([jnp.where(cur, 0.0, p), jnp.where(cur, p, 0.0)], axis=0).astype(BF16)
            o_t = lax.dot_general(vcat, pf, (((0,), (0,)), ((), ())), preferred_element_type=F32) / den
            outs += [o_t[:, g * BLOCK:(g + 1) * BLOCK] for g in range(SWA_GROUP)]
        o_ref[0, rows, :] = jnp.concatenate(outs, axis=0).T


def _swa(q, kv, bias, sink_rows, *, nsub=3):
    bsz, seq, _ = q.shape
    tm = nsub * BLOCK
    cols = SWA_GROUP * BLOCK
    return pl.pallas_call(
        functools.partial(_swa_kernel, nsub=nsub),
        grid=(bsz, seq // tm),
        in_specs=[
            pl.BlockSpec((1, tm, SWA_WIDTH), lambda b, i: (b, i, 0)),
            pl.BlockSpec((1, BLOCK, 2 * SWA_KV_WIDTH), lambda b, i: (b, jnp.maximum(nsub * i - 1, 0), 0)),
            pl.BlockSpec((1, tm, 2 * SWA_KV_WIDTH), lambda b, i: (b, i, 0)),
            _const_spec((SWA_KV_HEADS, BLOCK, cols)),
            _const_spec((SWA_KV_HEADS, 1, cols)),
        ],
        out_specs=pl.BlockSpec((1, tm, SWA_WIDTH), lambda b, i: (b, i, 0)),
        out_shape=jax.ShapeDtypeStruct((bsz, seq, SWA_WIDTH), F32),
        compiler_params=pltpu.CompilerParams(dimension_semantics=("parallel", "parallel")),
        name="swa",
    )(q, kv, kv, bias, sink_rows)


def _mla_kernel(q_ref, k_ref, wuv_ref, o_ref, s_sc, m_sc, acc_sc, *, tq, rows):
    i = pl.program_id(1)
    cols = MLA_HEADS * tq
    nsub = tq // BLOCK
    q = q_ref[0].reshape(cols, MLA_QK)
    n_groups = ((i + 1) * tq + rows - 1) // rows
    nt_dims = (((1,), (1,)), ((), ()))
    tn_dims = (((0,), (0,)), ((), ()))

    def fold(x, op):
        return op(x.reshape(x.shape[0] // 8, 8, x.shape[1]), axis=0)

    m_sc[...] = jnp.full((8, cols), NEG, F32)

    def scores(g, carry):
        start = pl.multiple_of(g * rows, rows)
        s = lax.dot_general(k_ref[0, pl.ds(start, rows), :], q, nt_dims, preferred_element_type=F32)
        s_sc[pl.ds(start, rows), :] = s
        m_sc[...] = jnp.maximum(m_sc[...], fold(s, jnp.max))
        return carry

    lax.fori_loop(0, n_groups, scores, 0)

    lane = lax.broadcasted_iota(jnp.int32, (1, MLA_QK), 1)
    onehot = (lane >= MLA_KV_RANK + 2) & (lane < MLA_KV_RANK + 2 + MLA_BLOCKS)
    hcols = MLA_HEADS * BLOCK
    causal = lax.broadcasted_iota(jnp.int32, (BLOCK, hcols), 0) <= (
        lax.broadcasted_iota(jnp.int32, (BLOCK, hcols), 1) & (BLOCK - 1))
    for r in range(nsub):
        row0 = pl.multiple_of((i * nsub + r) * BLOCK, BLOCK)
        qd = jnp.concatenate(
            [q[hd * tq + r * BLOCK:hd * tq + (r + 1) * BLOCK, :] for hd in range(MLA_HEADS)], axis=0)
        qd = jnp.where(onehot, jnp.zeros((), BF16), qd)
        sd = lax.dot_general(k_ref[0, pl.ds(row0, BLOCK), :], qd, nt_dims, preferred_element_type=F32)
        sd = jnp.where(causal, sd, NEG)
        for hd in range(MLA_HEADS):
            lo = hd * tq + r * BLOCK
            sdh = sd[:, hd * BLOCK:(hd + 1) * BLOCK]
            s_sc[pl.ds(row0, BLOCK), lo:lo + BLOCK] = sdh
            m_sc[:, lo:lo + BLOCK] = jnp.maximum(m_sc[:, lo:lo + BLOCK], fold(sdh, jnp.max))
    m = jnp.max(m_sc[...], axis=0, keepdims=True)

    acc_sc[...] = jnp.zeros((MLA_ACC, cols), F32)

    def weights(g, carry):
        start = pl.multiple_of(g * rows, rows)
        p = jnp.exp2(s_sc[pl.ds(start, rows), :] - m).astype(BF16)
        v = k_ref[0, pl.ds(start, rows), 0:MLA_ACC]
        acc_sc[...] += lax.dot_general(v, p, tn_dims, preferred_element_type=F32)
        return carry

    lax.fori_loop(0, n_groups, weights, 0)
    l_sum = acc_sc[MLA_KV_RANK:MLA_KV_RANK + 1, :]

    o = (acc_sc[0:MLA_KV_RANK, :] / l_sum).astype(BF16)
    y = jnp.zeros((tq, MLA_WIDTH), F32)
    for hd in range(MLA_HEADS):
        y += lax.dot_general(o[:, hd * tq:(hd + 1) * tq], wuv_ref[hd], tn_dims, preferred_element_type=F32)
    o_ref[0] = y


def _mla(qm, kc_padded, wuv_heads, *, tq=384, rows=MLA_KEY_ROWS):
    bsz, _, seq, _ = qm.shape
    seq_k = kc_padded.shape[1]
    cols = MLA_HEADS * tq
    return pl.pallas_call(
        functools.partial(_mla_kernel, tq=tq, rows=rows),
        grid=(bsz, seq // tq),
        in_specs=[
            pl.BlockSpec((1, MLA_HEADS, tq, MLA_QK), lambda b, i: (b, 0, i, 0)),
            pl.BlockSpec((1, seq_k, MLA_QK), lambda b, i: (b, 0, 0)),
            _const_spec((MLA_HEADS, MLA_KV_RANK, MLA_WIDTH)),
        ],
        out_specs=pl.BlockSpec((1, tq, MLA_WIDTH), lambda b, i: (b, i, 0)),
        out_shape=jax.ShapeDtypeStruct((bsz, seq, MLA_WIDTH), F32),
        scratch_shapes=[pltpu.VMEM((seq_k, cols), F32), pltpu.VMEM((8, cols), F32),
                        pltpu.VMEM((MLA_ACC, cols), F32)],
        compiler_params=pltpu.CompilerParams(
            dimension_semantics=("parallel", "arbitrary"), vmem_limit_bytes=VMEM_LIMIT),
        name="mla",
    )(qm, kc_padded, wuv_heads)


def _gelu_tanh(x):
    return 0.5 * x * (1.0 + jnp.tanh(math.sqrt(2.0 / math.pi) * (x + 0.044715 * (x * x * x))))


def _mix_out_kernel(h_ref, ys_ref, yw_ref, ym_ref, wglu_ref, bglu_ref, gh_ref, wout_ref, gpm_ref, gpre_ref,
                    wup_ref, wdn_ref, gpost_ref, o_ref, *, ff_chunk):
    z = _gelu_tanh(jnp.concatenate([ys_ref[0, 0], ys_ref[0, 1]], axis=1))
    gate = jnp.dot(z.astype(BF16), wglu_ref[...], preferred_element_type=F32) + bglu_ref[...]
    y_ssm = z * (1.0 / (1.0 + jnp.exp(-gate)))
    gh = gh_ref[...]
    mixed = jnp.concatenate([
        _rms(y_ssm, gh[:, 0:SSM_WIDTH]),
        _rms(yw_ref[0], gh[:, SSM_WIDTH:SSM_WIDTH + SWA_WIDTH]),
        _rms(ym_ref[0], gh[:, SSM_WIDTH + SWA_WIDTH:]),
    ], axis=1).astype(BF16)
    mo = jnp.dot(mixed, wout_ref[...], preferred_element_type=F32)
    h1 = h_ref[0] + _rms(mo, gpm_ref[...])
    a = _rms(h1, gpre_ref[...]).astype(BF16)
    f = jnp.zeros_like(h1)
    for c in range(D_FF // ff_chunk):
        up = jnp.dot(a, wup_ref[:, c * ff_chunk:(c + 1) * ff_chunk], preferred_element_type=F32)
        hid = jnp.square(jnp.maximum(up, 0.0)).astype(BF16)
        f += jnp.dot(hid, wdn_ref[c * ff_chunk:(c + 1) * ff_chunk, :], preferred_element_type=F32)
    o_ref[0] = h1 + _rms(f, gpost_ref[...])


def _mix_out(h, ys, yw, ym, wglu, bglu, gh, wout, gpm, gpre, wup, wdn, gpost, *, skip=0, tm=384, ff_chunk=1024):
    bsz, seq, _ = h.shape
    out_rows = seq - skip
    if skip:
        el = pl.Element
        first = lambda i: pl.multiple_of(skip + i * tm, BLOCK)
        row = lambda w: pl.BlockSpec((el(1), el(tm), el(w)), lambda b, i: (b, first(i), 0))
        slabs = pl.BlockSpec((el(1), el(2), el(tm), el(128)), lambda b, i: (b, 0, first(i), 0))
    else:
        row = lambda w: pl.BlockSpec((1, tm, w), lambda b, i: (b, i, 0))
        slabs = pl.BlockSpec((1, 2, tm, 128), lambda b, i: (b, 0, i, 0))
    return pl.pallas_call(
        functools.partial(_mix_out_kernel, ff_chunk=ff_chunk),
        grid=(bsz, out_rows // tm),
        in_specs=[
            row(D_MODEL), slabs, row(SWA_WIDTH), row(MLA_WIDTH),
            _const_spec((SSM_WIDTH, SSM_WIDTH)), _const_spec((1, SSM_WIDTH)), _const_spec((1, D_MODEL)),
            _const_spec((D_MODEL, D_MODEL)), _const_spec((1, D_MODEL)), _const_spec((1, D_MODEL)),
            _const_spec((D_MODEL, D_FF)), _const_spec((D_FF, D_MODEL)), _const_spec((1, D_MODEL)),
        ],
        out_specs=pl.BlockSpec((1, tm, D_MODEL), lambda b, i: (b, i, 0)),
        out_shape=jax.ShapeDtypeStruct((bsz, out_rows, D_MODEL), F32),
        compiler_params=pltpu.CompilerParams(
            dimension_semantics=("parallel", "parallel"), vmem_limit_bytes=VMEM_LIMIT),
        name="mix_out",
    )(h, ys, yw, ym, wglu, bglu, gh, wout, gpm, gpre, wup, wdn, gpost)


def _rot_half_cols(w):
    half = w.shape[-1] // 2
    return jnp.concatenate([-w[..., half:], w[..., :half]], axis=-1)


def _layer_weights(w_in, w_uq, w_uk, w_uv):
    z = lambda n: jnp.zeros((D_MODEL, n), w_in.dtype)
    s_cq = SSM_WIDTH + SWA_WIDTH + 2 * SWA_KV_WIDTH
    s_ckv = s_cq + MLA_Q_RANK
    s_kr = s_ckv + MLA_KV_RANK
    kr = w_in[:, s_kr:s_kr + MLA_ROPE]
    w1 = jnp.concatenate([
        w_in[:, :s_cq], w_in[:, s_cq:s_ckv], z(C_CKV - C_CQ - MLA_Q_RANK),
        w_in[:, s_ckv:s_kr], kr, _rot_half_cols(kr), z(C_END - C_KR - 2 * MLA_ROPE)], axis=1).astype(BF16)
    uq = w_uq.reshape(MLA_Q_RANK, MLA_HEADS, MLA_NOPE + MLA_ROPE)
    rope = uq[:, :, MLA_NOPE:]
    wuq = jnp.concatenate([
        uq[:, :, :MLA_NOPE].reshape(MLA_Q_RANK, -1), rope.reshape(MLA_Q_RANK, -1),
        _rot_half_cols(rope).reshape(MLA_Q_RANK, -1)], axis=1)
    wuq = jnp.concatenate([wuq, jnp.zeros((256 - MLA_Q_RANK, wuq.shape[1]), wuq.dtype)], axis=0).astype(BF16)
    uk = w_uk.reshape(MLA_KV_RANK, MLA_HEADS, MLA_NOPE).transpose(1, 2, 0)
    wuk_bd = _group_block_diag(uk).astype(BF16)
    uv = w_uv.reshape(MLA_KV_RANK, MLA_HEADS, MLA_V).transpose(1, 0, 2)
    wuv_heads = _group_block_diag(uv).reshape(MLA_HEADS, MLA_KV_RANK, MLA_WIDTH).astype(BF16)
    return w1, wuq, wuk_bd, wuv_heads


def kernel(x, meta_tokens, norm_pre_mix, norm_post_mix, norm_pre_mlp, norm_post_mlp, w_in, w_out, norm_heads,
           ssm_a_re, ssm_a_im, ssm_log_step, ssm_b_re, ssm_b_im, ssm_c_re, ssm_c_im, ssm_d, ssm_w_glu,
           ssm_b_glu, swa_sinks, mla_q_norm, mla_kv_norm, mla_w_uq, mla_w_uk, mla_w_uv, w_mlp_up, w_mlp_down):
    bsz, seq_real, d = x.shape
    depth = w_in.shape[0]
    meta = jnp.broadcast_to(meta_tokens.astype(x.dtype)[None], (bsz, N_META, d))
    h = jnp.concatenate([jnp.zeros((bsz, PAD, d), x.dtype), meta, x], axis=1)
    seq = h.shape[1]

    half = MLA_ROPE // 2
    inv_freq = ROPE_THETA ** (-jnp.arange(half, dtype=F32) / half)
    ang = (jnp.arange(seq) - PAD).astype(F32)[:, None] * inv_freq[None, :]
    cos_t = jnp.tile(jnp.cos(ang), (1, 2 * MLA_HEADS))
    sin_t = jnp.tile(jnp.sin(ang), (1, 2 * MLA_HEADS))
    seq_k = -(-seq // MLA_KEY_ROWS) * MLA_KEY_ROWS
    kblk = jnp.arange(seq_k)[:, None] // BLOCK
    bcol = jnp.arange(MLA_BLOCKS)[None, :]
    one = jnp.ones((seq_k, 1), F32)
    fill = jnp.zeros((seq_k, 128 - 2 - MLA_BLOCKS), F32)
    kx_t = jnp.concatenate([one, jnp.where(jnp.arange(seq_k)[:, None] < PAD, NEG, 0.0),
                            (kblk == bcol).astype(F32), fill], axis=1).astype(BF16)
    qx_t = jnp.concatenate([0.0 * one, one, jnp.where(bcol >= kblk, NEG, 0.0), fill], axis=1).astype(BF16)[:seq]
    key_tail = jnp.zeros((bsz, seq_k - seq, MLA_QK), BF16).at[:, :, MLA_KV_RANK:MLA_KV_RANK + MLA_EXT].set(
        kx_t[seq:, :MLA_EXT])
    slopes = 2.0 ** (-8.0 * jnp.arange(1, SWA_HEADS + 1, dtype=F32) / SWA_HEADS)
    per_row = lambda v: jnp.repeat(v.reshape(SWA_KV_HEADS, SWA_GROUP), BLOCK, axis=1)[:, None, :]
    jl = jnp.arange(BLOCK)[:, None]
    tq = jnp.arange(BLOCK)[None, :]
    dist = jnp.where(jl <= tq, tq - jl, tq - jl + BLOCK).astype(F32)
    swa_bias = -per_row(slopes) * jnp.tile(dist, (1, SWA_GROUP))[None]

    r_mat, p_mat, q_mat, lam = _s5_operators(
        ssm_a_re, ssm_a_im, ssm_log_step, ssm_b_re, ssm_b_im, ssm_c_re, ssm_c_im)
    row = lambda v: v.reshape(1, -1).astype(F32)

    for l in range(depth):
        w1, wuq, wuk_bd, wuv_heads = _layer_weights(w_in[l], mla_w_uq[l], mla_w_uk[l], mla_w_uv[l])
        gq = jnp.concatenate([mla_q_norm[l], jnp.zeros((256 - MLA_Q_RANK,), F32)]).reshape(1, 256)
        u, q, kv, qm, kc = _mix_in(h, row(norm_pre_mix[l]), w1, gq, row(mla_kv_norm[l]), wuq, wuk_bd,
                                   cos_t, sin_t, qx_t, kx_t[:seq])
        d_fold = jnp.tile(ssm_d[l].astype(F32), SSM_CHUNK).reshape(1, SSM_FOLD)
        y_ssm = _s5_sequence(u, r_mat[l], p_mat[l], q_mat[l], lam[l], d_fold)
        y_swa = _swa(q, kv, swa_bias, per_row(swa_sinks[l].astype(F32)))
        kc_padded = jnp.concatenate([kc, key_tail], axis=1)
        y_mla = _mla(qm, kc_padded, wuv_heads)
        last = l == depth - 1
        h = _mix_out(
            h, y_ssm, y_swa, y_mla,
            ssm_w_glu[l].astype(BF16), row(ssm_b_glu[l]), row(norm_heads[l]), w_out[l].astype(BF16),
            row(norm_post_mix[l]), row(norm_pre_mlp[l]), w_mlp_up[l].astype(BF16), w_mlp_down[l].astype(BF16),
            row(norm_post_mlp[l]), skip=PAD + N_META if last else 0, tm=512 if last else 384)
    return h
```

```python
import functools
import math

import jax
import jax.numpy as jnp
from jax import lax
from jax.experimental import pallas as pl
from jax.experimental.pallas import tpu as pltpu

F32 = jnp.float32
BF16 = jnp.bfloat16

D_MODEL = 1024
N_META = 16
BLOCK = 128
PAD = BLOCK - N_META
SSM_GROUPS = 16
SSM_GROUP_CH = 16
SSM_WIDTH = SSM_GROUPS * SSM_GROUP_CH
SSM_STATE = 64
SWA_HEADS = 8
SWA_KV_HEADS = 2
SWA_GROUP = SWA_HEADS // SWA_KV_HEADS
HEAD_DIM = 64
SWA_WIDTH = SWA_HEADS * HEAD_DIM
SWA_KV_WIDTH = SWA_KV_HEADS * HEAD_DIM
MLA_HEADS = 4
MLA_Q_RANK = 192
MLA_KV_RANK = 128
MLA_NOPE = 64
MLA_ROPE = 32
MLA_V = 64
MLA_WIDTH = MLA_HEADS * MLA_V
MLA_BLOCKS = 40
MLA_EXT = 48
MLA_ACC = MLA_KV_RANK + 8
LOG2E = 1.4426950408889634
MLA_QK = MLA_KV_RANK + MLA_EXT + MLA_ROPE
ROPE_THETA = 10000.0
D_FF = 4 * D_MODEL
EPS = 1e-6
NEG = -1e30

SSM_CHUNK = 8
SSM_FOLD = SSM_CHUNK * SSM_WIDTH
SSM_MODES = SSM_GROUPS * SSM_STATE

C_U, C_Q, C_K, C_CQ, C_CKV, C_KR, C_END = 0, 256, 768, 1024, 1280, 1408, 1536

MLA_KEY_ROWS = 512

VMEM_LIMIT = 56 * 1024 * 1024


def _rms(x, g, n=None):
    n = x.shape[-1] if n is None else n
    ms = jnp.sum(x * x, axis=-1, keepdims=True) * (1.0 / n)
    return x * lax.rsqrt(ms + EPS) * g


def _const_spec(shape):
    nd = len(shape)
    return pl.BlockSpec(shape, lambda *_: (0,) * nd, pipeline_mode=pl.Buffered(1))


def _mix_in_kernel(h_ref, g_ref, w1_ref, gq_ref, gkv_ref, wuq_ref, wuk_ref, cos_ref, sin_ref, qx_ref, kx_ref,
                   u_ref, q_ref, kv_ref, qm_ref, kc_ref, *, tm, nsub):
    j = pl.program_id(1)
    ts = tm // nsub
    scale = LOG2E * (MLA_NOPE + MLA_ROPE) ** -0.5
    ext = slice(MLA_KV_RANK, MLA_KV_RANK + MLA_EXT)
    rot = slice(MLA_KV_RANK + MLA_EXT, MLA_QK)
    for sub in range(nsub):
        rows = slice(sub * ts, (sub + 1) * ts)
        a = _rms(h_ref[0, rows, :], g_ref[...]).astype(BF16)
        proj = jnp.dot(a, w1_ref[...], preferred_element_type=F32)
        row = j * tm + sub * ts + lax.broadcasted_iota(jnp.int32, (ts, 1), 0)
        for c in range(2):
            u_ref[0, c, rows, :] = jnp.where(row >= PAD, proj[:, C_U + c * 128:C_U + (c + 1) * 128], 0.0)
        q_ref[0, rows, :] = (proj[:, C_Q:C_K] * (HEAD_DIM ** -0.5)).astype(BF16)
        kv_ref[0, rows, :] = proj[:, C_K:C_CQ].astype(BF16)

        cos = cos_ref[rows, :]
        sin = sin_ref[rows, :]
        qn = _rms(proj[:, C_CQ:C_CKV], gq_ref[...], n=MLA_Q_RANK).astype(BF16)
        q2 = jnp.dot(qn, wuq_ref[...], preferred_element_type=F32)
        qr = (q2[:, 256:384] * cos + q2[:, 384:512] * sin) * scale
        qa = jnp.dot(q2[:, 0:256].astype(BF16), wuk_ref[...], preferred_element_type=F32) * scale
        c = _rms(proj[:, C_CKV:C_KR], gkv_ref[...])
        krr = proj[:, C_KR:C_END]
        kr = krr[:, 0:MLA_ROPE] * cos[:, 0:MLA_ROPE] + krr[:, MLA_ROPE:2 * MLA_ROPE] * sin[:, 0:MLA_ROPE]
        for hd in range(MLA_HEADS):
            qm_ref[0, hd, rows, 0:MLA_KV_RANK] = qa[:, hd * MLA_KV_RANK:(hd + 1) * MLA_KV_RANK].astype(BF16)
            qm_ref[0, hd, rows, ext] = qx_ref[rows, 0:MLA_EXT]
            qm_ref[0, hd, rows, rot] = qr[:, hd * MLA_ROPE:(hd + 1) * MLA_ROPE].astype(BF16)
        kc_ref[0, rows, 0:MLA_KV_RANK] = c.astype(BF16)
        kc_ref[0, rows, ext] = kx_ref[rows, 0:MLA_EXT]
        kc_ref[0, rows, rot] = kr.astype(BF16)


def _mix_in(h, g, w1, gq, gkv, wuq, wuk, cos_t, sin_t, qx_t, kx_t, *, tm=2112, nsub=4):
    bsz, seq, _ = h.shape
    grid = (bsz, seq // tm)
    row3 = lambda b, j: (b, j, 0)
    return pl.pallas_call(
        functools.partial(_mix_in_kernel, tm=tm, nsub=nsub),
        grid=grid,
        in_specs=[
            pl.BlockSpec((1, tm, D_MODEL), row3),
            _const_spec((1, D_MODEL)),
            _const_spec((D_MODEL, C_END)),
            _const_spec((1, 256)),
            _const_spec((1, MLA_KV_RANK)),
            _const_spec((256, 512)),
            _const_spec((256, 512)),
            pl.BlockSpec((tm, 128), lambda b, j: (j, 0)),
            pl.BlockSpec((tm, 128), lambda b, j: (j, 0)),
            pl.BlockSpec((tm, 128), lambda b, j: (j, 0)),
            pl.BlockSpec((tm, 128), lambda b, j: (j, 0)),
        ],
        out_specs=[
            pl.BlockSpec((1, 2, tm, 128), lambda b, j: (b, 0, j, 0)),
            pl.BlockSpec((1, tm, SWA_WIDTH), row3),
            pl.BlockSpec((1, tm, 2 * SWA_KV_WIDTH), row3),
            pl.BlockSpec((1, MLA_HEADS, tm, MLA_QK), lambda b, j: (b, 0, j, 0)),
            pl.BlockSpec((1, tm, MLA_QK), row3),
        ],
        out_shape=[
            jax.ShapeDtypeStruct((bsz, 2, seq, 128), F32),
            jax.ShapeDtypeStruct((bsz, seq, SWA_WIDTH), BF16),
            jax.ShapeDtypeStruct((bsz, seq, 2 * SWA_KV_WIDTH), BF16),
            jax.ShapeDtypeStruct((bsz, MLA_HEADS, seq, MLA_QK), BF16),
            jax.ShapeDtypeStruct((bsz, seq, MLA_QK), BF16),
        ],
        compiler_params=pltpu.CompilerParams(
            dimension_semantics=("parallel", "parallel"), vmem_limit_bytes=VMEM_LIMIT),
        name="mix_in",
    )(h, g, w1, gq, gkv, wuq, wuk, cos_t, sin_t, qx_t, kx_t)


def _split_bf16(x):
    hi = x.astype(BF16)
    lo = (x - hi.astype(F32)).astype(BF16)
    return hi, lo


def _bdot3(a, b):
    ah, al = _split_bf16(a)
    bh, bl = _split_bf16(b)
    dn = (((2,), (2,)), ((0,), (0,)))
    f = lambda x, y: lax.dot_general(x, y, dn, preferred_element_type=F32)
    return f(ah, bh) + f(ah, bl) + f(al, bh)


def _expand_groups(a2, row_shift, col_shift, n_cols):
    rows, w = a2.shape
    tile = (lax.broadcasted_iota(jnp.int32, (w, n_cols), 1) & (w - 1)) == lax.broadcasted_iota(
        jnp.int32, (w, n_cols), 0)
    wide = jnp.dot(a2.astype(BF16), jnp.where(tile, 1.0, 0.0).astype(BF16), preferred_element_type=F32)
    same = (lax.broadcasted_iota(jnp.int32, (rows, n_cols), 0) >> row_shift) == (
        lax.broadcasted_iota(jnp.int32, (rows, n_cols), 1) >> col_shift)
    return jnp.where(same, wide, 0.0)


def _s5_operator_kernel(ar_ref, ai_ref, ls_ref, btr_ref, bti_ref, cr_ref, ci_ref,
                        r_ref, p_ref, q_ref, l8r_ref, l8i_ref, pwr_sc, pwi_sc, kk_sc):
    s = pl.program_id(1)
    ar, ai = ar_ref[0], ai_ref[0]
    step = jnp.exp(ls_ref[0])
    lr = jnp.exp(ar * step) * jnp.cos(ai * step)
    li = jnp.exp(ar * step) * jnp.sin(ai * step)

    den = ar * ar + ai * ai
    nr, ni = lr - 1.0, li
    coef_re = (nr * ar + ni * ai) / den
    coef_im = (ni * ar - nr * ai) / den
    btr, bti = btr_ref[0], bti_ref[0]
    bb_re = coef_re * btr - coef_im * bti
    bb_im = coef_re * bti + coef_im * btr
    cr, ci = cr_ref[0], ci_ref[0]

    def c_lam(k):
        qr_, qi_ = pwr_sc[k], pwi_sc[k]
        return cr * qr_ - ci * qi_, cr * qi_ + ci * qr_

    gc = SSM_GROUPS * SSM_GROUP_CH

    @pl.when(s == 0)
    def _():
        pr, pi = jnp.ones_like(lr), jnp.zeros_like(li)
        for k in range(SSM_CHUNK + 1):
            pwr_sc[k] = pr
            pwi_sc[k] = pi
            pr, pi = pr * lr - pi * li, pr * li + pi * lr
        for k in range(SSM_CHUNK):
            clr, cli = c_lam(k)
            kk_sc[k] = (_bdot3(bb_re, clr) - _bdot3(bb_im, cli)).reshape(gc, SSM_GROUP_CH)

    for t in range(SSM_CHUNK):
        lag = t - s
        blk = _expand_groups(kk_sc[jnp.maximum(lag, 0)], 4, 4, gc)
        r_ref[0, :, t * gc:(t + 1) * gc] = jnp.where(lag >= 0, blk, 0.0).astype(BF16)

    pw_r, pw_i = pwr_sc[SSM_CHUNK - 1 - s], pwi_sc[SSM_CHUNK - 1 - s]
    p_re = (pw_r * bb_re - pw_i * bb_im).reshape(gc, SSM_STATE)
    p_im = (pw_r * bb_im + pw_i * bb_re).reshape(gc, SSM_STATE)
    p_ref[0, :, 0:SSM_MODES] = _expand_groups(p_re, 4, 6, SSM_MODES).astype(BF16)
    p_ref[0, :, SSM_MODES:2 * SSM_MODES] = _expand_groups(p_im, 4, 6, SSM_MODES).astype(BF16)

    clr, cli = c_lam(s + 1)
    q_ref[0, 0:SSM_MODES, :] = _expand_groups(clr.reshape(gc, SSM_STATE), 4, 6, SSM_MODES).T.astype(BF16)
    q_ref[0, SSM_MODES:2 * SSM_MODES, :] = (
        -_expand_groups(cli.reshape(gc, SSM_STATE), 4, 6, SSM_MODES)).T.astype(BF16)

    l8r_ref[0] = pwr_sc[SSM_CHUNK]
    l8i_ref[0] = pwi_sc[SSM_CHUNK]


def _s5_operators(a_re, a_im, log_step, b_re, b_im, c_re, c_im):
    depth = a_re.shape[0]
    G, P, C, T = SSM_GROUPS, SSM_STATE, SSM_GROUP_CH, SSM_CHUNK
    lay = lambda *shape: pl.BlockSpec((1,) + shape, lambda l, s: (l,) + (0,) * len(shape))
    ls = jnp.broadcast_to(log_step[:, :, None, None], (depth, G, 1, P))
    args = (a_re[:, :, None, :], a_im[:, :, None, :], ls,
            b_re.transpose(0, 1, 3, 2), b_im.transpose(0, 1, 3, 2), c_re, c_im)
    r_mat, p_mat, q_mat, l8r, l8i = pl.pallas_call(
        _s5_operator_kernel,
        grid=(depth, T),
        in_specs=[lay(G, 1, P)] * 3 + [lay(G, C, P)] * 4,
        out_specs=[
            pl.BlockSpec((1, SSM_WIDTH, SSM_FOLD), lambda l, s: (l, s, 0)),
            pl.BlockSpec((1, SSM_WIDTH, 2 * SSM_MODES), lambda l, s: (l, s, 0)),
            pl.BlockSpec((1, 2 * SSM_MODES, SSM_WIDTH), lambda l, s: (l, 0, s)),
            lay(G, 1, P), lay(G, 1, P)],
        out_shape=[
            jax.ShapeDtypeStruct((depth, SSM_FOLD, SSM_FOLD), BF16),
            jax.ShapeDtypeStruct((depth, SSM_FOLD, 2 * SSM_MODES), BF16),
            jax.ShapeDtypeStruct((depth, 2 * SSM_MODES, SSM_FOLD), BF16),
            jax.ShapeDtypeStruct((depth, G, 1, P), F32),
            jax.ShapeDtypeStruct((depth, G, 1, P), F32),
        ],
        scratch_shapes=[pltpu.VMEM((T + 1, G, 1, P), F32), pltpu.VMEM((T + 1, G, 1, P), F32),
                        pltpu.VMEM((T, G * C, C), F32)],
        compiler_params=pltpu.CompilerParams(dimension_semantics=("parallel", "arbitrary")),
        name="s5_operators",
    )(*args)
    lam = jnp.concatenate([l8r.reshape(depth, 1, SSM_MODES), l8i.reshape(depth, 1, SSM_MODES)], axis=2)
    return r_mat, p_mat, q_mat, lam


def _group_block_diag(x):
    *lead, G, r, c = x.shape
    eye = jnp.eye(G, dtype=x.dtype)
    y = x[..., :, :, None, :] * eye[:, None, :, None]
    return y.reshape(*lead, G * r, G * c)


def _fold_tokens(u_ref, n_chunks):
    return jnp.concatenate(
        [u_ref[0, c, pl.ds(s, n_chunks, stride=SSM_CHUNK), :] for s in range(SSM_CHUNK) for c in range(2)], axis=1)


def _s5_state_kernel(u_ref, p_ref, lam_ref, x_ref, s_sc, x_sc, *, n_chunks):
    u = _fold_tokens(u_ref, n_chunks).astype(BF16)
    s_sc[...] = jnp.dot(u, p_ref[...], preferred_element_type=F32)
    lr = lam_ref[:, 0:SSM_MODES]
    li = lam_ref[:, SSM_MODES:2 * SSM_MODES]

    def body(n, carry):
        xr, xi = carry
        x_sc[pl.ds(n, 1), 0:SSM_MODES] = xr
        x_sc[pl.ds(n, 1), SSM_MODES:2 * SSM_MODES] = xi
        sr = s_sc[pl.ds(n, 1), 0:SSM_MODES]
        si = s_sc[pl.ds(n, 1), SSM_MODES:2 * SSM_MODES]
        return lr * xr - li * xi + sr, lr * xi + li * xr + si

    z = jnp.zeros((1, SSM_MODES), F32)
    lax.fori_loop(0, n_chunks, body, (z, z))
    x_ref[0] = x_sc[...].astype(BF16)


def _s5_out_kernel(u_ref, x_ref, r_ref, q_ref, d_ref, y_ref, *, n_chunks):
    u = _fold_tokens(u_ref, n_chunks)
    y = jnp.dot(u.astype(BF16), r_ref[...], preferred_element_type=F32)
    y += jnp.dot(x_ref[0], q_ref[...], preferred_element_type=F32)
    y += d_ref[...] * u
    for s in range(SSM_CHUNK):
        for c in range(2):
            lo = (2 * s + c) * 128
            y_ref[0, c, pl.ds(s, n_chunks, stride=SSM_CHUNK), :] = y[:, lo:lo + 128]


def _s5_sequence(u, r_mat, p_mat, q_mat, lam, d_fold):
    bsz, _, seq, _ = u.shape
    n_chunks = seq // SSM_CHUNK
    params = pltpu.CompilerParams(dimension_semantics=("parallel",), vmem_limit_bytes=VMEM_LIMIT)
    slabs = pl.BlockSpec((1, 2, seq, 128), lambda b: (b, 0, 0, 0))
    states = pl.BlockSpec((1, n_chunks, 2 * SSM_MODES), lambda b: (b, 0, 0))
    x_prev = pl.pallas_call(
        functools.partial(_s5_state_kernel, n_chunks=n_chunks), grid=(bsz,),
        in_specs=[slabs, _const_spec((SSM_FOLD, 2 * SSM_MODES)), _const_spec((1, 2 * SSM_MODES))],
        out_specs=states,
        out_shape=jax.ShapeDtypeStruct((bsz, n_chunks, 2 * SSM_MODES), BF16),
        scratch_shapes=[pltpu.VMEM((n_chunks, 2 * SSM_MODES), F32), pltpu.VMEM((n_chunks, 2 * SSM_MODES), F32)],
        compiler_params=params, name="s5_state",
    )(u, p_mat, lam)
    return pl.pallas_call(
        functools.partial(_s5_out_kernel, n_chunks=n_chunks), grid=(bsz,),
        in_specs=[slabs, states, _const_spec((SSM_FOLD, SSM_FOLD)), _const_spec((2 * SSM_MODES, SSM_FOLD)),
                  _const_spec((1, SSM_FOLD))],
        out_specs=slabs,
        out_shape=jax.ShapeDtypeStruct((bsz, 2, seq, 128), F32),
        compiler_params=params, name="s5_out",
    )(u, x_prev, r_mat, q_mat, d_fold)


def _swa_kernel(q_ref, kvp_ref, kvc_ref, bias_ref, sink_ref, o_ref, *, nsub):
    i = pl.program_id(1)
    cols = SWA_GROUP * BLOCK
    jl = lax.broadcasted_iota(jnp.int32, (BLOCK, cols), 0)
    t = lax.broadcasted_iota(jnp.int32, (BLOCK, cols), 1) & (BLOCK - 1)
    cur = jl <= t
    for r in range(nsub):
        blk = i * nsub + r
        rows = slice(r * BLOCK, (r + 1) * BLOCK)
        q = q_ref[0, rows, :]
        kvc = kvc_ref[0, rows, :]
        kvp = kvp_ref[0] if r == 0 else kvc_ref[0, (r - 1) * BLOCK:r * BLOCK, :]
        valid = jnp.where(cur, blk * BLOCK + jl, (blk - 1) * BLOCK + jl) >= PAD
        outs = []
        for hk in range(SWA_KV_HEADS):
            ks = slice(hk * HEAD_DIM, (hk + 1) * HEAD_DIM)
            vs = slice(SWA_KV_WIDTH + hk * HEAD_DIM, SWA_KV_WIDTH + (hk + 1) * HEAD_DIM)
            kcat = jnp.concatenate([kvp[:, ks], kvc[:, ks]], axis=0)
            vcat = jnp.concatenate([kvp[:, vs], kvc[:, vs]], axis=0)
            qs = jnp.concatenate(
                [q[:, (hk * SWA_GROUP + g) * HEAD_DIM:(hk * SWA_GROUP + g + 1) * HEAD_DIM]
                 for g in range(SWA_GROUP)], axis=0)
            sf = lax.dot_general(kcat, qs, (((1,), (1,)), ((), ())), preferred_element_type=F32)
            s = jnp.where(cur, sf[BLOCK:2 * BLOCK], sf[0:BLOCK]) + bias_ref[hk]
            s = jnp.where(valid, s, NEG)
            sink = sink_ref[hk]
            m = jnp.maximum(jnp.max(s, axis=0, keepdims=True), sink)
            p = jnp.exp(s - m)
            den = jnp.sum(p, axis=0, keepdims=True) + jnp.exp(sink - m)
            pf = jnp.concatenate([jnp.where(cur, 0.0, p), jnp.where(cur, p, 0.0)], axis=0).astype(BF16)
            o_t = lax.dot_general(vcat, pf, (((0,), (0,)), ((), ())), preferred_element_type=F32) / den
            outs += [o_t[:, g * BLOCK:(g + 1) * BLOCK] for g in range(SWA_GROUP)]
        o_ref[0, rows, :] = jnp.concatenate(outs, axis=0).T


def _swa(q, kv, bias, sink_rows, *, nsub=3):
    bsz, seq, _ = q.shape
    tm = nsub * BLOCK
    cols = SWA_GROUP * BLOCK
    return pl.pallas_call(
        functools.partial(_swa_kernel, nsub=nsub),
        grid=(bsz, seq // tm),
        in_specs=[
            pl.BlockSpec((1, tm, SWA_WIDTH), lambda b, i: (b, i, 0)),
            pl.BlockSpec((1, BLOCK, 2 * SWA_KV_WIDTH), lambda b, i: (b, jnp.maximum(nsub * i - 1, 0), 0)),
            pl.BlockSpec((1, tm, 2 * SWA_KV_WIDTH), lambda b, i: (b, i, 0)),
            _const_spec((SWA_KV_HEADS, BLOCK, cols)),
            _const_spec((SWA_KV_HEADS, 1, cols)),
        ],
        out_specs=pl.BlockSpec((1, tm, SWA_WIDTH), lambda b, i: (b, i, 0)),
        out_shape=jax.ShapeDtypeStruct((bsz, seq, SWA_WIDTH), F32),
        compiler_params=pltpu.CompilerParams(dimension_semantics=("parallel", "parallel")),
        name="swa",
    )(q, kv, kv, bias, sink_rows)


def _mla_kernel(q_ref, k_ref, wuv_ref, o_ref, sa_sc, sb_sc, acc_sc, *, tq, rows):
    i = pl.program_id(1)
    cols = MLA_HEADS * tq
    nsub = tq // BLOCK
    q = q_ref[0].reshape(cols, MLA_QK)
    n_groups = ((i + 1) * tq + rows - 1) // rows
    nt_dims = (((1,), (1,)), ((), ()))
    tn_dims = (((0,), (0,)), ((), ()))

    def fold(x, op):
        return op(x.reshape(x.shape[0] // 8, 8, x.shape[1]), axis=0)

    def scores(g, buf):
        start = pl.multiple_of(g * rows, rows)
        s = lax.dot_general(k_ref[0, pl.ds(start, rows), :], q, nt_dims, preferred_element_type=F32)
        buf[...] = s
        return fold(s, jnp.max)

    def weights(g, buf, mx8, m_run):
        start = pl.multiple_of(g * rows, rows)
        m_new = jnp.maximum(m_run, jnp.max(mx8, axis=0, keepdims=True))
        p = jnp.exp2(buf[...] - m_new).astype(BF16)
        v = k_ref[0, pl.ds(start, rows), 0:MLA_ACC]
        pv = lax.dot_general(v, p, tn_dims, preferred_element_type=F32)
        acc_sc[...] = acc_sc[...] * jnp.exp2(m_run - m_new) + pv
        return m_new

    acc_sc[...] = jnp.zeros((MLA_ACC, cols), F32)
    m0 = jnp.full((1, cols), NEG, F32)

    def pair(k, carry):
        mxa, m_run = carry
        mxb = scores(2 * k + 1, sb_sc)
        m_run = weights(2 * k, sa_sc, mxa, m_run)
        mxa = scores(2 * k + 2, sa_sc)
        m_run = weights(2 * k + 1, sb_sc, mxb, m_run)
        return mxa, m_run

    n_pairs = (n_groups - 1) // 2
    mxa, m_run = lax.fori_loop(0, n_pairs, pair, (scores(0, sa_sc), m0))

    def one_left(m_run):
        return weights(n_groups - 1, sa_sc, mxa, m_run)

    def two_left(m_run):
        mxb = scores(n_groups - 1, sb_sc)
        m_run = weights(n_groups - 2, sa_sc, mxa, m_run)
        return weights(n_groups - 1, sb_sc, mxb, m_run)

    m_run = lax.cond(n_groups - 1 == 2 * n_pairs, one_left, two_left, m_run)

    lane = lax.broadcasted_iota(jnp.int32, (1, MLA_QK), 1)
    onehot = (lane >= MLA_KV_RANK + 2) & (lane < MLA_KV_RANK + 2 + MLA_BLOCKS)
    hcols = MLA_HEADS * BLOCK
    causal = lax.broadcasted_iota(jnp.int32, (BLOCK, hcols), 0) <= (
        lax.broadcasted_iota(jnp.int32, (BLOCK, hcols), 1) & (BLOCK - 1))
    o_parts = []
    for r in range(nsub):
        row0 = pl.multiple_of((i * nsub + r) * BLOCK, BLOCK)
        lanes = [slice(hd * tq + r * BLOCK, hd * tq + (r + 1) * BLOCK) for hd in range(MLA_HEADS)]
        qd = jnp.concatenate([q[ln, :] for ln in lanes], axis=0)
        qd = jnp.where(onehot, jnp.zeros((), BF16), qd)
        sd = lax.dot_general(k_ref[0, pl.ds(row0, BLOCK), :], qd, nt_dims, preferred_element_type=F32)
        sd = jnp.where(causal, sd, NEG)
        m_old = jnp.concatenate([m_run[:, ln] for ln in lanes], axis=1)
        m_new = jnp.maximum(m_old, jnp.max(sd, axis=0, keepdims=True))
        p = jnp.exp2(sd - m_new).astype(BF16)
        pv = lax.dot_general(k_ref[0, pl.ds(row0, BLOCK), 0:MLA_ACC], p, tn_dims, preferred_element_type=F32)
        acc = jnp.concatenate([acc_sc[:, ln] for ln in lanes], axis=1) * jnp.exp2(m_old - m_new) + pv
        o_parts.append((acc[0:MLA_KV_RANK, :] / acc[MLA_KV_RANK:MLA_KV_RANK + 1, :]).astype(BF16))

    y_rows = []
    for r in range(nsub):
        y = jnp.zeros((BLOCK, MLA_WIDTH), F32)
        for hd in range(MLA_HEADS):
            y += lax.dot_general(o_parts[r][:, hd * BLOCK:(hd + 1) * BLOCK], wuv_ref[hd], tn_dims,
                                 preferred_element_type=F32)
        y_rows.append(y)
    o_ref[0] = jnp.concatenate(y_rows, axis=0)


def _mla(qm, kc_padded, wuv_heads, *, tq=384, rows=MLA_KEY_ROWS):
    bsz, _, seq, _ = qm.shape
    seq_k = kc_padded.shape[1]
    cols = MLA_HEADS * tq
    return pl.pallas_call(
        functools.partial(_mla_kernel, tq=tq, rows=rows),
        grid=(bsz, seq // tq),
        in_specs=[
            pl.BlockSpec((1, MLA_HEADS, tq, MLA_QK), lambda b, i: (b, 0, i, 0)),
            pl.BlockSpec((1, seq_k, MLA_QK), lambda b, i: (b, 0, 0)),
            _const_spec((MLA_HEADS, MLA_KV_RANK, MLA_WIDTH)),
        ],
        out_specs=pl.BlockSpec((1, tq, MLA_WIDTH), lambda b, i: (b, i, 0)),
        out_shape=jax.ShapeDtypeStruct((bsz, seq, MLA_WIDTH), F32),
        scratch_shapes=[pltpu.VMEM((rows, cols), F32), pltpu.VMEM((rows, cols), F32),
                        pltpu.VMEM((MLA_ACC, cols), F32)],
        compiler_params=pltpu.CompilerParams(
            dimension_semantics=("parallel", "arbitrary"), vmem_limit_bytes=VMEM_LIMIT),
        name="mla",
    )(qm, kc_padded, wuv_heads)


def _gelu_tanh(x):
    return 0.5 * x * (1.0 + jnp.tanh(math.sqrt(2.0 / math.pi) * (x + 0.044715 * (x * x * x))))


def _mix_out_kernel(h_ref, ys_ref, yw_ref, ym_ref, wglu_ref, bglu_ref, gh_ref, wout_ref, gpm_ref, gpre_ref,
                    wup_ref, wdn_ref, gpost_ref, o_ref, *, ff_chunk):
    z = _gelu_tanh(jnp.concatenate([ys_ref[0, 0], ys_ref[0, 1]], axis=1))
    gate = jnp.dot(z.astype(BF16), wglu_ref[...], preferred_element_type=F32) + bglu_ref[...]
    y_ssm = z * (1.0 / (1.0 + jnp.exp(-gate)))
    gh = gh_ref[...]
    mixed = jnp.concatenate([
        _rms(y_ssm, gh[:, 0:SSM_WIDTH]),
        _rms(yw_ref[0], gh[:, SSM_WIDTH:SSM_WIDTH + SWA_WIDTH]),
        _rms(ym_ref[0], gh[:, SSM_WIDTH + SWA_WIDTH:]),
    ], axis=1).astype(BF16)
    mo = jnp.dot(mixed, wout_ref[...], preferred_element_type=F32)
    h1 = h_ref[0] + _rms(mo, gpm_ref[...])
    a = _rms(h1, gpre_ref[...]).astype(BF16)
    f = jnp.zeros_like(h1)
    for c in range(D_FF // ff_chunk):
        up = jnp.dot(a, wup_ref[:, c * ff_chunk:(c + 1) * ff_chunk], preferred_element_type=F32)
        hid = jnp.square(jnp.maximum(up, 0.0)).astype(BF16)
        f += jnp.dot(hid, wdn_ref[c * ff_chunk:(c + 1) * ff_chunk, :], preferred_element_type=F32)
    o_ref[0] = h1 + _rms(f, gpost_ref[...])


def _mix_out(h, ys, yw, ym, wglu, bglu, gh, wout, gpm, gpre, wup, wdn, gpost, *, skip=0, tm=384, ff_chunk=1024):
    bsz, seq, _ = h.shape
    out_rows = seq - skip
    if skip:
        el = pl.Element
        first = lambda i: pl.multiple_of(skip + i * tm, BLOCK)
        row = lambda w: pl.BlockSpec((el(1), el(tm), el(w)), lambda b, i: (b, first(i), 0))
        slabs = pl.BlockSpec((el(1), el(2), el(tm), el(128)), lambda b, i: (b, 0, first(i), 0))
    else:
        row = lambda w: pl.BlockSpec((1, tm, w), lambda b, i: (b, i, 0))
        slabs = pl.BlockSpec((1, 2, tm, 128), lambda b, i: (b, 0, i, 0))
    return pl.pallas_call(
        functools.partial(_mix_out_kernel, ff_chunk=ff_chunk),
        grid=(bsz, out_rows // tm),
        in_specs=[
            row(D_MODEL), slabs, row(SWA_WIDTH), row(MLA_WIDTH),
            _const_spec((SSM_WIDTH, SSM_WIDTH)), _const_spec((1, SSM_WIDTH)), _const_spec((1, D_MODEL)),
            _const_spec((D_MODEL, D_MODEL)), _const_spec((1, D_MODEL)), _const_spec((1, D_MODEL)),
            _const_spec((D_MODEL, D_FF)), _const_spec((D_FF, D_MODEL)), _const_spec((1, D_MODEL)),
        ],
        out_specs=pl.BlockSpec((1, tm, D_MODEL), lambda b, i: (b, i, 0)),
        out_shape=jax.ShapeDtypeStruct((bsz, out_rows, D_MODEL), F32),
        compiler_params=pltpu.CompilerParams(
            dimension_semantics=("parallel", "parallel"), vmem_limit_bytes=VMEM_LIMIT),
        name="mix_out",
    )(h, ys, yw, ym, wglu, bglu, gh, wout, gpm, gpre, wup, wdn, gpost)


def _rot_half_cols(w):
    half = w.shape[-1] // 2
    return jnp.concatenate([-w[..., half:], w[..., :half]], axis=-1)


def _layer_weights(w_in, w_uq, w_uk, w_uv):
    z = lambda n: jnp.zeros((D_MODEL, n), w_in.dtype)
    s_cq = SSM_WIDTH + SWA_WIDTH + 2 * SWA_KV_WIDTH
    s_ckv = s_cq + MLA_Q_RANK
    s_kr = s_ckv + MLA_KV_RANK
    kr = w_in[:, s_kr:s_kr + MLA_ROPE]
    w1 = jnp.concatenate([
        w_in[:, :s_cq], w_in[:, s_cq:s_ckv], z(C_CKV - C_CQ - MLA_Q_RANK),
        w_in[:, s_ckv:s_kr], kr, _rot_half_cols(kr), z(C_END - C_KR - 2 * MLA_ROPE)], axis=1).astype(BF16)
    uq = w_uq.reshape(MLA_Q_RANK, MLA_HEADS, MLA_NOPE + MLA_ROPE)
    rope = uq[:, :, MLA_NOPE:]
    wuq = jnp.concatenate([
        uq[:, :, :MLA_NOPE].reshape(MLA_Q_RANK, -1), rope.reshape(MLA_Q_RANK, -1),
        _rot_half_cols(rope).reshape(MLA_Q_RANK, -1)], axis=1)
    wuq = jnp.concatenate([wuq, jnp.zeros((256 - MLA_Q_RANK, wuq.shape[1]), wuq.dtype)], axis=0).astype(BF16)
    uk = w_uk.reshape(MLA_KV_RANK, MLA_HEADS, MLA_NOPE).transpose(1, 2, 0)
    wuk_bd = _group_block_diag(uk).astype(BF16)
    uv = w_uv.reshape(MLA_KV_RANK, MLA_HEADS, MLA_V).transpose(1, 0, 2)
    wuv_heads = _group_block_diag(uv).reshape(MLA_HEADS, MLA_KV_RANK, MLA_WIDTH).astype(BF16)
    return w1, wuq, wuk_bd, wuv_heads


def kernel(x, meta_tokens, norm_pre_mix, norm_post_mix, norm_pre_mlp, norm_post_mlp, w_in, w_out, norm_heads,
           ssm_a_re, ssm_a_im, ssm_log_step, ssm_b_re, ssm_b_im, ssm_c_re, ssm_c_im, ssm_d, ssm_w_glu,
           ssm_b_glu, swa_sinks, mla_q_norm, mla_kv_norm, mla_w_uq, mla_w_uk, mla_w_uv, w_mlp_up, w_mlp_down):
    bsz, seq_real, d = x.shape
    depth = w_in.shape[0]
    meta = jnp.broadcast_to(meta_tokens.astype(x.dtype)[None], (bsz, N_META, d))
    h = jnp.concatenate([jnp.zeros((bsz, PAD, d), x.dtype), meta, x], axis=1)
    seq = h.shape[1]

    half = MLA_ROPE // 2
    inv_freq = ROPE_THETA ** (-jnp.arange(half, dtype=F32) / half)
    ang = (jnp.arange(seq) - PAD).astype(F32)[:, None] * inv_freq[None, :]
    cos_t = jnp.tile(jnp.cos(ang), (1, 2 * MLA_HEADS))
    sin_t = jnp.tile(jnp.sin(ang), (1, 2 * MLA_HEADS))
    seq_k = -(-seq // MLA_KEY_ROWS) * MLA_KEY_ROWS
    kblk = jnp.arange(seq_k)[:, None] // BLOCK
    bcol = jnp.arange(MLA_BLOCKS)[None, :]
    one = jnp.ones((seq_k, 1), F32)
    fill = jnp.zeros((seq_k, 128 - 2 - MLA_BLOCKS), F32)
    kx_t = jnp.concatenate([one, jnp.where(jnp.arange(seq_k)[:, None] < PAD, NEG, 0.0),
                            (kblk == bcol).astype(F32), fill], axis=1).astype(BF16)
    qx_t = jnp.concatenate([0.0 * one, one, jnp.where(bcol >= kblk, NEG, 0.0), fill], axis=1).astype(BF16)[:seq]
    key_tail = jnp.zeros((bsz, seq_k - seq, MLA_QK), BF16).at[:, :, MLA_KV_RANK:MLA_KV_RANK + MLA_EXT].set(
        kx_t[seq:, :MLA_EXT])
    slopes = 2.0 ** (-8.0 * jnp.arange(1, SWA_HEADS + 1, dtype=F32) / SWA_HEADS)
    per_row = lambda v: jnp.repeat(v.reshape(SWA_KV_HEADS, SWA_GROUP), BLOCK, axis=1)[:, None, :]
    jl = jnp.arange(BLOCK)[:, None]
    tq = jnp.arange(BLOCK)[None, :]
    dist = jnp.where(jl <= tq, tq - jl, tq - jl + BLOCK).astype(F32)
    swa_bias = -per_row(slopes) * jnp.tile(dist, (1, SWA_GROUP))[None]

    r_mat, p_mat, q_mat, lam = _s5_operators(
        ssm_a_re, ssm_a_im, ssm_log_step, ssm_b_re, ssm_b_im, ssm_c_re, ssm_c_im)
    row = lambda v: v.reshape(1, -1).astype(F32)

    for l in range(depth):
        w1, wuq, wuk_bd, wuv_heads = _layer_weights(w_in[l], mla_w_uq[l], mla_w_uk[l], mla_w_uv[l])
        gq = jnp.concatenate([mla_q_norm[l], jnp.zeros((256 - MLA_Q_RANK,), F32)]).reshape(1, 256)
        u, q, kv, qm, kc = _mix_in(h, row(norm_pre_mix[l]), w1, gq, row(mla_kv_norm[l]), wuq, wuk_bd,
                                   cos_t, sin_t, qx_t, kx_t[:seq])
        d_fold = jnp.tile(ssm_d[l].astype(F32), SSM_CHUNK).reshape(1, SSM_FOLD)
        y_ssm = _s5_sequence(u, r_mat[l], p_mat[l], q_mat[l], lam[l], d_fold)
        y_swa = _swa(q, kv, swa_bias, per_row(swa_sinks[l].astype(F32)))
        kc_padded = jnp.concatenate([kc, key_tail], axis=1)
        y_mla = _mla(qm, kc_padded, wuv_heads)
        last = l == depth - 1
        h = _mix_out(
            h, y_ssm, y_swa, y_mla,
            ssm_w_glu[l].astype(BF16), row(ssm_b_glu[l]), row(norm_heads[l]), w_out[l].astype(BF16),
            row(norm_post_mix[l]), row(norm_pre_mlp[l]), w_mlp_up[l].astype(BF16), w_mlp_down[l].astype(BF16),
            row(norm_post_mlp[l]), skip=PAD + N_META if last else 0, tm=512 if last else 384)
    return h
```

```python
import functools
import math

import jax
import jax.numpy as jnp
from jax import lax
from jax.experimental import pallas as pl
from jax.experimental.pallas import tpu as pltpu

F32 = jnp.float32
BF16 = jnp.bfloat16

D_MODEL = 1024
N_META = 16
BLOCK = 128
PAD = BLOCK - N_META
SSM_GROUPS = 16
SSM_GROUP_CH = 16
SSM_WIDTH = SSM_GROUPS * SSM_GROUP_CH
SSM_STATE = 64
SWA_HEADS = 8
SWA_KV_HEADS = 2
SWA_GROUP = SWA_HEADS // SWA_KV_HEADS
HEAD_DIM = 64
SWA_WIDTH = SWA_HEADS * HEAD_DIM
SWA_KV_WIDTH = SWA_KV_HEADS * HEAD_DIM
MLA_HEADS = 4
MLA_Q_RANK = 192
MLA_KV_RANK = 128
MLA_NOPE = 64
MLA_ROPE = 32
MLA_V = 64
MLA_WIDTH = MLA_HEADS * MLA_V
MLA_BLOCKS = 40
MLA_EXT = 48
MLA_ACC = MLA_KV_RANK + 8
LOG2E = 1.4426950408889634
MLA_QK = MLA_KV_RANK + MLA_EXT + MLA_ROPE
ROPE_THETA = 10000.0
D_FF = 4 * D_MODEL
EPS = 1e-6
NEG = -1e30

SSM_CHUNK = 8
SSM_FOLD = SSM_CHUNK * SSM_WIDTH
SSM_MODES = SSM_GROUPS * SSM_STATE

C_U, C_Q, C_K, C_CQ, C_CKV, C_KR, C_END = 0, 256, 768, 1024, 1280, 1408, 1536

MLA_KEY_ROWS = 512

VMEM_LIMIT = 56 * 1024 * 1024


def _rms(x, g, n=None):
    n = x.shape[-1] if n is None else n
    ms = jnp.sum(x * x, axis=-1, keepdims=True) * (1.0 / n)
    return x * lax.rsqrt(ms + EPS) * g


def _const_spec(shape):
    nd = len(shape)
    return pl.BlockSpec(shape, lambda *_: (0,) * nd, pipeline_mode=pl.Buffered(1))


def _layer_spec(shape, layer):
    nd = len(shape)
    return pl.BlockSpec((1,) + shape, lambda *_: (layer,) + (0,) * nd, pipeline_mode=pl.Buffered(1))


def _mix_in_kernel(h_ref, g_ref, w1_ref, gq_ref, gkv_ref, wuq_ref, wuk_ref, cos_ref, sin_ref, qx_ref, kx_ref,
                   u_ref, q_ref, kv_ref, qm_ref, kc_ref, *, tm, nsub):
    j = pl.program_id(1)
    ts = tm // nsub
    scale = LOG2E * (MLA_NOPE + MLA_ROPE) ** -0.5
    ext = slice(MLA_KV_RANK, MLA_KV_RANK + MLA_EXT)
    rot = slice(MLA_KV_RANK + MLA_EXT, MLA_QK)
    for sub in range(nsub):
        rows = slice(sub * ts, (sub + 1) * ts)
        a = _rms(h_ref[0, rows, :], g_ref[...]).astype(BF16)
        proj = jnp.dot(a, w1_ref[...], preferred_element_type=F32)
        row = j * tm + sub * ts + lax.broadcasted_iota(jnp.int32, (ts, 1), 0)
        for c in range(2):
            u_ref[0, c, rows, :] = jnp.where(row >= PAD, proj[:, C_U + c * 128:C_U + (c + 1) * 128], 0.0)
        q_ref[0, rows, :] = (proj[:, C_Q:C_K] * (HEAD_DIM ** -0.5)).astype(BF16)
        kv_ref[0, rows, :] = proj[:, C_K:C_CQ].astype(BF16)

        cos = cos_ref[rows, :]
        sin = sin_ref[rows, :]
        qn = _rms(proj[:, C_CQ:C_CKV], gq_ref[...], n=MLA_Q_RANK).astype(BF16)
        q2 = jnp.dot(qn, wuq_ref[...], preferred_element_type=F32)
        qr = (q2[:, 256:384] * cos + q2[:, 384:512] * sin) * scale
        qa = jnp.dot(q2[:, 0:256].astype(BF16), wuk_ref[...], preferred_element_type=F32) * scale
        c = _rms(proj[:, C_CKV:C_KR], gkv_ref[...])
        krr = proj[:, C_KR:C_END]
        kr = krr[:, 0:MLA_ROPE] * cos[:, 0:MLA_ROPE] + krr[:, MLA_ROPE:2 * MLA_ROPE] * sin[:, 0:MLA_ROPE]
        for hd in range(MLA_HEADS):
            qm_ref[0, hd, rows, 0:MLA_KV_RANK] = qa[:, hd * MLA_KV_RANK:(hd + 1) * MLA_KV_RANK].astype(BF16)
            qm_ref[0, hd, rows, ext] = qx_ref[rows, 0:MLA_EXT]
            qm_ref[0, hd, rows, rot] = qr[:, hd * MLA_ROPE:(hd + 1) * MLA_ROPE].astype(BF16)
        kc_ref[0, rows, 0:MLA_KV_RANK] = c.astype(BF16)
        kc_ref[0, rows, ext] = kx_ref[rows, 0:MLA_EXT]
        kc_ref[0, rows, rot] = kr.astype(BF16)


def _mix_in(h, g, w1, gq, gkv, wuq, wuk, cos_t, sin_t, qx_t, kx_t, *, tm=2112, nsub=4):
    bsz, seq, _ = h.shape
    grid = (bsz, seq // tm)
    row3 = lambda b, j: (b, j, 0)
    return pl.pallas_call(
        functools.partial(_mix_in_kernel, tm=tm, nsub=nsub),
        grid=grid,
        in_specs=[
            pl.BlockSpec((1, tm, D_MODEL), row3),
            _const_spec((1, D_MODEL)),
            _const_spec((D_MODEL, C_END)),
            _const_spec((1, 256)),
            _const_spec((1, MLA_KV_RANK)),
            _const_spec((256, 512)),
            _const_spec((256, 512)),
            pl.BlockSpec((tm, 128), lambda b, j: (j, 0)),
            pl.BlockSpec((tm, 128), lambda b, j: (j, 0)),
            pl.BlockSpec((tm, 128), lambda b, j: (j, 0)),
            pl.BlockSpec((tm, 128), lambda b, j: (j, 0)),
        ],
        out_specs=[
            pl.BlockSpec((1, 2, tm, 128), lambda b, j: (b, 0, j, 0)),
            pl.BlockSpec((1, tm, SWA_WIDTH), row3),
            pl.BlockSpec((1, tm, 2 * SWA_KV_WIDTH), row3),
            pl.BlockSpec((1, MLA_HEADS, tm, MLA_QK), lambda b, j: (b, 0, j, 0)),
            pl.BlockSpec((1, tm, MLA_QK), row3),
        ],
        out_shape=[
            jax.ShapeDtypeStruct((bsz, 2, seq, 128), F32),
            jax.ShapeDtypeStruct((bsz, seq, SWA_WIDTH), BF16),
            jax.ShapeDtypeStruct((bsz, seq, 2 * SWA_KV_WIDTH), BF16),
            jax.ShapeDtypeStruct((bsz, MLA_HEADS, seq, MLA_QK), BF16),
            jax.ShapeDtypeStruct((bsz, seq, MLA_QK), BF16),
        ],
        compiler_params=pltpu.CompilerParams(
            dimension_semantics=("parallel", "parallel"), vmem_limit_bytes=VMEM_LIMIT),
        name="mix_in",
    )(h, g, w1, gq, gkv, wuq, wuk, cos_t, sin_t, qx_t, kx_t)


def _split_bf16(x):
    hi = x.astype(BF16)
    lo = (x - hi.astype(F32)).astype(BF16)
    return hi, lo


def _bdot3(a, b):
    ah, al = _split_bf16(a)
    bh, bl = _split_bf16(b)
    dn = (((2,), (2,)), ((0,), (0,)))
    f = lambda x, y: lax.dot_general(x, y, dn, preferred_element_type=F32)
    return f(ah, bh) + f(ah, bl) + f(al, bh)


def _expand_groups(a2, row_shift, col_shift, n_cols):
    rows, w = a2.shape
    tile = (lax.broadcasted_iota(jnp.int32, (w, n_cols), 1) & (w - 1)) == lax.broadcasted_iota(
        jnp.int32, (w, n_cols), 0)
    wide = jnp.dot(a2.astype(BF16), jnp.where(tile, 1.0, 0.0).astype(BF16), preferred_element_type=F32)
    same = (lax.broadcasted_iota(jnp.int32, (rows, n_cols), 0) >> row_shift) == (
        lax.broadcasted_iota(jnp.int32, (rows, n_cols), 1) >> col_shift)
    return jnp.where(same, wide, 0.0)


def _s5_operator_kernel(ar_ref, ai_ref, ls_ref, btr_ref, bti_ref, cr_ref, ci_ref,
                        r_ref, p_ref, q_ref, l8r_ref, l8i_ref, pwr_sc, pwi_sc, kk_sc):
    s = pl.program_id(1)
    ar, ai = ar_ref[0], ai_ref[0]
    step = jnp.exp(ls_ref[0])
    lr = jnp.exp(ar * step) * jnp.cos(ai * step)
    li = jnp.exp(ar * step) * jnp.sin(ai * step)

    den = ar * ar + ai * ai
    nr, ni = lr - 1.0, li
    coef_re = (nr * ar + ni * ai) / den
    coef_im = (ni * ar - nr * ai) / den
    btr, bti = btr_ref[0], bti_ref[0]
    bb_re = coef_re * btr - coef_im * bti
    bb_im = coef_re * bti + coef_im * btr
    cr, ci = cr_ref[0], ci_ref[0]

    def c_lam(k):
        qr_, qi_ = pwr_sc[k], pwi_sc[k]
        return cr * qr_ - ci * qi_, cr * qi_ + ci * qr_

    gc = SSM_GROUPS * SSM_GROUP_CH

    @pl.when(s == 0)
    def _():
        pr, pi = jnp.ones_like(lr), jnp.zeros_like(li)
        for k in range(SSM_CHUNK + 1):
            pwr_sc[k] = pr
            pwi_sc[k] = pi
            pr, pi = pr * lr - pi * li, pr * li + pi * lr
        for k in range(SSM_CHUNK):
            clr, cli = c_lam(k)
            kk_sc[k] = (_bdot3(bb_re, clr) - _bdot3(bb_im, cli)).reshape(gc, SSM_GROUP_CH)

    for t in range(SSM_CHUNK):
        lag = t - s
        blk = _expand_groups(kk_sc[jnp.maximum(lag, 0)], 4, 4, gc)
        r_ref[0, :, t * gc:(t + 1) * gc] = jnp.where(lag >= 0, blk, 0.0).astype(BF16)

    pw_r, pw_i = pwr_sc[SSM_CHUNK - 1 - s], pwi_sc[SSM_CHUNK - 1 - s]
    p_re = (pw_r * bb_re - pw_i * bb_im).reshape(gc, SSM_STATE)
    p_im = (pw_r * bb_im + pw_i * bb_re).reshape(gc, SSM_STATE)
    p_ref[0, :, 0:SSM_MODES] = _expand_groups(p_re, 4, 6, SSM_MODES).astype(BF16)
    p_ref[0, :, SSM_MODES:2 * SSM_MODES] = _expand_groups(p_im, 4, 6, SSM_MODES).astype(BF16)

    clr, cli = c_lam(s + 1)
    q_ref[0, 0:SSM_MODES, :] = _expand_groups(clr.reshape(gc, SSM_STATE), 4, 6, SSM_MODES).T.astype(BF16)
    q_ref[0, SSM_MODES:2 * SSM_MODES, :] = (
        -_expand_groups(cli.reshape(gc, SSM_STATE), 4, 6, SSM_MODES)).T.astype(BF16)

    l8r_ref[0] = pwr_sc[SSM_CHUNK]
    l8i_ref[0] = pwi_sc[SSM_CHUNK]


def _s5_operators(a_re, a_im, log_step, b_re, b_im, c_re, c_im):
    depth = a_re.shape[0]
    G, P, C, T = SSM_GROUPS, SSM_STATE, SSM_GROUP_CH, SSM_CHUNK
    lay = lambda *shape: pl.BlockSpec((1,) + shape, lambda l, s: (l,) + (0,) * len(shape))
    ls = jnp.broadcast_to(log_step[:, :, None, None], (depth, G, 1, P))
    args = (a_re[:, :, None, :], a_im[:, :, None, :], ls,
            b_re.transpose(0, 1, 3, 2), b_im.transpose(0, 1, 3, 2), c_re, c_im)
    r_mat, p_mat, q_mat, l8r, l8i = pl.pallas_call(
        _s5_operator_kernel,
        grid=(depth, T),
        in_specs=[lay(G, 1, P)] * 3 + [lay(G, C, P)] * 4,
        out_specs=[
            pl.BlockSpec((1, SSM_WIDTH, SSM_FOLD), lambda l, s: (l, s, 0)),
            pl.BlockSpec((1, SSM_WIDTH, 2 * SSM_MODES), lambda l, s: (l, s, 0)),
            pl.BlockSpec((1, 2 * SSM_MODES, SSM_WIDTH), lambda l, s: (l, 0, s)),
            lay(G, 1, P), lay(G, 1, P)],
        out_shape=[
            jax.ShapeDtypeStruct((depth, SSM_FOLD, SSM_FOLD), BF16),
            jax.ShapeDtypeStruct((depth, SSM_FOLD, 2 * SSM_MODES), BF16),
            jax.ShapeDtypeStruct((depth, 2 * SSM_MODES, SSM_FOLD), BF16),
            jax.ShapeDtypeStruct((depth, G, 1, P), F32),
            jax.ShapeDtypeStruct((depth, G, 1, P), F32),
        ],
        scratch_shapes=[pltpu.VMEM((T + 1, G, 1, P), F32), pltpu.VMEM((T + 1, G, 1, P), F32),
                        pltpu.VMEM((T, G * C, C), F32)],
        compiler_params=pltpu.CompilerParams(dimension_semantics=("parallel", "arbitrary")),
        name="s5_operators",
    )(*args)
    lam = jnp.concatenate([l8r.reshape(depth, 1, SSM_MODES), l8i.reshape(depth, 1, SSM_MODES)], axis=2)
    return r_mat, p_mat, q_mat, lam


def _group_block_diag(x):
    *lead, G, r, c = x.shape
    eye = jnp.eye(G, dtype=x.dtype)
    y = x[..., :, :, None, :] * eye[:, None, :, None]
    return y.reshape(*lead, G * r, G * c)


def _fold_tokens(u_ref, n_chunks):
    return jnp.concatenate(
        [u_ref[0, c, pl.ds(s, n_chunks, stride=SSM_CHUNK), :] for s in range(SSM_CHUNK) for c in range(2)], axis=1)


def _s5_state_kernel(u_ref, p_ref, lam_ref, x_ref, s_sc, x_sc, *, n_chunks):
    u = _fold_tokens(u_ref, n_chunks).astype(BF16)
    s_sc[...] = jnp.dot(u, p_ref[0], preferred_element_type=F32)
    lr = lam_ref[0, :, 0:SSM_MODES]
    li = lam_ref[0, :, SSM_MODES:2 * SSM_MODES]

    def body(n, carry):
        xr, xi = carry
        x_sc[pl.ds(n, 1), 0:SSM_MODES] = xr
        x_sc[pl.ds(n, 1), SSM_MODES:2 * SSM_MODES] = xi
        sr = s_sc[pl.ds(n, 1), 0:SSM_MODES]
        si = s_sc[pl.ds(n, 1), SSM_MODES:2 * SSM_MODES]
        return lr * xr - li * xi + sr, lr * xi + li * xr + si

    z = jnp.zeros((1, SSM_MODES), F32)
    lax.fori_loop(0, n_chunks, body, (z, z))
    x_ref[0] = x_sc[...].astype(BF16)


def _s5_out_kernel(u_ref, x_ref, r_ref, q_ref, d_ref, y_ref, *, n_chunks):
    u = _fold_tokens(u_ref, n_chunks)
    ub = u.astype(BF16)
    x = x_ref[0]
    for t in range(SSM_CHUNK):
        cols = slice(t * SSM_WIDTH, (t + 1) * SSM_WIDTH)
        rows = (t + 1) * SSM_WIDTH
        y = jnp.dot(ub[:, 0:rows], r_ref[0, 0:rows, cols], preferred_element_type=F32)
        y += jnp.dot(x, q_ref[0, :, cols], preferred_element_type=F32)
        y += d_ref[:, cols] * u[:, cols]
        for c in range(2):
            y_ref[0, c, pl.ds(t, n_chunks, stride=SSM_CHUNK), :] = y[:, c * 128:(c + 1) * 128]


def _s5_sequence(u, r_mat, p_mat, q_mat, lam, d_fold, layer):
    bsz, _, seq, _ = u.shape
    n_chunks = seq // SSM_CHUNK
    params = pltpu.CompilerParams(dimension_semantics=("parallel",), vmem_limit_bytes=VMEM_LIMIT)
    slabs = pl.BlockSpec((1, 2, seq, 128), lambda b: (b, 0, 0, 0))
    states = pl.BlockSpec((1, n_chunks, 2 * SSM_MODES), lambda b: (b, 0, 0))
    x_prev = pl.pallas_call(
        functools.partial(_s5_state_kernel, n_chunks=n_chunks), grid=(bsz,),
        in_specs=[slabs, _layer_spec((SSM_FOLD, 2 * SSM_MODES), layer), _layer_spec((1, 2 * SSM_MODES), layer)],
        out_specs=states,
        out_shape=jax.ShapeDtypeStruct((bsz, n_chunks, 2 * SSM_MODES), BF16),
        scratch_shapes=[pltpu.VMEM((n_chunks, 2 * SSM_MODES), F32), pltpu.VMEM((n_chunks, 2 * SSM_MODES), F32)],
        compiler_params=params, name="s5_state",
    )(u, p_mat, lam)
    return pl.pallas_call(
        functools.partial(_s5_out_kernel, n_chunks=n_chunks), grid=(bsz,),
        in_specs=[slabs, states, _layer_spec((SSM_FOLD, SSM_FOLD), layer),
                  _layer_spec((2 * SSM_MODES, SSM_FOLD), layer), _const_spec((1, SSM_FOLD))],
        out_specs=slabs,
        out_shape=jax.ShapeDtypeStruct((bsz, 2, seq, 128), F32),
        compiler_params=params, name="s5_out",
    )(u, x_prev, r_mat, q_mat, d_fold)


def _swa_kernel(q_ref, kvp_ref, kvc_ref, bias_ref, sink_ref, o_ref, *, nsub):
    i = pl.program_id(1)
    cols = SWA_GROUP * BLOCK
    jl = lax.broadcasted_iota(jnp.int32, (BLOCK, cols), 0)
    t = lax.broadcasted_iota(jnp.int32, (BLOCK, cols), 1) & (BLOCK - 1)
    cur = jl <= t
    for r in range(nsub):
        blk = i * nsub + r
        rows = slice(r * BLOCK, (r + 1) * BLOCK)
        q = q_ref[0, rows, :]
        kvc = kvc_ref[0, rows, :]
        kvp = kvp_ref[0] if r == 0 else kvc_ref[0, (r - 1) * BLOCK:r * BLOCK, :]
        valid = jnp.where(cur, blk * BLOCK + jl, (blk - 1) * BLOCK + jl) >= PAD
        outs = []
        for hk in range(SWA_KV_HEADS):
            ks = slice(hk * HEAD_DIM, (hk + 1) * HEAD_DIM)
            vs = slice(SWA_KV_WIDTH + hk * HEAD_DIM, SWA_KV_WIDTH + (hk + 1) * HEAD_DIM)
            kcat = jnp.concatenate([kvp[:, ks], kvc[:, ks]], axis=0)
            vcat = jnp.concatenate([kvp[:, vs], kvc[:, vs]], axis=0)
            qs = jnp.concatenate(
                [q[:, (hk * SWA_GROUP + g) * HEAD_DIM:(hk * SWA_GROUP + g + 1) * HEAD_DIM]
                 for g in range(SWA_GROUP)], axis=0)
            sf = lax.dot_general(kcat, qs, (((1,), (1,)), ((), ())), preferred_element_type=F32)
            s = jnp.where(cur, sf[BLOCK:2 * BLOCK], sf[0:BLOCK]) + bias_ref[hk]
            s = jnp.where(valid, s, NEG)
            sink = sink_ref[hk]
            m = jnp.maximum(jnp.max(s, axis=0, keepdims=True), sink)
            p = jnp.exp(s - m)
            den = jnp.sum(p, axis=0, keepdims=True) + jnp.exp(sink - m)
            pf = jnp.concatenate([jnp.where(cur, 0.0, p), jnp.where(cur, p, 0.0)], axis=0).astype(BF16)
            o_t = lax.dot_general(vcat, pf, (((0,), (0,)), ((), ())), preferred_element_type=F32) / den
            outs += [o_t[:, g * BLOCK:(g + 1) * BLOCK] for g in range(SWA_GROUP)]
        o_ref[0, rows, :] = jnp.concatenate(outs, axis=0).T


def _swa(q, kv, bias, sink_rows, *, nsub=3):
    bsz, seq, _ = q.shape
    tm = nsub * BLOCK
    cols = SWA_GROUP * BLOCK
    return pl.pallas_call(
        functools.partial(_swa_kernel, nsub=nsub),
        grid=(bsz, seq // tm),
        in_specs=[
            pl.BlockSpec((1, tm, SWA_WIDTH), lambda b, i: (b, i, 0)),
            pl.BlockSpec((1, BLOCK, 2 * SWA_KV_WIDTH), lambda b, i: (b, jnp.maximum(nsub * i - 1, 0), 0)),
            pl.BlockSpec((1, tm, 2 * SWA_KV_WIDTH), lambda b, i: (b, i, 0)),
            _const_spec((SWA_KV_HEADS, BLOCK, cols)),
            _const_spec((SWA_KV_HEADS, 1, cols)),
        ],
        out_specs=pl.BlockSpec((1, tm, SWA_WIDTH), lambda b, i: (b, i, 0)),
        out_shape=jax.ShapeDtypeStruct((bsz, seq, SWA_WIDTH), F32),
        compiler_params=pltpu.CompilerParams(dimension_semantics=("parallel", "parallel")),
        name="swa",
    )(q, kv, kv, bias, sink_rows)


def _mla_kernel(q_ref, k_ref, wuv_ref, o_ref, sa_sc, sb_sc, acc_sc, *, tq, rows):
    i = pl.program_id(1)
    cols = MLA_HEADS * tq
    nsub = tq // BLOCK
    q = q_ref[0].reshape(cols, MLA_QK)
    n_groups = ((i + 1) * tq + rows - 1) // rows
    nt_dims = (((1,), (1,)), ((), ()))
    tn_dims = (((0,), (0,)), ((), ()))

    def fold(x, op):
        return op(x.reshape(x.shape[0] // 8, 8, x.shape[1]), axis=0)

    def scores(g, buf):
        start = pl.multiple_of(g * rows, rows)
        s = lax.dot_general(k_ref[0, pl.ds(start, rows), :], q, nt_dims, preferred_element_type=F32)
        buf[...] = s
        return fold(s, jnp.max)

    def weights(g, buf, mx8, m_run):
        start = pl.multiple_of(g * rows, rows)
        m_new = jnp.maximum(m_run, jnp.max(mx8, axis=0, keepdims=True))
        p = jnp.exp2(buf[...] - m_new).astype(BF16)
        v = k_ref[0, pl.ds(start, rows), 0:MLA_ACC]
        pv = lax.dot_general(v, p, tn_dims, preferred_element_type=F32)
        acc_sc[...] = acc_sc[...] * jnp.exp2(m_run - m_new) + pv
        return m_new

    acc_sc[...] = jnp.zeros((MLA_ACC, cols), F32)
    m0 = jnp.full((1, cols), NEG, F32)

    def pair(k, carry):
        mxa, m_run = carry
        mxb = scores(2 * k + 1, sb_sc)
        m_run = weights(2 * k, sa_sc, mxa, m_run)
        mxa = scores(2 * k + 2, sa_sc)
        m_run = weights(2 * k + 1, sb_sc, mxb, m_run)
        return mxa, m_run

    n_pairs = (n_groups - 1) // 2
    mxa, m_run = lax.fori_loop(0, n_pairs, pair, (scores(0, sa_sc), m0))

    def one_left(m_run):
        return weights(n_groups - 1, sa_sc, mxa, m_run)

    def two_left(m_run):
        mxb = scores(n_groups - 1, sb_sc)
        m_run = weights(n_groups - 2, sa_sc, mxa, m_run)
        return weights(n_groups - 1, sb_sc, mxb, m_run)

    m_run = lax.cond(n_groups - 1 == 2 * n_pairs, one_left, two_left, m_run)

    lane = lax.broadcasted_iota(jnp.int32, (1, MLA_QK), 1)
    onehot = (lane >= MLA_KV_RANK + 2) & (lane < MLA_KV_RANK + 2 + MLA_BLOCKS)
    hcols = MLA_HEADS * BLOCK
    causal = lax.broadcasted_iota(jnp.int32, (BLOCK, hcols), 0) <= (
        lax.broadcasted_iota(jnp.int32, (BLOCK, hcols), 1) & (BLOCK - 1))
    o_parts = []
    for r in range(nsub):
        row0 = pl.multiple_of((i * nsub + r) * BLOCK, BLOCK)
        lanes = [slice(hd * tq + r * BLOCK, hd * tq + (r + 1) * BLOCK) for hd in range(MLA_HEADS)]
        qd = jnp.concatenate([q[ln, :] for ln in lanes], axis=0)
        qd = jnp.where(onehot, jnp.zeros((), BF16), qd)
        sd = lax.dot_general(k_ref[0, pl.ds(row0, BLOCK), :], qd, nt_dims, preferred_element_type=F32)
        sd = jnp.where(causal, sd, NEG)
        m_old = jnp.concatenate([m_run[:, ln] for ln in lanes], axis=1)
        m_new = jnp.maximum(m_old, jnp.max(sd, axis=0, keepdims=True))
        p = jnp.exp2(sd - m_new).astype(BF16)
        pv = lax.dot_general(k_ref[0, pl.ds(row0, BLOCK), 0:MLA_ACC], p, tn_dims, preferred_element_type=F32)
        acc = jnp.concatenate([acc_sc[:, ln] for ln in lanes], axis=1) * jnp.exp2(m_old - m_new) + pv
        o_parts.append((acc[0:MLA_KV_RANK, :] / acc[MLA_KV_RANK:MLA_KV_RANK + 1, :]).astype(BF16))

    y_rows = []
    for r in range(nsub):
        y = jnp.zeros((BLOCK, MLA_WIDTH), F32)
        for hd in range(MLA_HEADS):
            y += lax.dot_general(o_parts[r][:, hd * BLOCK:(hd + 1) * BLOCK], wuv_ref[hd], tn_dims,
                                 preferred_element_type=F32)
        y_rows.append(y)
    o_ref[0] = jnp.concatenate(y_rows, axis=0)


def _mla(qm, kc_padded, wuv_heads, *, tq=384, rows=MLA_KEY_ROWS):
    bsz, _, seq, _ = qm.shape
    seq_k = kc_padded.shape[1]
    cols = MLA_HEADS * tq
    return pl.pallas_call(
        functools.partial(_mla_kernel, tq=tq, rows=rows),
        grid=(bsz, seq // tq),
        in_specs=[
            pl.BlockSpec((1, MLA_HEADS, tq, MLA_QK), lambda b, i: (b, 0, i, 0)),
            pl.BlockSpec((1, seq_k, MLA_QK), lambda b, i: (b, 0, 0)),
            _const_spec((MLA_HEADS, MLA_KV_RANK, MLA_WIDTH)),
        ],
        out_specs=pl.BlockSpec((1, tq, MLA_WIDTH), lambda b, i: (b, i, 0)),
        out_shape=jax.ShapeDtypeStruct((bsz, seq, MLA_WIDTH), F32),
        scratch_shapes=[pltpu.VMEM((rows, cols), F32), pltpu.VMEM((rows, cols), F32),
                        pltpu.VMEM((MLA_ACC, cols), F32)],
        compiler_params=pltpu.CompilerParams(
            dimension_semantics=("parallel", "arbitrary"), vmem_limit_bytes=VMEM_LIMIT),
        name="mla",
    )(qm, kc_padded, wuv_heads)


def _gelu_tanh(x):
    return 0.5 * x * (1.0 + jnp.tanh(math.sqrt(2.0 / math.pi) * (x + 0.044715 * (x * x * x))))


def _mix_out_kernel(h_ref, ys_ref, yw_ref, ym_ref, wglu_ref, bglu_ref, gh_ref, wout_ref, gpm_ref, gpre_ref,
                    wup_ref, wdn_ref, gpost_ref, o_ref, *, ff_chunk):
    z = _gelu_tanh(jnp.concatenate([ys_ref[0, 0], ys_ref[0, 1]], axis=1))
    gate = jnp.dot(z.astype(BF16), wglu_ref[0], preferred_element_type=F32) + bglu_ref[...]
    y_ssm = z * (1.0 / (1.0 + jnp.exp(-gate)))
    gh = gh_ref[...]
    mixed = jnp.concatenate([
        _rms(y_ssm, gh[:, 0:SSM_WIDTH]),
        _rms(yw_ref[0], gh[:, SSM_WIDTH:SSM_WIDTH + SWA_WIDTH]),
        _rms(ym_ref[0], gh[:, SSM_WIDTH + SWA_WIDTH:]),
    ], axis=1).astype(BF16)
    mo = jnp.dot(mixed, wout_ref[0], preferred_element_type=F32)
    h1 = h_ref[0] + _rms(mo, gpm_ref[...])
    a = _rms(h1, gpre_ref[...]).astype(BF16)
    f = jnp.zeros_like(h1)
    for c in range(D_FF // ff_chunk):
        up = jnp.dot(a, wup_ref[0, :, c * ff_chunk:(c + 1) * ff_chunk], preferred_element_type=F32)
        hid = jnp.square(jnp.maximum(up, 0.0)).astype(BF16)
        f += jnp.dot(hid, wdn_ref[0, c * ff_chunk:(c + 1) * ff_chunk, :], preferred_element_type=F32)
    o_ref[0] = h1 + _rms(f, gpost_ref[...])


def _mix_out(h, ys, yw, ym, wglu, bglu, gh, wout, gpm, gpre, wup, wdn, gpost, layer, *, skip=0, tm=1056,
             ff_chunk=1024):
    bsz, seq, _ = h.shape
    out_rows = seq - skip
    if skip:
        el = pl.Element
        first = lambda i: pl.multiple_of(skip + i * tm, BLOCK)
        row = lambda w: pl.BlockSpec((el(1), el(tm), el(w)), lambda b, i: (b, first(i), 0))
        slabs = pl.BlockSpec((el(1), el(2), el(tm), el(128)), lambda b, i: (b, 0, first(i), 0))
    else:
        row = lambda w: pl.BlockSpec((1, tm, w), lambda b, i: (b, i, 0))
        slabs = pl.BlockSpec((1, 2, tm, 128), lambda b, i: (b, 0, i, 0))
    return pl.pallas_call(
        functools.partial(_mix_out_kernel, ff_chunk=ff_chunk),
        grid=(bsz, out_rows // tm),
        in_specs=[
            row(D_MODEL), slabs, row(SWA_WIDTH), row(MLA_WIDTH),
            _layer_spec((SSM_WIDTH, SSM_WIDTH), layer), _const_spec((1, SSM_WIDTH)), _const_spec((1, D_MODEL)),
            _layer_spec((D_MODEL, D_MODEL), layer), _const_spec((1, D_MODEL)), _const_spec((1, D_MODEL)),
            _layer_spec((D_MODEL, D_FF), layer), _layer_spec((D_FF, D_MODEL), layer), _const_spec((1, D_MODEL)),
        ],
        out_specs=pl.BlockSpec((1, tm, D_MODEL), lambda b, i: (b, i, 0)),
        out_shape=jax.ShapeDtypeStruct((bsz, out_rows, D_MODEL), F32),
        compiler_params=pltpu.CompilerParams(
            dimension_semantics=("parallel", "parallel"), vmem_limit_bytes=VMEM_LIMIT),
        name="mix_out",
    )(h, ys, yw, ym, wglu, bglu, gh, wout, gpm, gpre, wup, wdn, gpost)


def _rot_half_cols(w):
    half = w.shape[-1] // 2
    return jnp.concatenate([-w[..., half:], w[..., :half]], axis=-1)


def _layer_weights(w_in, w_uq, w_uk, w_uv):
    z = lambda n: jnp.zeros((D_MODEL, n), w_in.dtype)
    s_cq = SSM_WIDTH + SWA_WIDTH + 2 * SWA_KV_WIDTH
    s_ckv = s_cq + MLA_Q_RANK
    s_kr = s_ckv + MLA_KV_RANK
    kr = w_in[:, s_kr:s_kr + MLA_ROPE]
    w1 = jnp.concatenate([
        w_in[:, :s_cq], w_in[:, s_cq:s_ckv], z(C_CKV - C_CQ - MLA_Q_RANK),
        w_in[:, s_ckv:s_kr], kr, _rot_half_cols(kr), z(C_END - C_KR - 2 * MLA_ROPE)], axis=1).astype(BF16)
    uq = w_uq.reshape(MLA_Q_RANK, MLA_HEADS, MLA_NOPE + MLA_ROPE)
    rope = uq[:, :, MLA_NOPE:]
    wuq = jnp.concatenate([
        uq[:, :, :MLA_NOPE].reshape(MLA_Q_RANK, -1), rope.reshape(MLA_Q_RANK, -1),
        _rot_half_cols(rope).reshape(MLA_Q_RANK, -1)], axis=1)
    wuq = jnp.concatenate([wuq, jnp.zeros((256 - MLA_Q_RANK, wuq.shape[1]), wuq.dtype)], axis=0).astype(BF16)
    uk = w_uk.reshape(MLA_KV_RANK, MLA_HEADS, MLA_NOPE).transpose(1, 2, 0)
    wuk_bd = _group_block_diag(uk).astype(BF16)
    uv = w_uv.reshape(MLA_KV_RANK, MLA_HEADS, MLA_V).transpose(1, 0, 2)
    wuv_heads = _group_block_diag(uv).reshape(MLA_HEADS, MLA_KV_RANK, MLA_WIDTH).astype(BF16)
    return w1, wuq, wuk_bd, wuv_heads


def kernel(x, meta_tokens, norm_pre_mix, norm_post_mix, norm_pre_mlp, norm_post_mlp, w_in, w_out, norm_heads,
           ssm_a_re, ssm_a_im, ssm_log_step, ssm_b_re, ssm_b_im, ssm_c_re, ssm_c_im, ssm_d, ssm_w_glu,
           ssm_b_glu, swa_sinks, mla_q_norm, mla_kv_norm, mla_w_uq, mla_w_uk, mla_w_uv, w_mlp_up, w_mlp_down):
    bsz, seq_real, d = x.shape
    depth = w_in.shape[0]
    meta = jnp.broadcast_to(meta_tokens.astype(x.dtype)[None], (bsz, N_META, d))
    h = jnp.concatenate([jnp.zeros((bsz, PAD, d), x.dtype), meta, x], axis=1)
    seq = h.shape[1]

    half = MLA_ROPE // 2
    inv_freq = ROPE_THETA ** (-jnp.arange(half, dtype=F32) / half)
    ang = (jnp.arange(seq) - PAD).astype(F32)[:, None] * inv_freq[None, :]
    cos_t = jnp.tile(jnp.cos(ang), (1, 2 * MLA_HEADS))
    sin_t = jnp.tile(jnp.sin(ang), (1, 2 * MLA_HEADS))
    seq_k = -(-seq // MLA_KEY_ROWS) * MLA_KEY_ROWS
    kblk = jnp.arange(seq_k)[:, None] // BLOCK
    bcol = jnp.arange(MLA_BLOCKS)[None, :]
    one = jnp.ones((seq_k, 1), F32)
    fill = jnp.zeros((seq_k, 128 - 2 - MLA_BLOCKS), F32)
    kx_t = jnp.concatenate([one, jnp.where(jnp.arange(seq_k)[:, None] < PAD, NEG, 0.0),
                            (kblk == bcol).astype(F32), fill], axis=1).astype(BF16)
    qx_t = jnp.concatenate([0.0 * one, one, jnp.where(bcol >= kblk, NEG, 0.0), fill], axis=1).astype(BF16)[:seq]
    key_tail = jnp.zeros((bsz, seq_k - seq, MLA_QK), BF16).at[:, :, MLA_KV_RANK:MLA_KV_RANK + MLA_EXT].set(
        kx_t[seq:, :MLA_EXT])
    slopes = 2.0 ** (-8.0 * jnp.arange(1, SWA_HEADS + 1, dtype=F32) / SWA_HEADS)
    per_row = lambda v: jnp.repeat(v.reshape(SWA_KV_HEADS, SWA_GROUP), BLOCK, axis=1)[:, None, :]
    jl = jnp.arange(BLOCK)[:, None]
    tq = jnp.arange(BLOCK)[None, :]
    dist = jnp.where(jl <= tq, tq - jl, tq - jl + BLOCK).astype(F32)
    swa_bias = -per_row(slopes) * jnp.tile(dist, (1, SWA_GROUP))[None]

    r_mat, p_mat, q_mat, lam = _s5_operators(
        ssm_a_re, ssm_a_im, ssm_log_step, ssm_b_re, ssm_b_im, ssm_c_re, ssm_c_im)
    row = lambda v: v.reshape(1, -1).astype(F32)
    wglu, wout, wup, wdn = (w.astype(BF16) for w in (ssm_w_glu, w_out, w_mlp_up, w_mlp_down))

    for l in range(depth):
        w1, wuq, wuk_bd, wuv_heads = _layer_weights(w_in[l], mla_w_uq[l], mla_w_uk[l], mla_w_uv[l])
        gq = jnp.concatenate([mla_q_norm[l], jnp.zeros((256 - MLA_Q_RANK,), F32)]).reshape(1, 256)
        u, q, kv, qm, kc = _mix_in(h, row(norm_pre_mix[l]), w1, gq, row(mla_kv_norm[l]), wuq, wuk_bd,
                                   cos_t, sin_t, qx_t, kx_t[:seq])
        d_fold = jnp.tile(ssm_d[l].astype(F32), SSM_CHUNK).reshape(1, SSM_FOLD)
        y_ssm = _s5_sequence(u, r_mat, p_mat, q_mat, lam, d_fold, l)
        y_swa = _swa(q, kv, swa_bias, per_row(swa_sinks[l].astype(F32)))
        kc_padded = jnp.concatenate([kc, key_tail], axis=1)
        y_mla = _mla(qm, kc_padded, wuv_heads)
        last = l == depth - 1
        h = _mix_out(
            h, y_ssm, y_swa, y_mla,
            wglu, row(ssm_b_glu[l]), row(norm_heads[l]), wout, row(norm_post_mix[l]), row(norm_pre_mlp[l]),
            wup, wdn, row(norm_post_mlp[l]), l, skip=PAD + N_META if last else 0, tm=1024 if last else 1056)
    return h
```

```python
import functools
import math

import jax
import jax.numpy as jnp
from jax import lax
from jax.experimental import pallas as pl
from jax.experimental.pallas import tpu as pltpu

F32 = jnp.float32
BF16 = jnp.bfloat16

D_MODEL = 1024
N_META = 16
BLOCK = 128
PAD = BLOCK - N_META
SSM_GROUPS = 16
SSM_GROUP_CH = 16
SSM_WIDTH = SSM_GROUPS * SSM_GROUP_CH
SSM_STATE = 64
SWA_HEADS = 8
SWA_KV_HEADS = 2
SWA_GROUP = SWA_HEADS // SWA_KV_HEADS
HEAD_DIM = 64
SWA_WIDTH = SWA_HEADS * HEAD_DIM
SWA_KV_WIDTH = SWA_KV_HEADS * HEAD_DIM
SWA_KV_COLS = SWA_KV_WIDTH + SWA_KV_HEADS * BLOCK
MLA_HEADS = 4
MLA_Q_RANK = 192
MLA_KV_RANK = 128
MLA_NOPE = 64
MLA_ROPE = 32
MLA_V = 64
MLA_WIDTH = MLA_HEADS * MLA_V
MLA_BLOCKS = 40
MLA_EXT = 48
MLA_ACC = MLA_KV_RANK + 8
LOG2E = 1.4426950408889634
MLA_QK = MLA_KV_RANK + MLA_EXT + MLA_ROPE
ROPE_THETA = 10000.0
D_FF = 4 * D_MODEL
EPS = 1e-6
NEG = -1e30

SSM_CHUNK = 8
SSM_FOLD = SSM_CHUNK * SSM_WIDTH
SSM_MODES = SSM_GROUPS * SSM_STATE

C_U, C_Q, C_K, C_CQ, C_CKV, C_KR, C_END = 0, 256, 768, 1024, 1280, 1408, 1536

MLA_KEY_ROWS = 512
MLA_PART = 256

VMEM_LIMIT = 56 * 1024 * 1024


def _rms(x, g, n=None):
    n = x.shape[-1] if n is None else n
    ms = jnp.sum(x * x, axis=-1, keepdims=True) * (1.0 / n)
    return x * lax.rsqrt(ms + EPS) * g


def _const_spec(shape):
    nd = len(shape)
    return pl.BlockSpec(shape, lambda *_: (0,) * nd, pipeline_mode=pl.Buffered(1))


def _layer_spec(shape, layer):
    nd = len(shape)
    return pl.BlockSpec((1,) + shape, lambda *_: (layer,) + (0,) * nd, pipeline_mode=pl.Buffered(1))


def _mix_in_kernel(h_ref, g_ref, w1_ref, gq_ref, gkv_ref, wuq_ref, wuk_ref, cos_ref, sin_ref, qx_ref, kx_ref,
                   u_ref, q_ref, kv_ref, qm_ref, kc_ref, *, tm, nsub):
    j = pl.program_id(1)
    ts = tm // nsub
    scale = LOG2E * (MLA_NOPE + MLA_ROPE) ** -0.5
    ext = slice(MLA_KV_RANK, MLA_KV_RANK + MLA_EXT)
    rot = slice(MLA_KV_RANK + MLA_EXT, MLA_QK)
    for sub in range(nsub):
        rows = slice(sub * ts, (sub + 1) * ts)
        a = _rms(h_ref[0, rows, :], g_ref[...]).astype(BF16)
        proj = jnp.dot(a, w1_ref[...], preferred_element_type=F32)
        row = j * tm + sub * ts + lax.broadcasted_iota(jnp.int32, (ts, 1), 0)
        for c in range(2):
            u_ref[0, c, rows, :] = jnp.where(row >= PAD, proj[:, C_U + c * 128:C_U + (c + 1) * 128], 0.0)
        q_ref[0, rows, :] = (proj[:, C_Q:C_K] * (LOG2E * HEAD_DIM ** -0.5)).astype(BF16)
        kv_ref[0, rows, 0:SWA_KV_WIDTH] = proj[:, C_K:C_K + SWA_KV_WIDTH].astype(BF16)
        ones_col = jnp.where(lax.broadcasted_iota(jnp.int32, (ts, HEAD_DIM), 1) == 0, 1.0, 0.0).astype(BF16)
        for hk in range(SWA_KV_HEADS):
            lo = SWA_KV_WIDTH + hk * BLOCK
            v_lo = C_K + SWA_KV_WIDTH + hk * HEAD_DIM
            kv_ref[0, rows, lo:lo + HEAD_DIM] = proj[:, v_lo:v_lo + HEAD_DIM].astype(BF16)
            kv_ref[0, rows, lo + HEAD_DIM:lo + BLOCK] = ones_col

        cos = cos_ref[rows, :]
        sin = sin_ref[rows, :]
        qn = _rms(proj[:, C_CQ:C_CKV], gq_ref[...], n=MLA_Q_RANK).astype(BF16)
        q2 = jnp.dot(qn, wuq_ref[...], preferred_element_type=F32)
        qr = (q2[:, 256:384] * cos + q2[:, 384:512] * sin) * scale
        qa = jnp.dot(q2[:, 0:256].astype(BF16), wuk_ref[...], preferred_element_type=F32) * scale
        c = _rms(proj[:, C_CKV:C_KR], gkv_ref[...])
        krr = proj[:, C_KR:C_END]
        kr = krr[:, 0:MLA_ROPE] * cos[:, 0:MLA_ROPE] + krr[:, MLA_ROPE:2 * MLA_ROPE] * sin[:, 0:MLA_ROPE]
        for hd in range(MLA_HEADS):
            qm_ref[0, hd, rows, 0:MLA_KV_RANK] = qa[:, hd * MLA_KV_RANK:(hd + 1) * MLA_KV_RANK].astype(BF16)
            qm_ref[0, hd, rows, ext] = qx_ref[rows, 0:MLA_EXT]
            qm_ref[0, hd, rows, rot] = qr[:, hd * MLA_ROPE:(hd + 1) * MLA_ROPE].astype(BF16)
        kc_ref[0, rows, 0:MLA_KV_RANK] = c.astype(BF16)
        kc_ref[0, rows, ext] = kx_ref[rows, 0:MLA_EXT]
        kc_ref[0, rows, rot] = kr.astype(BF16)


def _mix_in(h, g, w1, gq, gkv, wuq, wuk, cos_t, sin_t, qx_t, kx_t, *, tm=2112, nsub=4):
    bsz, seq, _ = h.shape
    grid = (bsz, seq // tm)
    row3 = lambda b, j: (b, j, 0)
    return pl.pallas_call(
        functools.partial(_mix_in_kernel, tm=tm, nsub=nsub),
        grid=grid,
        in_specs=[
            pl.BlockSpec((1, tm, D_MODEL), row3),
            _const_spec((1, D_MODEL)),
            _const_spec((D_MODEL, C_END)),
            _const_spec((1, 256)),
            _const_spec((1, MLA_KV_RANK)),
            _const_spec((256, 512)),
            _const_spec((256, 512)),
            pl.BlockSpec((tm, 128), lambda b, j: (j, 0)),
            pl.BlockSpec((tm, 128), lambda b, j: (j, 0)),
            pl.BlockSpec((tm, 128), lambda b, j: (j, 0)),
            pl.BlockSpec((tm, 128), lambda b, j: (j, 0)),
        ],
        out_specs=[
            pl.BlockSpec((1, 2, tm, 128), lambda b, j: (b, 0, j, 0)),
            pl.BlockSpec((1, tm, SWA_WIDTH), row3),
            pl.BlockSpec((1, tm, SWA_KV_COLS), row3),
            pl.BlockSpec((1, MLA_HEADS, tm, MLA_QK), lambda b, j: (b, 0, j, 0)),
            pl.BlockSpec((1, tm, MLA_QK), row3),
        ],
        out_shape=[
            jax.ShapeDtypeStruct((bsz, 2, seq, 128), F32),
            jax.ShapeDtypeStruct((bsz, seq, SWA_WIDTH), BF16),
            jax.ShapeDtypeStruct((bsz, seq, SWA_KV_COLS), BF16),
            jax.ShapeDtypeStruct((bsz, MLA_HEADS, seq, MLA_QK), BF16),
            jax.ShapeDtypeStruct((bsz, seq, MLA_QK), BF16),
        ],
        compiler_params=pltpu.CompilerParams(
            dimension_semantics=("parallel", "parallel"), vmem_limit_bytes=VMEM_LIMIT),
        name="mix_in",
    )(h, g, w1, gq, gkv, wuq, wuk, cos_t, sin_t, qx_t, kx_t)


def _split_bf16(x):
    hi = x.astype(BF16)
    lo = (x - hi.astype(F32)).astype(BF16)
    return hi, lo


def _bdot3(a, b):
    ah, al = _split_bf16(a)
    bh, bl = _split_bf16(b)
    dn = (((2,), (2,)), ((0,), (0,)))
    f = lambda x, y: lax.dot_general(x, y, dn, preferred_element_type=F32)
    return f(ah, bh) + f(ah, bl) + f(al, bh)


def _expand_groups(a2, row_shift, col_shift, n_cols):
    rows, w = a2.shape
    tile = (lax.broadcasted_iota(jnp.int32, (w, n_cols), 1) & (w - 1)) == lax.broadcasted_iota(
        jnp.int32, (w, n_cols), 0)
    wide = jnp.dot(a2.astype(BF16), jnp.where(tile, 1.0, 0.0).astype(BF16), preferred_element_type=F32)
    same = (lax.broadcasted_iota(jnp.int32, (rows, n_cols), 0) >> row_shift) == (
        lax.broadcasted_iota(jnp.int32, (rows, n_cols), 1) >> col_shift)
    return jnp.where(same, wide, 0.0)


def _s5_operator_kernel(ar_ref, ai_ref, ls_ref, btr_ref, bti_ref, cr_ref, ci_ref,
                        r_ref, p_ref, q_ref, l8r_ref, l8i_ref, pwr_sc, pwi_sc, kk_sc):
    s = pl.program_id(1)
    ar, ai = ar_ref[0], ai_ref[0]
    step = jnp.exp(ls_ref[0])
    lr = jnp.exp(ar * step) * jnp.cos(ai * step)
    li = jnp.exp(ar * step) * jnp.sin(ai * step)

    den = ar * ar + ai * ai
    nr, ni = lr - 1.0, li
    coef_re = (nr * ar + ni * ai) / den
    coef_im = (ni * ar - nr * ai) / den
    btr, bti = btr_ref[0], bti_ref[0]
    bb_re = coef_re * btr - coef_im * bti
    bb_im = coef_re * bti + coef_im * btr
    cr, ci = cr_ref[0], ci_ref[0]

    def c_lam(k):
        qr_, qi_ = pwr_sc[k], pwi_sc[k]
        return cr * qr_ - ci * qi_, cr * qi_ + ci * qr_

    gc = SSM_GROUPS * SSM_GROUP_CH

    @pl.when(s == 0)
    def _():
        pr, pi = jnp.ones_like(lr), jnp.zeros_like(li)
        for k in range(SSM_CHUNK + 1):
            pwr_sc[k] = pr
            pwi_sc[k] = pi
            pr, pi = pr * lr - pi * li, pr * li + pi * lr
        for k in range(SSM_CHUNK):
            clr, cli = c_lam(k)
            kk_sc[k] = (_bdot3(bb_re, clr) - _bdot3(bb_im, cli)).reshape(gc, SSM_GROUP_CH)

    for t in range(SSM_CHUNK):
        lag = t - s
        blk = _expand_groups(kk_sc[jnp.maximum(lag, 0)], 4, 4, gc)
        r_ref[0, :, t * gc:(t + 1) * gc] = jnp.where(lag >= 0, blk, 0.0).astype(BF16)

    pw_r, pw_i = pwr_sc[SSM_CHUNK - 1 - s], pwi_sc[SSM_CHUNK - 1 - s]
    p_re = (pw_r * bb_re - pw_i * bb_im).reshape(gc, SSM_STATE)
    p_im = (pw_r * bb_im + pw_i * bb_re).reshape(gc, SSM_STATE)
    p_ref[0, :, 0:SSM_MODES] = _expand_groups(p_re, 4, 6, SSM_MODES).astype(BF16)
    p_ref[0, :, SSM_MODES:2 * SSM_MODES] = _expand_groups(p_im, 4, 6, SSM_MODES).astype(BF16)

    clr, cli = c_lam(s + 1)
    q_ref[0, 0:SSM_MODES, :] = _expand_groups(clr.reshape(gc, SSM_STATE), 4, 6, SSM_MODES).T.astype(BF16)
    q_ref[0, SSM_MODES:2 * SSM_MODES, :] = (
        -_expand_groups(cli.reshape(gc, SSM_STATE), 4, 6, SSM_MODES)).T.astype(BF16)

    l8r_ref[0] = pwr_sc[SSM_CHUNK]
    l8i_ref[0] = pwi_sc[SSM_CHUNK]


def _s5_operators(a_re, a_im, log_step, b_re, b_im, c_re, c_im):
    depth = a_re.shape[0]
    G, P, C, T = SSM_GROUPS, SSM_STATE, SSM_GROUP_CH, SSM_CHUNK
    lay = lambda *shape: pl.BlockSpec((1,) + shape, lambda l, s: (l,) + (0,) * len(shape))
    ls = jnp.broadcast_to(log_step[:, :, None, None], (depth, G, 1, P))
    args = (a_re[:, :, None, :], a_im[:, :, None, :], ls,
            b_re.transpose(0, 1, 3, 2), b_im.transpose(0, 1, 3, 2), c_re, c_im)
    r_mat, p_mat, q_mat, l8r, l8i = pl.pallas_call(
        _s5_operator_kernel,
        grid=(depth, T),
        in_specs=[lay(G, 1, P)] * 3 + [lay(G, C, P)] * 4,
        out_specs=[
            pl.BlockSpec((1, SSM_WIDTH, SSM_FOLD), lambda l, s: (l, s, 0)),
            pl.BlockSpec((1, SSM_WIDTH, 2 * SSM_MODES), lambda l, s: (l, s, 0)),
            pl.BlockSpec((1, 2 * SSM_MODES, SSM_WIDTH), lambda l, s: (l, 0, s)),
            lay(G, 1, P), lay(G, 1, P)],
        out_shape=[
            jax.ShapeDtypeStruct((depth, SSM_FOLD, SSM_FOLD), BF16),
            jax.ShapeDtypeStruct((depth, SSM_FOLD, 2 * SSM_MODES), BF16),
            jax.ShapeDtypeStruct((depth, 2 * SSM_MODES, SSM_FOLD), BF16),
            jax.ShapeDtypeStruct((depth, G, 1, P), F32),
            jax.ShapeDtypeStruct((depth, G, 1, P), F32),
        ],
        scratch_shapes=[pltpu.VMEM((T + 1, G, 1, P), F32), pltpu.VMEM((T + 1, G, 1, P), F32),
                        pltpu.VMEM((T, G * C, C), F32)],
        compiler_params=pltpu.CompilerParams(dimension_semantics=("parallel", "arbitrary")),
        name="s5_operators",
    )(*args)
    lam = jnp.concatenate([l8r.reshape(depth, 1, SSM_MODES), l8i.reshape(depth, 1, SSM_MODES)], axis=2)
    return r_mat, p_mat, q_mat, lam


def _group_block_diag(x):
    *lead, G, r, c = x.shape
    eye = jnp.eye(G, dtype=x.dtype)
    y = x[..., :, :, None, :] * eye[:, None, :, None]
    return y.reshape(*lead, G * r, G * c)


def _fold_tokens(u_ref, n_chunks):
    return jnp.concatenate(
        [u_ref[0, c, pl.ds(s, n_chunks, stride=SSM_CHUNK), :] for s in range(SSM_CHUNK) for c in range(2)], axis=1)


def _s5_state_kernel(u_ref, p_ref, lam_ref, x_ref, s_sc, x_sc, *, n_chunks):
    u = _fold_tokens(u_ref, n_chunks).astype(BF16)
    s_sc[...] = jnp.dot(u, p_ref[0], preferred_element_type=F32)
    lr = lam_ref[0, :, 0:SSM_MODES]
    li = lam_ref[0, :, SSM_MODES:2 * SSM_MODES]

    def body(n, carry):
        xr, xi = carry
        x_sc[pl.ds(n, 1), 0:SSM_MODES] = xr
        x_sc[pl.ds(n, 1), SSM_MODES:2 * SSM_MODES] = xi
        sr = s_sc[pl.ds(n, 1), 0:SSM_MODES]
        si = s_sc[pl.ds(n, 1), SSM_MODES:2 * SSM_MODES]
        return lr * xr - li * xi + sr, lr * xi + li * xr + si

    z = jnp.zeros((1, SSM_MODES), F32)
    lax.fori_loop(0, n_chunks, body, (z, z))
    x_ref[0] = x_sc[...].astype(BF16)


def _s5_out_kernel(u_ref, x_ref, r_ref, q_ref, d_ref, y_ref, *, n_chunks):
    u = _fold_tokens(u_ref, n_chunks)
    ub = u.astype(BF16)
    x = x_ref[0]
    for t in range(SSM_CHUNK):
        cols = slice(t * SSM_WIDTH, (t + 1) * SSM_WIDTH)
        rows = (t + 1) * SSM_WIDTH
        y = jnp.dot(ub[:, 0:rows], r_ref[0, 0:rows, cols], preferred_element_type=F32)
        y += jnp.dot(x, q_ref[0, :, cols], preferred_element_type=F32)
        y += d_ref[:, cols] * u[:, cols]
        for c in range(2):
            y_ref[0, c, pl.ds(t, n_chunks, stride=SSM_CHUNK), :] = y[:, c * 128:(c + 1) * 128]


def _s5_sequence(u, r_mat, p_mat, q_mat, lam, d_fold, layer):
    bsz, _, seq, _ = u.shape
    n_chunks = seq // SSM_CHUNK
    params = pltpu.CompilerParams(dimension_semantics=("parallel",), vmem_limit_bytes=VMEM_LIMIT)
    slabs = pl.BlockSpec((1, 2, seq, 128), lambda b: (b, 0, 0, 0))
    states = pl.BlockSpec((1, n_chunks, 2 * SSM_MODES), lambda b: (b, 0, 0))
    x_prev = pl.pallas_call(
        functools.partial(_s5_state_kernel, n_chunks=n_chunks), grid=(bsz,),
        in_specs=[slabs, _layer_spec((SSM_FOLD, 2 * SSM_MODES), layer), _layer_spec((1, 2 * SSM_MODES), layer)],
        out_specs=states,
        out_shape=jax.ShapeDtypeStruct((bsz, n_chunks, 2 * SSM_MODES), BF16),
        scratch_shapes=[pltpu.VMEM((n_chunks, 2 * SSM_MODES), F32), pltpu.VMEM((n_chunks, 2 * SSM_MODES), F32)],
        compiler_params=params, name="s5_state",
    )(u, p_mat, lam)
    return pl.pallas_call(
        functools.partial(_s5_out_kernel, n_chunks=n_chunks), grid=(bsz,),
        in_specs=[slabs, states, _layer_spec((SSM_FOLD, SSM_FOLD), layer),
                  _layer_spec((2 * SSM_MODES, SSM_FOLD), layer), _const_spec((1, SSM_FOLD))],
        out_specs=slabs,
        out_shape=jax.ShapeDtypeStruct((bsz, 2, seq, 128), F32),
        compiler_params=params, name="s5_out",
    )(u, x_prev, r_mat, q_mat, d_fold)


def _swa_kernel(q_ref, kvp_ref, kvc_ref, bias_ref, sink_ref, o_ref, *, nsub):
    i = pl.program_id(1)
    nt_dims = (((1,), (1,)), ((), ()))
    tn_dims = (((0,), (0,)), ((), ()))
    chains = [(r, hk) for r in range(nsub) for hk in range(SWA_KV_HEADS)]
    scores, vcats = [], []
    for r, hk in chains:
        rows = slice(r * BLOCK, (r + 1) * BLOCK)
        q = q_ref[0, rows, :]
        kvc = kvc_ref[0, rows, :]
        kvp = kvp_ref[0] if r == 0 else kvc_ref[0, (r - 1) * BLOCK:r * BLOCK, :]
        ks = slice(hk * HEAD_DIM, (hk + 1) * HEAD_DIM)
        vs = slice(SWA_KV_WIDTH + hk * BLOCK, SWA_KV_WIDTH + (hk + 1) * BLOCK)
        kcat = jnp.concatenate([kvp[:, ks], kvc[:, ks]], axis=0)
        vcats.append(jnp.concatenate([kvp[:, vs], kvc[:, vs]], axis=0))
        qs = jnp.concatenate(
            [q[:, (hk * SWA_GROUP + g) * HEAD_DIM:(hk * SWA_GROUP + g + 1) * HEAD_DIM]
             for g in range(SWA_GROUP)], axis=0)
        table = jnp.minimum(i * nsub + r, 2)
        scores.append(lax.dot_general(kcat, qs, nt_dims, preferred_element_type=F32) + bias_ref[table, hk])
    weights, sink_terms = [], []
    for (r, hk), s in zip(chains, scores):
        sink = sink_ref[hk]
        m = jnp.maximum(jnp.max(s, axis=0, keepdims=True), sink)
        weights.append(jnp.exp2(s - m).astype(BF16))
        sink_terms.append(jnp.exp2(sink - m))
    o_ts = []
    for v, p, sink_term in zip(vcats, weights, sink_terms):
        o_t = lax.dot_general(v, p, tn_dims, preferred_element_type=F32)
        o_ts.append(o_t[0:HEAD_DIM, :] / (o_t[HEAD_DIM:HEAD_DIM + 1, :] + sink_term))
    for r in range(nsub):
        outs = []
        for hk in range(SWA_KV_HEADS):
            o_t = o_ts[r * SWA_KV_HEADS + hk]
            outs += [o_t[:, g * BLOCK:(g + 1) * BLOCK] for g in range(SWA_GROUP)]
        o_ref[0, r * BLOCK:(r + 1) * BLOCK, :] = jnp.concatenate(outs, axis=0).T


def _swa(q, kv, bias, sink_rows, *, nsub=11):
    bsz, seq, _ = q.shape
    tm = nsub * BLOCK
    cols = SWA_GROUP * BLOCK
    return pl.pallas_call(
        functools.partial(_swa_kernel, nsub=nsub),
        grid=(bsz, seq // tm),
        in_specs=[
            pl.BlockSpec((1, tm, SWA_WIDTH), lambda b, i: (b, i, 0)),
            pl.BlockSpec((1, BLOCK, SWA_KV_COLS), lambda b, i: (b, jnp.maximum(nsub * i - 1, 0), 0)),
            pl.BlockSpec((1, tm, SWA_KV_COLS), lambda b, i: (b, i, 0)),
            _const_spec((3, SWA_KV_HEADS, 2 * BLOCK, cols)),
            _const_spec((SWA_KV_HEADS, 1, cols)),
        ],
        out_specs=pl.BlockSpec((1, tm, SWA_WIDTH), lambda b, i: (b, i, 0)),
        out_shape=jax.ShapeDtypeStruct((bsz, seq, SWA_WIDTH), F32),
        compiler_params=pltpu.CompilerParams(
            dimension_semantics=("parallel", "parallel"), vmem_limit_bytes=VMEM_LIMIT),
        name="swa",
    )(q, kv, kv, bias, sink_rows)


def _mla_kernel(q_ref, k_ref, wuv_ref, o_ref, sa_sc, sb_sc, acc_sc, *, tq, rows):
    i = pl.program_id(1)
    cols = MLA_HEADS * tq
    nsub = tq // BLOCK
    q = q_ref[0].reshape(cols, MLA_QK)
    n_groups = ((i + 1) * tq + rows - 1) // rows
    nt_dims = (((1,), (1,)), ((), ()))
    tn_dims = (((0,), (0,)), ((), ()))

    def fold(x, op):
        return op(x.reshape(x.shape[0] // 8, 8, x.shape[1]), axis=0)

    n_parts = rows // MLA_PART

    def scores_part(g, buf, c):
        start = pl.multiple_of(g * rows + c * MLA_PART, MLA_PART)
        s = lax.dot_general(k_ref[0, pl.ds(start, MLA_PART), :], q, nt_dims, preferred_element_type=F32)
        buf[c * MLA_PART:(c + 1) * MLA_PART, :] = s
        return fold(s, jnp.max)

    def rescale(mx8, m_run):
        m_new = jnp.maximum(m_run, jnp.max(mx8, axis=0, keepdims=True))
        acc_sc[...] = acc_sc[...] * jnp.exp2(m_run - m_new)
        return m_new

    def weights_part(g, buf, c, m_new):
        start = pl.multiple_of(g * rows + c * MLA_PART, MLA_PART)
        p = jnp.exp2(buf[c * MLA_PART:(c + 1) * MLA_PART, :] - m_new).astype(BF16)
        v = k_ref[0, pl.ds(start, MLA_PART), 0:MLA_ACC]
        acc_sc[...] += lax.dot_general(v, p, tn_dims, preferred_element_type=F32)

    def scores(g, buf):
        return functools.reduce(jnp.maximum, [scores_part(g, buf, c) for c in range(n_parts)])

    def weights(g, buf, mx8, m_run):
        m_new = rescale(mx8, m_run)
        for c in range(n_parts):
            weights_part(g, buf, c, m_new)
        return m_new

    def overlapped(g_next, buf_next, g, buf, mx8, m_run):
        m_new = rescale(mx8, m_run)
        parts = []
        for c in range(n_parts):
            parts.append(scores_part(g_next, buf_next, c))
            weights_part(g, buf, c, m_new)
        return functools.reduce(jnp.maximum, parts), m_new

    acc_sc[...] = jnp.zeros((MLA_ACC, cols), F32)
    m0 = jnp.full((1, cols), NEG, F32)

    def pair(k, carry):
        mxa, m_run = carry
        mxb, m_run = overlapped(2 * k + 1, sb_sc, 2 * k, sa_sc, mxa, m_run)
        mxa, m_run = overlapped(2 * k + 2, sa_sc, 2 * k + 1, sb_sc, mxb, m_run)
        return mxa, m_run

    n_pairs = (n_groups - 1) // 2
    mxa, m_run = lax.fori_loop(0, n_pairs, pair, (scores(0, sa_sc), m0))

    def one_left(m_run):
        return weights(n_groups - 1, sa_sc, mxa, m_run)

    def two_left(m_run):
        mxb, m_run = overlapped(n_groups - 1, sb_sc, n_groups - 2, sa_sc, mxa, m_run)
        return weights(n_groups - 1, sb_sc, mxb, m_run)

    m_run = lax.cond(n_groups - 1 == 2 * n_pairs, one_left, two_left, m_run)

    lane = lax.broadcasted_iota(jnp.int32, (1, MLA_QK), 1)
    onehot = (lane >= MLA_KV_RANK + 2) & (lane < MLA_KV_RANK + 2 + MLA_BLOCKS)
    hcols = MLA_HEADS * BLOCK
    causal = lax.broadcasted_iota(jnp.int32, (BLOCK, hcols), 0) <= (
        lax.broadcasted_iota(jnp.int32, (BLOCK, hcols), 1) & (BLOCK - 1))
    row0s = [pl.multiple_of((i * nsub + r) * BLOCK, BLOCK) for r in range(nsub)]
    lanes = [[slice(hd * tq + r * BLOCK, hd * tq + (r + 1) * BLOCK) for hd in range(MLA_HEADS)]
             for r in range(nsub)]
    sds = []
    for r in range(nsub):
        qd = jnp.concatenate([q[ln, :] for ln in lanes[r]], axis=0)
        qd = jnp.where(onehot, jnp.zeros((), BF16), qd)
        sd = lax.dot_general(k_ref[0, pl.ds(row0s[r], BLOCK), :], qd, nt_dims, preferred_element_type=F32)
        sds.append(jnp.where(causal, sd, NEG))
    ps, alphas = [], []
    for r in range(nsub):
        m_old = jnp.concatenate([m_run[:, ln] for ln in lanes[r]], axis=1)
        m_new = jnp.maximum(m_old, jnp.max(sds[r], axis=0, keepdims=True))
        ps.append(jnp.exp2(sds[r] - m_new).astype(BF16))
        alphas.append(jnp.exp2(m_old - m_new))
    pvs = [lax.dot_general(k_ref[0, pl.ds(row0s[r], BLOCK), 0:MLA_ACC], ps[r], tn_dims,
                           preferred_element_type=F32) for r in range(nsub)]
    o_parts = []
    for r in range(nsub):
        acc = jnp.concatenate([acc_sc[:, ln] for ln in lanes[r]], axis=1) * alphas[r] + pvs[r]
        o_parts.append((acc[0:MLA_KV_RANK, :] / acc[MLA_KV_RANK:MLA_KV_RANK + 1, :]).astype(BF16))
    ys = [[lax.dot_general(o_parts[r][:, hd * BLOCK:(hd + 1) * BLOCK], wuv_ref[hd], tn_dims,
                           preferred_element_type=F32) for hd in range(MLA_HEADS)] for r in range(nsub)]
    o_ref[0] = jnp.concatenate([functools.reduce(jnp.add, ys[r]) for r in range(nsub)], axis=0)


def _mla(qm, kc_padded, wuv_heads, *, tq=384, rows=MLA_KEY_ROWS):
    bsz, _, seq, _ = qm.shape
    seq_k = kc_padded.shape[1]
    cols = MLA_HEADS * tq
    return pl.pallas_call(
        functools.partial(_mla_kernel, tq=tq, rows=rows),
        grid=(bsz, seq // tq),
        in_specs=[
            pl.BlockSpec((1, MLA_HEADS, tq, MLA_QK), lambda b, i: (b, 0, i, 0)),
            pl.BlockSpec((1, seq_k, MLA_QK), lambda b, i: (b, 0, 0)),
            _const_spec((MLA_HEADS, MLA_KV_RANK, MLA_WIDTH)),
        ],
        out_specs=pl.BlockSpec((1, tq, MLA_WIDTH), lambda b, i: (b, i, 0)),
        out_shape=jax.ShapeDtypeStruct((bsz, seq, MLA_WIDTH), F32),
        scratch_shapes=[pltpu.VMEM((rows, cols), F32), pltpu.VMEM((rows, cols), F32),
                        pltpu.VMEM((MLA_ACC, cols), F32)],
        compiler_params=pltpu.CompilerParams(
            dimension_semantics=("parallel", "arbitrary"), vmem_limit_bytes=VMEM_LIMIT),
        name="mla",
    )(qm, kc_padded, wuv_heads)


def _gelu_tanh(x):
    return 0.5 * x * (1.0 + jnp.tanh(math.sqrt(2.0 / math.pi) * (x + 0.044715 * (x * x * x))))


def _mix_out_kernel(h_ref, ys_ref, yw_ref, ym_ref, wglu_ref, bglu_ref, gh_ref, wout_ref, gpm_ref, gpre_ref,
                    wup_ref, wdn_ref, gpost_ref, o_ref, *, ff_chunk):
    z = _gelu_tanh(jnp.concatenate([ys_ref[0, 0], ys_ref[0, 1]], axis=1))
    gate = jnp.dot(z.astype(BF16), wglu_ref[0], preferred_element_type=F32) + bglu_ref[...]
    y_ssm = z * (1.0 / (1.0 + jnp.exp(-gate)))
    gh = gh_ref[...]
    mixed = jnp.concatenate([
        _rms(y_ssm, gh[:, 0:SSM_WIDTH]),
        _rms(yw_ref[0], gh[:, SSM_WIDTH:SSM_WIDTH + SWA_WIDTH]),
        _rms(ym_ref[0], gh[:, SSM_WIDTH + SWA_WIDTH:]),
    ], axis=1).astype(BF16)
    mo = jnp.dot(mixed, wout_ref[0], preferred_element_type=F32)
    h1 = h_ref[0] + _rms(mo, gpm_ref[...])
    a = _rms(h1, gpre_ref[...]).astype(BF16)
    f = jnp.zeros_like(h1)
    for c in range(D_FF // ff_chunk):
        up = jnp.dot(a, wup_ref[0, :, c * ff_chunk:(c + 1) * ff_chunk], preferred_element_type=F32)
        hid = jnp.square(jnp.maximum(up, 0.0)).astype(BF16)
        f += jnp.dot(hid, wdn_ref[0, c * ff_chunk:(c + 1) * ff_chunk, :], preferred_element_type=F32)
    o_ref[0] = h1 + _rms(f, gpost_ref[...])


def _mix_out(h, ys, yw, ym, wglu, bglu, gh, wout, gpm, gpre, wup, wdn, gpost, layer, *, skip=0, tm=1056,
             ff_chunk=1024):
    bsz, seq, _ = h.shape
    out_rows = seq - skip
    if skip:
        el = pl.Element
        first = lambda i: pl.multiple_of(skip + i * tm, BLOCK)
        row = lambda w: pl.BlockSpec((el(1), el(tm), el(w)), lambda b, i: (b, first(i), 0))
        slabs = pl.BlockSpec((el(1), el(2), el(tm), el(128)), lambda b, i: (b, 0, first(i), 0))
    else:
        row = lambda w: pl.BlockSpec((1, tm, w), lambda b, i: (b, i, 0))
        slabs = pl.BlockSpec((1, 2, tm, 128), lambda b, i: (b, 0, i, 0))
    return pl.pallas_call(
        functools.partial(_mix_out_kernel, ff_chunk=ff_chunk),
        grid=(bsz, out_rows // tm),
        in_specs=[
            row(D_MODEL), slabs, row(SWA_WIDTH), row(MLA_WIDTH),
            _layer_spec((SSM_WIDTH, SSM_WIDTH), layer), _const_spec((1, SSM_WIDTH)), _const_spec((1, D_MODEL)),
            _layer_spec((D_MODEL, D_MODEL), layer), _const_spec((1, D_MODEL)), _const_spec((1, D_MODEL)),
            _layer_spec((D_MODEL, D_FF), layer), _layer_spec((D_FF, D_MODEL), layer), _const_spec((1, D_MODEL)),
        ],
        out_specs=pl.BlockSpec((1, tm, D_MODEL), lambda b, i: (b, i, 0)),
        out_shape=jax.ShapeDtypeStruct((bsz, out_rows, D_MODEL), F32),
        compiler_params=pltpu.CompilerParams(
            dimension_semantics=("parallel", "parallel"), vmem_limit_bytes=VMEM_LIMIT),
        name="mix_out",
    )(h, ys, yw, ym, wglu, bglu, gh, wout, gpm, gpre, wup, wdn, gpost)


def _rot_half_cols(w):
    half = w.shape[-1] // 2
    return jnp.concatenate([-w[..., half:], w[..., :half]], axis=-1)


def _layer_weights(w_in, w_uq, w_uk, w_uv):
    z = lambda n: jnp.zeros((D_MODEL, n), w_in.dtype)
    s_cq = SSM_WIDTH + SWA_WIDTH + 2 * SWA_KV_WIDTH
    s_ckv = s_cq + MLA_Q_RANK
    s_kr = s_ckv + MLA_KV_RANK
    kr = w_in[:, s_kr:s_kr + MLA_ROPE]
    w1 = jnp.concatenate([
        w_in[:, :s_cq], w_in[:, s_cq:s_ckv], z(C_CKV - C_CQ - MLA_Q_RANK),
        w_in[:, s_ckv:s_kr], kr, _rot_half_cols(kr), z(C_END - C_KR - 2 * MLA_ROPE)], axis=1).astype(BF16)
    uq = w_uq.reshape(MLA_Q_RANK, MLA_HEADS, MLA_NOPE + MLA_ROPE)
    rope = uq[:, :, MLA_NOPE:]
    wuq = jnp.concatenate([
        uq[:, :, :MLA_NOPE].reshape(MLA_Q_RANK, -1), rope.reshape(MLA_Q_RANK, -1),
        _rot_half_cols(rope).reshape(MLA_Q_RANK, -1)], axis=1)
    wuq = jnp.concatenate([wuq, jnp.zeros((256 - MLA_Q_RANK, wuq.shape[1]), wuq.dtype)], axis=0).astype(BF16)
    uk = w_uk.reshape(MLA_KV_RANK, MLA_HEADS, MLA_NOPE).transpose(1, 2, 0)
    wuk_bd = _group_block_diag(uk).astype(BF16)
    uv = w_uv.reshape(MLA_KV_RANK, MLA_HEADS, MLA_V).transpose(1, 0, 2)
    wuv_heads = _group_block_diag(uv).reshape(MLA_HEADS, MLA_KV_RANK, MLA_WIDTH).astype(BF16)
    return w1, wuq, wuk_bd, wuv_heads


def kernel(x, meta_tokens, norm_pre_mix, norm_post_mix, norm_pre_mlp, norm_post_mlp, w_in, w_out, norm_heads,
           ssm_a_re, ssm_a_im, ssm_log_step, ssm_b_re, ssm_b_im, ssm_c_re, ssm_c_im, ssm_d, ssm_w_glu,
           ssm_b_glu, swa_sinks, mla_q_norm, mla_kv_norm, mla_w_uq, mla_w_uk, mla_w_uv, w_mlp_up, w_mlp_down):
    bsz, seq_real, d = x.shape
    depth = w_in.shape[0]
    meta = jnp.broadcast_to(meta_tokens.astype(x.dtype)[None], (bsz, N_META, d))
    h = jnp.concatenate([jnp.zeros((bsz, PAD, d), x.dtype), meta, x], axis=1)
    seq = h.shape[1]

    half = MLA_ROPE // 2
    inv_freq = ROPE_THETA ** (-jnp.arange(half, dtype=F32) / half)
    ang = (jnp.arange(seq) - PAD).astype(F32)[:, None] * inv_freq[None, :]
    cos_t = jnp.tile(jnp.cos(ang), (1, 2 * MLA_HEADS))
    sin_t = jnp.tile(jnp.sin(ang), (1, 2 * MLA_HEADS))
    seq_k = -(-seq // MLA_KEY_ROWS) * MLA_KEY_ROWS
    kblk = jnp.arange(seq_k)[:, None] // BLOCK
    bcol = jnp.arange(MLA_BLOCKS)[None, :]
    one = jnp.ones((seq_k, 1), F32)
    fill = jnp.zeros((seq_k, 128 - 2 - MLA_BLOCKS), F32)
    kx_t = jnp.concatenate([one, jnp.where(jnp.arange(seq_k)[:, None] < PAD, NEG, 0.0),
                            (kblk == bcol).astype(F32), fill], axis=1).astype(BF16)
    qx_t = jnp.concatenate([0.0 * one, one, jnp.where(bcol >= kblk, NEG, 0.0), fill], axis=1).astype(BF16)[:seq]
    key_tail = jnp.zeros((bsz, seq_k - seq, MLA_QK), BF16).at[:, :, MLA_KV_RANK:MLA_KV_RANK + MLA_EXT].set(
        kx_t[seq:, :MLA_EXT])
    slopes = 2.0 ** (-8.0 * jnp.arange(1, SWA_HEADS + 1, dtype=F32) / SWA_HEADS)
    per_row = lambda v: jnp.repeat(v.reshape(SWA_KV_HEADS, SWA_GROUP), BLOCK, axis=1)[:, None, :]
    dist = (jnp.arange(BLOCK)[None, :] + BLOCK) - jnp.arange(2 * BLOCK)[:, None]
    in_window = jnp.tile((dist >= 0) & (dist < BLOCK), (1, SWA_GROUP))[None, None]
    alibi = (-LOG2E * per_row(slopes) * jnp.tile(dist.astype(F32), (1, SWA_GROUP))[None])[None]
    key_pos = (jnp.arange(3)[:, None] - 1) * BLOCK + jnp.arange(2 * BLOCK)[None, :]
    key_ok = ((key_pos >= PAD) | (jnp.arange(3)[:, None] >= 2))[:, None, :, None]
    swa_bias = jnp.where(in_window & key_ok, alibi, NEG)

    r_mat, p_mat, q_mat, lam = _s5_operators(
        ssm_a_re, ssm_a_im, ssm_log_step, ssm_b_re, ssm_b_im, ssm_c_re, ssm_c_im)
    row = lambda v: v.reshape(1, -1).astype(F32)
    wglu, wout, wup, wdn = (w.astype(BF16) for w in (ssm_w_glu, w_out, w_mlp_up, w_mlp_down))

    for l in range(depth):
        w1, wuq, wuk_bd, wuv_heads = _layer_weights(w_in[l], mla_w_uq[l], mla_w_uk[l], mla_w_uv[l])
        gq = jnp.concatenate([mla_q_norm[l], jnp.zeros((256 - MLA_Q_RANK,), F32)]).reshape(1, 256)
        u, q, kv, qm, kc = _mix_in(h, row(norm_pre_mix[l]), w1, gq, row(mla_kv_norm[l]), wuq, wuk_bd,
                                   cos_t, sin_t, qx_t, kx_t[:seq])
        d_fold = jnp.tile(ssm_d[l].astype(F32), SSM_CHUNK).reshape(1, SSM_FOLD)
        y_ssm = _s5_sequence(u, r_mat, p_mat, q_mat, lam, d_fold, l)
        y_swa = _swa(q, kv, swa_bias, per_row(LOG2E * swa_sinks[l].astype(F32)))
        kc_padded = jnp.concatenate([kc, key_tail], axis=1)
        y_mla = _mla(qm, kc_padded, wuv_heads)
        last = l == depth - 1
        h = _mix_out(
            h, y_ssm, y_swa, y_mla,
            wglu, row(ssm_b_glu[l]), row(norm_heads[l]), wout, row(norm_post_mix[l]), row(norm_pre_mlp[l]),
            wup, wdn, row(norm_post_mlp[l]), l, skip=PAD + N_META if last else 0, tm=1024 if last else 1056)
    return h
```

```python
import functools
import math

import jax
import jax.numpy as jnp
from jax import lax
from jax.experimental import pallas as pl
from jax.experimental.pallas import tpu as pltpu

F32 = jnp.float32
BF16 = jnp.bfloat16

D_MODEL = 1024
N_META = 16
BLOCK = 128
PAD = BLOCK - N_META
SSM_GROUPS = 16
SSM_GROUP_CH = 16
SSM_WIDTH = SSM_GROUPS * SSM_GROUP_CH
SSM_STATE = 64
SWA_HEADS = 8
SWA_KV_HEADS = 2
SWA_GROUP = SWA_HEADS // SWA_KV_HEADS
HEAD_DIM = 64
SWA_WIDTH = SWA_HEADS * HEAD_DIM
SWA_KV_WIDTH = SWA_KV_HEADS * HEAD_DIM
SWA_KV_COLS = SWA_KV_WIDTH + SWA_KV_HEADS * BLOCK
MLA_HEADS = 4
MLA_Q_RANK = 192
MLA_KV_RANK = 128
MLA_NOPE = 64
MLA_ROPE = 32
MLA_V = 64
MLA_WIDTH = MLA_HEADS * MLA_V
MLA_BLOCKS = 40
MLA_EXT = 48
MLA_ACC = MLA_KV_RANK + 8
LOG2E = 1.4426950408889634
MLA_QK = MLA_KV_RANK + MLA_EXT + MLA_ROPE
ROPE_THETA = 10000.0
D_FF = 4 * D_MODEL
EPS = 1e-6
NEG = -1e30

SSM_CHUNK = 8
SSM_FOLD = SSM_CHUNK * SSM_WIDTH
SSM_MODES = SSM_GROUPS * SSM_STATE

C_U, C_Q, C_K, C_CQ, C_CKV, C_KR, C_END = 0, 256, 768, 1024, 1280, 1408, 1536

MLA_KEY_ROWS = 512
MLA_PART = 256

VMEM_LIMIT = 58 * 1024 * 1024


def _rms(x, g, n=None):
    n = x.shape[-1] if n is None else n
    ms = jnp.sum(x * x, axis=-1, keepdims=True) * (1.0 / n)
    return x * lax.rsqrt(ms + EPS) * g


def _const_spec(shape):
    nd = len(shape)
    return pl.BlockSpec(shape, lambda *_: (0,) * nd, pipeline_mode=pl.Buffered(1))


def _seq_window_spec(tm, width):
    el = pl.Element
    return pl.BlockSpec((el(1), el(tm), el(width)),
                        lambda b, i: (b, pl.multiple_of(jnp.maximum(i * tm - BLOCK, 0), 8), 0))


def _seq_rows(x_ref, head_ref, i, lo, n):
    first = jnp.where(i == 0, 1, 0)
    if lo == 0:
        top = jnp.where(i == 0, head_ref[...], x_ref[0, 0:BLOCK, :])
        rest = x_ref[0, pl.ds(pl.multiple_of(BLOCK - BLOCK * first, 8), n - BLOCK), :]
        return jnp.concatenate([top, rest], axis=0)
    return x_ref[0, pl.ds(pl.multiple_of(lo - BLOCK * first, 8), n), :]


def _layer_spec(shape, layer):
    nd = len(shape)
    return pl.BlockSpec((1,) + shape, lambda *_: (layer,) + (0,) * nd, pipeline_mode=pl.Buffered(1))


def _mix_in_kernel(h_ref, head_ref, g_ref, w1_ref, gq_ref, gkv_ref, wuq_ref, wuk_ref, cos_ref, sin_ref, qx_ref,
                   kx_ref, u_ref, q_ref, kv_ref, qm_ref, kc_ref, *, tm, nsub, from_x):
    j = pl.program_id(1)
    ts = tm // nsub
    scale = LOG2E * (MLA_NOPE + MLA_ROPE) ** -0.5
    ext = slice(MLA_KV_RANK, MLA_KV_RANK + MLA_EXT)
    rot = slice(MLA_KV_RANK + MLA_EXT, MLA_QK)
    for sub in range(nsub):
        rows = slice(sub * ts, (sub + 1) * ts)
        h_rows = _seq_rows(h_ref, head_ref, j, sub * ts, ts) if from_x else h_ref[0, rows, :]
        a = _rms(h_rows, g_ref[...]).astype(BF16)
        proj = jnp.dot(a, w1_ref[...], preferred_element_type=F32)
        row = j * tm + sub * ts + lax.broadcasted_iota(jnp.int32, (ts, 1), 0)
        for c in range(2):
            u_ref[0, c, rows, :] = jnp.where(row >= PAD, proj[:, C_U + c * 128:C_U + (c + 1) * 128], 0.0)
        q_ref[0, rows, :] = (proj[:, C_Q:C_K] * (LOG2E * HEAD_DIM ** -0.5)).astype(BF16)
        kv_ref[0, rows, 0:SWA_KV_WIDTH] = proj[:, C_K:C_K + SWA_KV_WIDTH].astype(BF16)
        ones_col = jnp.where(lax.broadcasted_iota(jnp.int32, (ts, HEAD_DIM), 1) == 0, 1.0, 0.0).astype(BF16)
        for hk in range(SWA_KV_HEADS):
            lo = SWA_KV_WIDTH + hk * BLOCK
            v_lo = C_K + SWA_KV_WIDTH + hk * HEAD_DIM
            kv_ref[0, rows, lo:lo + HEAD_DIM] = proj[:, v_lo:v_lo + HEAD_DIM].astype(BF16)
            kv_ref[0, rows, lo + HEAD_DIM:lo + BLOCK] = ones_col

        cos = cos_ref[rows, :]
        sin = sin_ref[rows, :]
        qn = _rms(proj[:, C_CQ:C_CKV], gq_ref[...], n=MLA_Q_RANK).astype(BF16)
        q2 = jnp.dot(qn, wuq_ref[...], preferred_element_type=F32)
        qr = (q2[:, 256:384] * cos + q2[:, 384:512] * sin) * scale
        qa = jnp.dot(q2[:, 0:256].astype(BF16), wuk_ref[...], preferred_element_type=F32) * scale
        c = _rms(proj[:, C_CKV:C_KR], gkv_ref[...])
        krr = proj[:, C_KR:C_END]
        kr = krr[:, 0:MLA_ROPE] * cos[:, 0:MLA_ROPE] + krr[:, MLA_ROPE:2 * MLA_ROPE] * sin[:, 0:MLA_ROPE]
        for hd in range(MLA_HEADS):
            qm_ref[0, hd, rows, 0:MLA_KV_RANK] = qa[:, hd * MLA_KV_RANK:(hd + 1) * MLA_KV_RANK].astype(BF16)
            qm_ref[0, hd, rows, ext] = qx_ref[rows, 0:MLA_EXT]
            qm_ref[0, hd, rows, rot] = qr[:, hd * MLA_ROPE:(hd + 1) * MLA_ROPE].astype(BF16)
        kc_ref[0, rows, 0:MLA_KV_RANK] = c.astype(BF16)
        kc_ref[0, rows, ext] = kx_ref[rows, 0:MLA_EXT]
        kc_ref[0, rows, rot] = kr.astype(BF16)


def _mix_in(h, head, g, w1, gq, gkv, wuq, wuk, cos_t, sin_t, qx_t, kx_t, *, from_x, tm=2112, nsub=4):
    bsz = h.shape[0]
    seq = cos_t.shape[0]
    grid = (bsz, seq // tm)
    row3 = lambda b, j: (b, j, 0)
    return pl.pallas_call(
        functools.partial(_mix_in_kernel, tm=tm, nsub=nsub, from_x=from_x),
        grid=grid,
        in_specs=[
            _seq_window_spec(tm, D_MODEL) if from_x else pl.BlockSpec((1, tm, D_MODEL), row3),
            _const_spec((BLOCK, D_MODEL)),
            _const_spec((1, D_MODEL)),
            _const_spec((D_MODEL, C_END)),
            _const_spec((1, 256)),
            _const_spec((1, MLA_KV_RANK)),
            _const_spec((256, 512)),
            _const_spec((256, 512)),
            pl.BlockSpec((tm, 128), lambda b, j: (j, 0)),
            pl.BlockSpec((tm, 128), lambda b, j: (j, 0)),
            pl.BlockSpec((tm, 128), lambda b, j: (j, 0)),
            pl.BlockSpec((tm, 128), lambda b, j: (j, 0)),
        ],
        out_specs=[
            pl.BlockSpec((1, 2, tm, 128), lambda b, j: (b, 0, j, 0)),
            pl.BlockSpec((1, tm, SWA_WIDTH), row3),
            pl.BlockSpec((1, tm, SWA_KV_COLS), row3),
            pl.BlockSpec((1, MLA_HEADS, tm, MLA_QK), lambda b, j: (b, 0, j, 0)),
            pl.BlockSpec((1, tm, MLA_QK), row3),
        ],
        out_shape=[
            jax.ShapeDtypeStruct((bsz, 2, seq, 128), F32),
            jax.ShapeDtypeStruct((bsz, seq, SWA_WIDTH), BF16),
            jax.ShapeDtypeStruct((bsz, seq, SWA_KV_COLS), BF16),
            jax.ShapeDtypeStruct((bsz, MLA_HEADS, seq, MLA_QK), BF16),
            jax.ShapeDtypeStruct((bsz, seq, MLA_QK), BF16),
        ],
        compiler_params=pltpu.CompilerParams(
            dimension_semantics=("parallel", "parallel"), vmem_limit_bytes=VMEM_LIMIT),
        name="mix_in",
    )(h, head, g, w1, gq, gkv, wuq, wuk, cos_t, sin_t, qx_t, kx_t)


def _split_bf16(x):
    hi = x.astype(BF16)
    lo = (x - hi.astype(F32)).astype(BF16)
    return hi, lo


def _bdot3(a, b):
    ah, al = _split_bf16(a)
    bh, bl = _split_bf16(b)
    dn = (((2,), (2,)), ((0,), (0,)))
    f = lambda x, y: lax.dot_general(x, y, dn, preferred_element_type=F32)
    return f(ah, bh) + f(ah, bl) + f(al, bh)


def _expand_groups(a2, row_shift, col_shift, n_cols):
    rows, w = a2.shape
    tile = (lax.broadcasted_iota(jnp.int32, (w, n_cols), 1) & (w - 1)) == lax.broadcasted_iota(
        jnp.int32, (w, n_cols), 0)
    wide = jnp.dot(a2.astype(BF16), jnp.where(tile, 1.0, 0.0).astype(BF16), preferred_element_type=F32)
    same = (lax.broadcasted_iota(jnp.int32, (rows, n_cols), 0) >> row_shift) == (
        lax.broadcasted_iota(jnp.int32, (rows, n_cols), 1) >> col_shift)
    return jnp.where(same, wide, 0.0)


def _s5_operator_kernel(ar_ref, ai_ref, ls_ref, btr_ref, bti_ref, cr_ref, ci_ref,
                        r_ref, p_ref, q_ref, l8r_ref, l8i_ref, pwr_sc, pwi_sc, kk_sc):
    s = pl.program_id(1)
    ar, ai = ar_ref[0], ai_ref[0]
    step = jnp.exp(ls_ref[0])
    lr = jnp.exp(ar * step) * jnp.cos(ai * step)
    li = jnp.exp(ar * step) * jnp.sin(ai * step)

    den = ar * ar + ai * ai
    nr, ni = lr - 1.0, li
    coef_re = (nr * ar + ni * ai) / den
    coef_im = (ni * ar - nr * ai) / den
    btr, bti = btr_ref[0], bti_ref[0]
    bb_re = coef_re * btr - coef_im * bti
    bb_im = coef_re * bti + coef_im * btr
    cr, ci = cr_ref[0], ci_ref[0]

    def c_lam(k):
        qr_, qi_ = pwr_sc[k], pwi_sc[k]
        return cr * qr_ - ci * qi_, cr * qi_ + ci * qr_

    gc = SSM_GROUPS * SSM_GROUP_CH

    @pl.when(s == 0)
    def _():
        pr, pi = jnp.ones_like(lr), jnp.zeros_like(li)
        for k in range(SSM_CHUNK + 1):
            pwr_sc[k] = pr
            pwi_sc[k] = pi
            pr, pi = pr * lr - pi * li, pr * li + pi * lr
        for k in range(SSM_CHUNK):
            clr, cli = c_lam(k)
            kk_sc[k] = (_bdot3(bb_re, clr) - _bdot3(bb_im, cli)).reshape(gc, SSM_GROUP_CH)

    for t in range(SSM_CHUNK):
        lag = t - s
        blk = _expand_groups(kk_sc[jnp.maximum(lag, 0)], 4, 4, gc)
        r_ref[0, :, t * gc:(t + 1) * gc] = jnp.where(lag >= 0, blk, 0.0).astype(BF16)

    pw_r, pw_i = pwr_sc[SSM_CHUNK - 1 - s], pwi_sc[SSM_CHUNK - 1 - s]
    p_re = (pw_r * bb_re - pw_i * bb_im).reshape(gc, SSM_STATE)
    p_im = (pw_r * bb_im + pw_i * bb_re).reshape(gc, SSM_STATE)
    p_ref[0, :, 0:SSM_MODES] = _expand_groups(p_re, 4, 6, SSM_MODES).astype(BF16)
    p_ref[0, :, SSM_MODES:2 * SSM_MODES] = _expand_groups(p_im, 4, 6, SSM_MODES).astype(BF16)

    clr, cli = c_lam(s + 1)
    q_ref[0, 0:SSM_MODES, :] = _expand_groups(clr.reshape(gc, SSM_STATE), 4, 6, SSM_MODES).T.astype(BF16)
    q_ref[0, SSM_MODES:2 * SSM_MODES, :] = (
        -_expand_groups(cli.reshape(gc, SSM_STATE), 4, 6, SSM_MODES)).T.astype(BF16)

    l8r_ref[0] = pwr_sc[SSM_CHUNK]
    l8i_ref[0] = pwi_sc[SSM_CHUNK]


def _s5_operators(a_re, a_im, log_step, b_re, b_im, c_re, c_im):
    depth = a_re.shape[0]
    G, P, C, T = SSM_GROUPS, SSM_STATE, SSM_GROUP_CH, SSM_CHUNK
    lay = lambda *shape: pl.BlockSpec((1,) + shape, lambda l, s: (l,) + (0,) * len(shape))
    ls = jnp.broadcast_to(log_step[:, :, None, None], (depth, G, 1, P))
    args = (a_re[:, :, None, :], a_im[:, :, None, :], ls,
            b_re.transpose(0, 1, 3, 2), b_im.transpose(0, 1, 3, 2), c_re, c_im)
    r_mat, p_mat, q_mat, l8r, l8i = pl.pallas_call(
        _s5_operator_kernel,
        grid=(depth, T),
        in_specs=[lay(G, 1, P)] * 3 + [lay(G, C, P)] * 4,
        out_specs=[
            pl.BlockSpec((1, SSM_WIDTH, SSM_FOLD), lambda l, s: (l, s, 0)),
            pl.BlockSpec((1, SSM_WIDTH, 2 * SSM_MODES), lambda l, s: (l, s, 0)),
            pl.BlockSpec((1, 2 * SSM_MODES, SSM_WIDTH), lambda l, s: (l, 0, s)),
            lay(G, 1, P), lay(G, 1, P)],
        out_shape=[
            jax.ShapeDtypeStruct((depth, SSM_FOLD, SSM_FOLD), BF16),
            jax.ShapeDtypeStruct((depth, SSM_FOLD, 2 * SSM_MODES), BF16),
            jax.ShapeDtypeStruct((depth, 2 * SSM_MODES, SSM_FOLD), BF16),
            jax.ShapeDtypeStruct((depth, G, 1, P), F32),
            jax.ShapeDtypeStruct((depth, G, 1, P), F32),
        ],
        scratch_shapes=[pltpu.VMEM((T + 1, G, 1, P), F32), pltpu.VMEM((T + 1, G, 1, P), F32),
                        pltpu.VMEM((T, G * C, C), F32)],
        compiler_params=pltpu.CompilerParams(dimension_semantics=("parallel", "arbitrary")),
        name="s5_operators",
    )(*args)
    lam = jnp.concatenate([l8r.reshape(depth, 1, SSM_MODES), l8i.reshape(depth, 1, SSM_MODES)], axis=2)
    return r_mat, p_mat, q_mat, lam


def _group_block_diag(x):
    *lead, G, r, c = x.shape
    eye = jnp.eye(G, dtype=x.dtype)
    y = x[..., :, :, None, :] * eye[:, None, :, None]
    return y.reshape(*lead, G * r, G * c)


def _fold_tokens(u_ref, n_chunks):
    return jnp.concatenate(
        [u_ref[0, c, pl.ds(s, n_chunks, stride=SSM_CHUNK), :] for s in range(SSM_CHUNK) for c in range(2)], axis=1)


def _s5_state_kernel(u_ref, p_ref, lam_ref, x_ref, s_sc, x_sc, *, n_chunks):
    u = _fold_tokens(u_ref, n_chunks).astype(BF16)
    s_sc[...] = jnp.dot(u, p_ref[0], preferred_element_type=F32)
    lr = lam_ref[0, :, 0:SSM_MODES]
    li = lam_ref[0, :, SSM_MODES:2 * SSM_MODES]

    def body(n, carry):
        xr, xi = carry
        x_sc[pl.ds(n, 1), 0:SSM_MODES] = xr
        x_sc[pl.ds(n, 1), SSM_MODES:2 * SSM_MODES] = xi
        sr = s_sc[pl.ds(n, 1), 0:SSM_MODES]
        si = s_sc[pl.ds(n, 1), SSM_MODES:2 * SSM_MODES]
        return lr * xr - li * xi + sr, lr * xi + li * xr + si

    z = jnp.zeros((1, SSM_MODES), F32)
    lax.fori_loop(0, n_chunks, body, (z, z))
    x_ref[0] = x_sc[...].astype(BF16)


def _s5_out_kernel(u_ref, x_ref, r_ref, q_ref, d_ref, y_ref, *, n_chunks):
    u = _fold_tokens(u_ref, n_chunks)
    ub = u.astype(BF16)
    x = x_ref[0]
    for t in range(SSM_CHUNK):
        cols = slice(t * SSM_WIDTH, (t + 1) * SSM_WIDTH)
        rows = (t + 1) * SSM_WIDTH
        y = jnp.dot(ub[:, 0:rows], r_ref[0, 0:rows, cols], preferred_element_type=F32)
        y += jnp.dot(x, q_ref[0, :, cols], preferred_element_type=F32)
        y += d_ref[:, cols] * u[:, cols]
        for c in range(2):
            y_ref[0, c, pl.ds(t, n_chunks, stride=SSM_CHUNK), :] = y[:, c * 128:(c + 1) * 128]


def _s5_sequence(u, r_mat, p_mat, q_mat, lam, d_fold, layer):
    bsz, _, seq, _ = u.shape
    n_chunks = seq // SSM_CHUNK
    params = pltpu.CompilerParams(dimension_semantics=("parallel",), vmem_limit_bytes=VMEM_LIMIT)
    slabs = pl.BlockSpec((1, 2, seq, 128), lambda b: (b, 0, 0, 0))
    states = pl.BlockSpec((1, n_chunks, 2 * SSM_MODES), lambda b: (b, 0, 0))
    x_prev = pl.pallas_call(
        functools.partial(_s5_state_kernel, n_chunks=n_chunks), grid=(bsz,),
        in_specs=[slabs, _layer_spec((SSM_FOLD, 2 * SSM_MODES), layer), _layer_spec((1, 2 * SSM_MODES), layer)],
        out_specs=states,
        out_shape=jax.ShapeDtypeStruct((bsz, n_chunks, 2 * SSM_MODES), BF16),
        scratch_shapes=[pltpu.VMEM((n_chunks, 2 * SSM_MODES), F32), pltpu.VMEM((n_chunks, 2 * SSM_MODES), F32)],
        compiler_params=params, name="s5_state",
    )(u, p_mat, lam)
    return pl.pallas_call(
        functools.partial(_s5_out_kernel, n_chunks=n_chunks), grid=(bsz,),
        in_specs=[slabs, states, _layer_spec((SSM_FOLD, SSM_FOLD), layer),
                  _layer_spec((2 * SSM_MODES, SSM_FOLD), layer), _const_spec((1, SSM_FOLD))],
        out_specs=slabs,
        out_shape=jax.ShapeDtypeStruct((bsz, 2, seq, 128), F32),
        compiler_params=params, name="s5_out",
    )(u, x_prev, r_mat, q_mat, d_fold)


def _swa_kernel(q_ref, kvp_ref, kvc_ref, bias_ref, sink_ref, o_ref, *, nsub):
    i = pl.program_id(1)
    nt_dims = (((1,), (1,)), ((), ()))
    tn_dims = (((0,), (0,)), ((), ()))
    chains = [(r, hk) for r in range(nsub) for hk in range(SWA_KV_HEADS)]
    scores, vcats = [], []
    for r, hk in chains:
        rows = slice(r * BLOCK, (r + 1) * BLOCK)
        q = q_ref[0, rows, :]
        kvc = kvc_ref[0, rows, :]
        kvp = kvp_ref[0] if r == 0 else kvc_ref[0, (r - 1) * BLOCK:r * BLOCK, :]
        ks = slice(hk * HEAD_DIM, (hk + 1) * HEAD_DIM)
        vs = slice(SWA_KV_WIDTH + hk * BLOCK, SWA_KV_WIDTH + (hk + 1) * BLOCK)
        kcat = jnp.concatenate([kvp[:, ks], kvc[:, ks]], axis=0)
        vcats.append(jnp.concatenate([kvp[:, vs], kvc[:, vs]], axis=0))
        qs = jnp.concatenate(
            [q[:, (hk * SWA_GROUP + g) * HEAD_DIM:(hk * SWA_GROUP + g + 1) * HEAD_DIM]
             for g in range(SWA_GROUP)], axis=0)
        table = jnp.minimum(i * nsub + r, 2)
        scores.append(lax.dot_general(kcat, qs, nt_dims, preferred_element_type=F32) + bias_ref[table, hk])
    weights, sink_terms = [], []
    for (r, hk), s in zip(chains, scores):
        sink = sink_ref[hk]
        m = jnp.maximum(jnp.max(s, axis=0, keepdims=True), sink)
        weights.append(jnp.exp2(s - m).astype(BF16))
        sink_terms.append(jnp.exp2(sink - m))
    o_ts = []
    for v, p, sink_term in zip(vcats, weights, sink_terms):
        o_t = lax.dot_general(v, p, tn_dims, preferred_element_type=F32)
        o_ts.append(o_t[0:HEAD_DIM, :] / (o_t[HEAD_DIM:HEAD_DIM + 1, :] + sink_term))
    for r in range(nsub):
        outs = []
        for hk in range(SWA_KV_HEADS):
            o_t = o_ts[r * SWA_KV_HEADS + hk]
            outs += [o_t[:, g * BLOCK:(g + 1) * BLOCK] for g in range(SWA_GROUP)]
        o_ref[0, r * BLOCK:(r + 1) * BLOCK, :] = jnp.concatenate(outs, axis=0).T


def _swa(q, kv, bias, sink_rows, *, nsub=11):
    bsz, seq, _ = q.shape
    tm = nsub * BLOCK
    cols = SWA_GROUP * BLOCK
    return pl.pallas_call(
        functools.partial(_swa_kernel, nsub=nsub),
        grid=(bsz, seq // tm),
        in_specs=[
            pl.BlockSpec((1, tm, SWA_WIDTH), lambda b, i: (b, i, 0)),
            pl.BlockSpec((1, BLOCK, SWA_KV_COLS), lambda b, i: (b, jnp.maximum(nsub * i - 1, 0), 0)),
            pl.BlockSpec((1, tm, SWA_KV_COLS), lambda b, i: (b, i, 0)),
            _const_spec((3, SWA_KV_HEADS, 2 * BLOCK, cols)),
            _const_spec((SWA_KV_HEADS, 1, cols)),
        ],
        out_specs=pl.BlockSpec((1, tm, SWA_WIDTH), lambda b, i: (b, i, 0)),
        out_shape=jax.ShapeDtypeStruct((bsz, seq, SWA_WIDTH), F32),
        compiler_params=pltpu.CompilerParams(
            dimension_semantics=("parallel", "parallel"), vmem_limit_bytes=VMEM_LIMIT),
        name="swa",
    )(q, kv, kv, bias, sink_rows)


def _mla_kernel(q_ref, qn_ref, kraw_ref, ktail_ref, wuv_ref, o_ref, k_sc, sa_sc, sb_sc, acc_sc, mx_sc, *, tq, rows):
    i = pl.program_id(1)
    cols = MLA_HEADS * tq
    nsub = tq // BLOCK
    seq = kraw_ref.shape[1]
    q = q_ref[0].reshape(cols, MLA_QK)
    n_groups = ((i + 1) * tq + rows - 1) // rows
    nt_dims = (((1,), (1,)), ((), ()))
    tn_dims = (((0,), (0,)), ((), ()))

    def fold(x, op):
        return op(x.reshape(x.shape[0] // 8, 8, x.shape[1]), axis=0)

    n_parts = rows // MLA_PART

    def scores_part(g, buf, c, qq=q):
        start = pl.multiple_of(g * rows + c * MLA_PART, MLA_PART)
        s = lax.dot_general(k_sc[pl.ds(start, MLA_PART), :], qq, nt_dims, preferred_element_type=F32)
        buf[c * MLA_PART:(c + 1) * MLA_PART, :] = s
        return fold(s, jnp.max)

    def rescale(mx8, m_run):
        m_new = jnp.maximum(m_run, jnp.max(mx8, axis=0, keepdims=True))
        acc_sc[...] = acc_sc[...] * jnp.exp2(m_run - m_new)
        return m_new

    def weights_part(g, buf, c, m_new):
        start = pl.multiple_of(g * rows + c * MLA_PART, MLA_PART)
        p = jnp.exp2(buf[c * MLA_PART:(c + 1) * MLA_PART, :] - m_new).astype(BF16)
        v = k_sc[pl.ds(start, MLA_PART), 0:MLA_ACC]
        acc_sc[...] += lax.dot_general(v, p, tn_dims, preferred_element_type=F32)

    def scores(g, buf):
        return functools.reduce(jnp.maximum, [scores_part(g, buf, c) for c in range(n_parts)])

    def weights(g, buf, mx8, m_run):
        m_new = rescale(mx8, m_run)
        for c in range(n_parts):
            weights_part(g, buf, c, m_new)
        return m_new

    def overlapped(g_next, buf_next, g, buf, mx8, m_run):
        m_new = rescale(mx8, m_run)
        parts = []
        for c in range(n_parts):
            parts.append(scores_part(g_next, buf_next, c))
            weights_part(g, buf, c, m_new)
        return functools.reduce(jnp.maximum, parts), m_new

    @pl.when(i == 0)
    def _():
        k_sc[0:seq, :] = kraw_ref[0]
        k_sc[seq:, :] = ktail_ref[...]
        mx_sc[...] = scores(0, sa_sc)

    acc_sc[...] = jnp.zeros((MLA_ACC, cols), F32)
    m0 = jnp.full((1, cols), NEG, F32)

    def pair(k, carry):
        mxa, m_run = carry
        mxb, m_run = overlapped(2 * k + 1, sb_sc, 2 * k, sa_sc, mxa, m_run)
        mxa, m_run = overlapped(2 * k + 2, sa_sc, 2 * k + 1, sb_sc, mxb, m_run)
        return mxa, m_run

    n_pairs = (n_groups - 1) // 2
    mxa, m_run = lax.fori_loop(0, n_pairs, pair, (mx_sc[...], m0))

    def one_left(m_run):
        return weights(n_groups - 1, sa_sc, mxa, m_run)

    def two_left(m_run):
        mxb, m_run = overlapped(n_groups - 1, sb_sc, n_groups - 2, sa_sc, mxa, m_run)
        return weights(n_groups - 1, sb_sc, mxb, m_run)

    m_run = lax.cond(n_groups - 1 == 2 * n_pairs, one_left, two_left, m_run)

    lane = lax.broadcasted_iota(jnp.int32, (1, MLA_QK), 1)
    onehot = (lane >= MLA_KV_RANK + 2) & (lane < MLA_KV_RANK + 2 + MLA_BLOCKS)
    hcols = MLA_HEADS * BLOCK
    causal = lax.broadcasted_iota(jnp.int32, (BLOCK, hcols), 0) <= (
        lax.broadcasted_iota(jnp.int32, (BLOCK, hcols), 1) & (BLOCK - 1))
    row0s = [pl.multiple_of((i * nsub + r) * BLOCK, BLOCK) for r in range(nsub)]
    lanes = [[slice(hd * tq + r * BLOCK, hd * tq + (r + 1) * BLOCK) for hd in range(MLA_HEADS)]
             for r in range(nsub)]
    qn = qn_ref[0].reshape(cols, MLA_QK)
    ahead = [scores_part(0, sa_sc, 0, qn)]
    sds = []
    for r in range(nsub):
        qd = jnp.concatenate([q[ln, :] for ln in lanes[r]], axis=0)
        qd = jnp.where(onehot, jnp.zeros((), BF16), qd)
        sd = lax.dot_general(k_sc[pl.ds(row0s[r], BLOCK), :], qd, nt_dims, preferred_element_type=F32)
        sds.append(jnp.where(causal, sd, NEG))
    ps, alphas = [], []
    for r in range(nsub):
        m_old = jnp.concatenate([m_run[:, ln] for ln in lanes[r]], axis=1)
        m_new = jnp.maximum(m_old, jnp.max(sds[r], axis=0, keepdims=True))
        ps.append(jnp.exp2(sds[r] - m_new).astype(BF16))
        alphas.append(jnp.exp2(m_old - m_new))
    ahead += [scores_part(0, sa_sc, c, qn) for c in range(1, n_parts)]
    mx_sc[...] = functools.reduce(jnp.maximum, ahead)
    pvs = [lax.dot_general(k_sc[pl.ds(row0s[r], BLOCK), 0:MLA_ACC], ps[r], tn_dims,
                           preferred_element_type=F32) for r in range(nsub)]
    o_parts = []
    for r in range(nsub):
        acc = jnp.concatenate([acc_sc[:, ln] for ln in lanes[r]], axis=1) * alphas[r] + pvs[r]
        o_parts.append((acc[0:MLA_KV_RANK, :] / acc[MLA_KV_RANK:MLA_KV_RANK + 1, :]).astype(BF16))
    ys = [[lax.dot_general(o_parts[r][:, hd * BLOCK:(hd + 1) * BLOCK], wuv_ref[hd], tn_dims,
                           preferred_element_type=F32) for hd in range(MLA_HEADS)] for r in range(nsub)]
    o_ref[0] = jnp.concatenate([functools.reduce(jnp.add, ys[r]) for r in range(nsub)], axis=0)


def _mla(qm, kc, key_tail, wuv_heads, *, tq=384, rows=MLA_KEY_ROWS):
    bsz, _, seq, _ = qm.shape
    seq_k = seq + key_tail.shape[0]
    cols = MLA_HEADS * tq
    n_blocks = seq // tq
    return pl.pallas_call(
        functools.partial(_mla_kernel, tq=tq, rows=rows),
        grid=(bsz, seq // tq),
        in_specs=[
            pl.BlockSpec((1, MLA_HEADS, tq, MLA_QK), lambda b, i: (b, 0, i, 0)),
            pl.BlockSpec((1, MLA_HEADS, tq, MLA_QK), lambda b, i: (b, 0, jnp.minimum(i + 1, n_blocks - 1), 0)),
            pl.BlockSpec((1, seq, MLA_QK), lambda b, i: (b, 0, 0)),
            _const_spec((seq_k - seq, MLA_QK)),
            _const_spec((MLA_HEADS, MLA_KV_RANK, MLA_WIDTH)),
        ],
        out_specs=pl.BlockSpec((1, tq, MLA_WIDTH), lambda b, i: (b, i, 0)),
        out_shape=jax.ShapeDtypeStruct((bsz, seq, MLA_WIDTH), F32),
        scratch_shapes=[pltpu.VMEM((seq_k, MLA_QK), BF16), pltpu.VMEM((rows, cols), F32),
                        pltpu.VMEM((rows, cols), F32), pltpu.VMEM((MLA_ACC, cols), F32), pltpu.VMEM((8, cols), F32)],
        compiler_params=pltpu.CompilerParams(
            dimension_semantics=("parallel", "arbitrary"), vmem_limit_bytes=VMEM_LIMIT),
        name="mla",
    )(qm, qm, kc, key_tail, wuv_heads)


def _gelu_tanh(x):
    return 0.5 * x * (1.0 + jnp.tanh(math.sqrt(2.0 / math.pi) * (x + 0.044715 * (x * x * x))))


def _mix_out_kernel(h_ref, head_ref, ys_ref, yw_ref, ym_ref, wglu_ref, bglu_ref, gh_ref, wout_ref, gpm_ref, gpre_ref,
                    wup_ref, wdn_ref, gpost_ref, o_ref, *, ff_chunk, from_x):
    z = _gelu_tanh(jnp.concatenate([ys_ref[0, 0], ys_ref[0, 1]], axis=1))
    gate = jnp.dot(z.astype(BF16), wglu_ref[0], preferred_element_type=F32) + bglu_ref[...]
    y_ssm = z * (1.0 / (1.0 + jnp.exp(-gate)))
    gh = gh_ref[...]
    mixed = jnp.concatenate([
        _rms(y_ssm, gh[:, 0:SSM_WIDTH]),
        _rms(yw_ref[0], gh[:, SSM_WIDTH:SSM_WIDTH + SWA_WIDTH]),
        _rms(ym_ref[0], gh[:, SSM_WIDTH + SWA_WIDTH:]),
    ], axis=1).astype(BF16)
    mo = jnp.dot(mixed, wout_ref[0], preferred_element_type=F32)
    h_rows = _seq_rows(h_ref, head_ref, pl.program_id(1), 0, o_ref.shape[1]) if from_x else h_ref[0]
    h1 = h_rows + _rms(mo, gpm_ref[...])
    a = _rms(h1, gpre_ref[...]).astype(BF16)
    f = jnp.zeros_like(h1)
    for c in range(D_FF // ff_chunk):
        up = jnp.dot(a, wup_ref[0, :, c * ff_chunk:(c + 1) * ff_chunk], preferred_element_type=F32)
        hid = jnp.square(jnp.maximum(up, 0.0)).astype(BF16)
        f += jnp.dot(hid, wdn_ref[0, c * ff_chunk:(c + 1) * ff_chunk, :], preferred_element_type=F32)
    o_ref[0] = h1 + _rms(f, gpost_ref[...])


def _mix_out(h, head, ys, yw, ym, wglu, bglu, gh, wout, gpm, gpre, wup, wdn, gpost, layer, *, from_x=False, skip=0,
             tm=1056, ff_chunk=1024):
    assert not (from_x and skip)
    bsz, seq = yw.shape[0], yw.shape[1]
    out_rows = seq - skip
    if skip:
        el = pl.Element
        first = lambda i: pl.multiple_of(skip + i * tm, BLOCK)
        row = lambda w: pl.BlockSpec((el(1), el(tm), el(w)), lambda b, i: (b, first(i), 0))
        slabs = pl.BlockSpec((el(1), el(2), el(tm), el(128)), lambda b, i: (b, 0, first(i), 0))
    else:
        row = lambda w: pl.BlockSpec((1, tm, w), lambda b, i: (b, i, 0))
        slabs = pl.BlockSpec((1, 2, tm, 128), lambda b, i: (b, 0, i, 0))
    return pl.pallas_call(
        functools.partial(_mix_out_kernel, ff_chunk=ff_chunk, from_x=from_x),
        grid=(bsz, out_rows // tm),
        in_specs=[
            _seq_window_spec(tm, D_MODEL) if from_x else row(D_MODEL), _const_spec((BLOCK, D_MODEL)),
            slabs, row(SWA_WIDTH), row(MLA_WIDTH),
            _layer_spec((SSM_WIDTH, SSM_WIDTH), layer), _const_spec((1, SSM_WIDTH)), _const_spec((1, D_MODEL)),
            _layer_spec((D_MODEL, D_MODEL), layer), _const_spec((1, D_MODEL)), _const_spec((1, D_MODEL)),
            _layer_spec((D_MODEL, D_FF), layer), _layer_spec((D_FF, D_MODEL), layer), _const_spec((1, D_MODEL)),
        ],
        out_specs=pl.BlockSpec((1, tm, D_MODEL), lambda b, i: (b, i, 0)),
        out_shape=jax.ShapeDtypeStruct((bsz, out_rows, D_MODEL), F32),
        compiler_params=pltpu.CompilerParams(
            dimension_semantics=("parallel", "parallel"), vmem_limit_bytes=VMEM_LIMIT),
        name="mix_out",
    )(h, head, ys, yw, ym, wglu, bglu, gh, wout, gpm, gpre, wup, wdn, gpost)


def _rot_half_cols(w):
    half = w.shape[-1] // 2
    return jnp.concatenate([-w[..., half:], w[..., :half]], axis=-1)


def _layer_weights(w_in, w_uq, w_uk, w_uv):
    z = lambda n: jnp.zeros((D_MODEL, n), w_in.dtype)
    s_cq = SSM_WIDTH + SWA_WIDTH + 2 * SWA_KV_WIDTH
    s_ckv = s_cq + MLA_Q_RANK
    s_kr = s_ckv + MLA_KV_RANK
    kr = w_in[:, s_kr:s_kr + MLA_ROPE]
    w1 = jnp.concatenate([
        w_in[:, :s_cq], w_in[:, s_cq:s_ckv], z(C_CKV - C_CQ - MLA_Q_RANK),
        w_in[:, s_ckv:s_kr], kr, _rot_half_cols(kr), z(C_END - C_KR - 2 * MLA_ROPE)], axis=1).astype(BF16)
    uq = w_uq.reshape(MLA_Q_RANK, MLA_HEADS, MLA_NOPE + MLA_ROPE)
    rope = uq[:, :, MLA_NOPE:]
    wuq = jnp.concatenate([
        uq[:, :, :MLA_NOPE].reshape(MLA_Q_RANK, -1), rope.reshape(MLA_Q_RANK, -1),
        _rot_half_cols(rope).reshape(MLA_Q_RANK, -1)], axis=1)
    wuq = jnp.concatenate([wuq, jnp.zeros((256 - MLA_Q_RANK, wuq.shape[1]), wuq.dtype)], axis=0).astype(BF16)
    uk = w_uk.reshape(MLA_KV_RANK, MLA_HEADS, MLA_NOPE).transpose(1, 2, 0)
    wuk_bd = _group_block_diag(uk).astype(BF16)
    uv = w_uv.reshape(MLA_KV_RANK, MLA_HEADS, MLA_V).transpose(1, 0, 2)
    wuv_heads = _group_block_diag(uv).reshape(MLA_HEADS, MLA_KV_RANK, MLA_WIDTH).astype(BF16)
    return w1, wuq, wuk_bd, wuv_heads


def kernel(x, meta_tokens, norm_pre_mix, norm_post_mix, norm_pre_mlp, norm_post_mlp, w_in, w_out, norm_heads,
           ssm_a_re, ssm_a_im, ssm_log_step, ssm_b_re, ssm_b_im, ssm_c_re, ssm_c_im, ssm_d, ssm_w_glu,
           ssm_b_glu, swa_sinks, mla_q_norm, mla_kv_norm, mla_w_uq, mla_w_uk, mla_w_uv, w_mlp_up, w_mlp_down):
    bsz, seq_real, d = x.shape
    depth = w_in.shape[0]
    head = jnp.concatenate([jnp.zeros((PAD, d), x.dtype), meta_tokens.astype(x.dtype)], axis=0)
    seq = BLOCK + seq_real
    h = x

    half = MLA_ROPE // 2
    inv_freq = ROPE_THETA ** (-jnp.arange(half, dtype=F32) / half)
    ang = (jnp.arange(seq) - PAD).astype(F32)[:, None] * inv_freq[None, :]
    cos_t = jnp.tile(jnp.cos(ang), (1, 2 * MLA_HEADS))
    sin_t = jnp.tile(jnp.sin(ang), (1, 2 * MLA_HEADS))
    seq_k = -(-seq // MLA_KEY_ROWS) * MLA_KEY_ROWS
    kblk = jnp.arange(seq_k)[:, None] // BLOCK
    bcol = jnp.arange(MLA_BLOCKS)[None, :]
    one = jnp.ones((seq_k, 1), F32)
    fill = jnp.zeros((seq_k, 128 - 2 - MLA_BLOCKS), F32)
    kx_t = jnp.concatenate([one, jnp.where(jnp.arange(seq_k)[:, None] < PAD, NEG, 0.0),
                            (kblk == bcol).astype(F32), fill], axis=1).astype(BF16)
    qx_t = jnp.concatenate([0.0 * one, one, jnp.where(bcol >= kblk, NEG, 0.0), fill], axis=1).astype(BF16)[:seq]
    key_tail = jnp.zeros((seq_k - seq, MLA_QK), BF16).at[:, MLA_KV_RANK:MLA_KV_RANK + MLA_EXT].set(
        kx_t[seq:, :MLA_EXT])
    slopes = 2.0 ** (-8.0 * jnp.arange(1, SWA_HEADS + 1, dtype=F32) / SWA_HEADS)
    per_row = lambda v: jnp.repeat(v.reshape(SWA_KV_HEADS, SWA_GROUP), BLOCK, axis=1)[:, None, :]
    dist = (jnp.arange(BLOCK)[None, :] + BLOCK) - jnp.arange(2 * BLOCK)[:, None]
    in_window = jnp.tile((dist >= 0) & (dist < BLOCK), (1, SWA_GROUP))[None, None]
    alibi = (-LOG2E * per_row(slopes) * jnp.tile(dist.astype(F32), (1, SWA_GROUP))[None])[None]
    key_pos = (jnp.arange(3)[:, None] - 1) * BLOCK + jnp.arange(2 * BLOCK)[None, :]
    key_ok = ((key_pos >= PAD) | (jnp.arange(3)[:, None] >= 2))[:, None, :, None]
    swa_bias = jnp.where(in_window & key_ok, alibi, NEG)

    r_mat, p_mat, q_mat, lam = _s5_operators(
        ssm_a_re, ssm_a_im, ssm_log_step, ssm_b_re, ssm_b_im, ssm_c_re, ssm_c_im)
    row = lambda v: v.reshape(1, -1).astype(F32)
    wglu, wout, wup, wdn = (w.astype(BF16) for w in (ssm_w_glu, w_out, w_mlp_up, w_mlp_down))

    for l in range(depth):
        w1, wuq, wuk_bd, wuv_heads = _layer_weights(w_in[l], mla_w_uq[l], mla_w_uk[l], mla_w_uv[l])
        gq = jnp.concatenate([mla_q_norm[l], jnp.zeros((256 - MLA_Q_RANK,), F32)]).reshape(1, 256)
        u, q, kv, qm, kc = _mix_in(h, head, row(norm_pre_mix[l]), w1, gq, row(mla_kv_norm[l]), wuq, wuk_bd,
                                   cos_t, sin_t, qx_t, kx_t[:seq], from_x=l == 0)
        d_fold = jnp.tile(ssm_d[l].astype(F32), SSM_CHUNK).reshape(1, SSM_FOLD)
        y_ssm = _s5_sequence(u, r_mat, p_mat, q_mat, lam, d_fold, l)
        y_swa = _swa(q, kv, swa_bias, per_row(LOG2E * swa_sinks[l].astype(F32)))
        y_mla = _mla(qm, kc, key_tail, wuv_heads)
        last = l == depth - 1
        h = _mix_out(
            h, head, y_ssm, y_swa, y_mla,
            wglu, row(ssm_b_glu[l]), row(norm_heads[l]), wout, row(norm_post_mix[l]), row(norm_pre_mlp[l]),
            wup, wdn, row(norm_post_mlp[l]), l, from_x=l == 0, skip=PAD + N_META if last else 0,
            tm=1024 if last else 1056)
    return h
```

```python
import functools
import math

import jax
import jax.numpy as jnp
import numpy as np
from jax import lax
from jax.experimental import pallas as pl
from jax.experimental.pallas import tpu as pltpu

F32 = jnp.float32
BF16 = jnp.bfloat16

D_MODEL = 1024
N_META = 16
BLOCK = 128
PAD = BLOCK - N_META
SSM_GROUPS = 16
SSM_GROUP_CH = 16
SSM_WIDTH = SSM_GROUPS * SSM_GROUP_CH
SSM_STATE = 64
SWA_HEADS = 8
SWA_KV_HEADS = 2
SWA_GROUP = SWA_HEADS // SWA_KV_HEADS
HEAD_DIM = 64
SWA_WIDTH = SWA_HEADS * HEAD_DIM
SWA_KV_WIDTH = SWA_KV_HEADS * HEAD_DIM
SWA_KV_COLS = SWA_KV_WIDTH + SWA_KV_HEADS * BLOCK
MLA_HEADS = 4
MLA_Q_RANK = 192
MLA_KV_RANK = 128
MLA_NOPE = 64
MLA_ROPE = 32
MLA_V = 64
MLA_WIDTH = MLA_HEADS * MLA_V
MLA_BLOCKS = 40
MLA_EXT = 48
MLA_ACC = MLA_KV_RANK + 8
LOG2E = 1.4426950408889634
MLA_QK = MLA_KV_RANK + MLA_EXT + MLA_ROPE
ROPE_THETA = 10000.0
D_FF = 4 * D_MODEL
EPS = 1e-6
NEG = -1e30

SSM_CHUNK = 8
SSM_FOLD = SSM_CHUNK * SSM_WIDTH
SSM_MODES = SSM_GROUPS * SSM_STATE

C_U, C_Q, C_K, C_CQ, C_CKV, C_KR, C_END = 0, 256, 768, 1024, 1280, 1408, 1536

MLA_KEY_ROWS = 512
MLA_PART = 256

VMEM_LIMIT = 58 * 1024 * 1024


def _rms(x, g, n=None):
    n = x.shape[-1] if n is None else n
    ms = jnp.sum(x * x, axis=-1, keepdims=True) * (1.0 / n)
    return x * lax.rsqrt(ms + EPS) * g


def _const_spec(shape):
    nd = len(shape)
    return pl.BlockSpec(shape, lambda *_: (0,) * nd, pipeline_mode=pl.Buffered(1))


def _seq_window_spec(tm, width):
    el = pl.Element
    return pl.BlockSpec((el(1), el(tm), el(width)),
                        lambda b, i: (b, pl.multiple_of(jnp.maximum(i * tm - BLOCK, 0), 8), 0))


def _seq_rows(x_ref, head_ref, i, lo, n):
    first = jnp.where(i == 0, 1, 0)
    if lo == 0:
        top = jnp.where(i == 0, head_ref[...], x_ref[0, 0:BLOCK, :])
        rest = x_ref[0, pl.ds(pl.multiple_of(BLOCK - BLOCK * first, 8), n - BLOCK), :]
        return jnp.concatenate([top, rest], axis=0)
    return x_ref[0, pl.ds(pl.multiple_of(lo - BLOCK * first, 8), n), :]


def _layer_spec(shape, layer):
    nd = len(shape)
    return pl.BlockSpec((1,) + shape, lambda *_: (layer,) + (0,) * nd, pipeline_mode=pl.Buffered(1))


def _mix_in_kernel(h_ref, head_ref, g_ref, w1_ref, gq_ref, gkv_ref, wuq_ref, wuk_ref, cos_ref, sin_ref, qx_ref,
                   kx_ref, u_ref, q_ref, kv_ref, qm_ref, kc_ref, *, tm, nsub, from_x):
    j = pl.program_id(1)
    ts = tm // nsub
    scale = LOG2E * (MLA_NOPE + MLA_ROPE) ** -0.5
    ext = slice(MLA_KV_RANK, MLA_KV_RANK + MLA_EXT)
    rot = slice(MLA_KV_RANK + MLA_EXT, MLA_QK)
    for sub in range(nsub):
        rows = slice(sub * ts, (sub + 1) * ts)
        h_rows = _seq_rows(h_ref, head_ref, j, sub * ts, ts) if from_x else h_ref[0, rows, :]
        a = _rms(h_rows, g_ref[...]).astype(BF16)
        proj = jnp.dot(a, w1_ref[0], preferred_element_type=F32)
        row = j * tm + sub * ts + lax.broadcasted_iota(jnp.int32, (ts, 1), 0)
        for c in range(2):
            u_ref[0, c, rows, :] = jnp.where(row >= PAD, proj[:, C_U + c * 128:C_U + (c + 1) * 128], 0.0)
        q_ref[0, rows, :] = (proj[:, C_Q:C_K] * (LOG2E * HEAD_DIM ** -0.5)).astype(BF16)
        kv_ref[0, rows, 0:SWA_KV_WIDTH] = proj[:, C_K:C_K + SWA_KV_WIDTH].astype(BF16)
        ones_col = jnp.where(lax.broadcasted_iota(jnp.int32, (ts, HEAD_DIM), 1) == 0, 1.0, 0.0).astype(BF16)
        for hk in range(SWA_KV_HEADS):
            lo = SWA_KV_WIDTH + hk * BLOCK
            v_lo = C_K + SWA_KV_WIDTH + hk * HEAD_DIM
            kv_ref[0, rows, lo:lo + HEAD_DIM] = proj[:, v_lo:v_lo + HEAD_DIM].astype(BF16)
            kv_ref[0, rows, lo + HEAD_DIM:lo + BLOCK] = ones_col

        cos = cos_ref[rows, :]
        sin = sin_ref[rows, :]
        qn = _rms(proj[:, C_CQ:C_CKV], gq_ref[...], n=MLA_Q_RANK).astype(BF16)
        q2 = jnp.dot(qn, wuq_ref[0], preferred_element_type=F32)
        qr = (q2[:, 256:384] * cos + q2[:, 384:512] * sin) * scale
        qa = jnp.dot(q2[:, 0:256].astype(BF16), wuk_ref[0], preferred_element_type=F32) * scale
        c = _rms(proj[:, C_CKV:C_KR], gkv_ref[...])
        krr = proj[:, C_KR:C_END]
        kr = krr[:, 0:MLA_ROPE] * cos[:, 0:MLA_ROPE] + krr[:, MLA_ROPE:2 * MLA_ROPE] * sin[:, 0:MLA_ROPE]
        for hd in range(MLA_HEADS):
            qm_ref[0, hd, rows, 0:MLA_KV_RANK] = qa[:, hd * MLA_KV_RANK:(hd + 1) * MLA_KV_RANK].astype(BF16)
            qm_ref[0, hd, rows, ext] = qx_ref[rows, 0:MLA_EXT]
            qm_ref[0, hd, rows, rot] = qr[:, hd * MLA_ROPE:(hd + 1) * MLA_ROPE].astype(BF16)
        kc_ref[0, rows, 0:MLA_KV_RANK] = c.astype(BF16)
        kc_ref[0, rows, ext] = kx_ref[rows, 0:MLA_EXT]
        kc_ref[0, rows, rot] = kr.astype(BF16)


def _mix_in(h, head, g, w1, gq, gkv, wuq, wuk, cos_t, sin_t, qx_t, kx_t, layer, *, from_x, tm=2112, nsub=4):
    bsz = h.shape[0]
    seq = cos_t.shape[0]
    grid = (bsz, seq // tm)
    row3 = lambda b, j: (b, j, 0)
    return pl.pallas_call(
        functools.partial(_mix_in_kernel, tm=tm, nsub=nsub, from_x=from_x),
        grid=grid,
        in_specs=[
            _seq_window_spec(tm, D_MODEL) if from_x else pl.BlockSpec((1, tm, D_MODEL), row3),
            _const_spec((BLOCK, D_MODEL)),
            _const_spec((1, D_MODEL)),
            _layer_spec((D_MODEL, C_END), layer),
            _const_spec((1, 256)),
            _const_spec((1, MLA_KV_RANK)),
            _layer_spec((256, 512), layer),
            _layer_spec((256, 512), layer),
            pl.BlockSpec((tm, 128), lambda b, j: (j, 0)),
            pl.BlockSpec((tm, 128), lambda b, j: (j, 0)),
            pl.BlockSpec((tm, 128), lambda b, j: (j, 0)),
            pl.BlockSpec((tm, 128), lambda b, j: (j, 0)),
        ],
        out_specs=[
            pl.BlockSpec((1, 2, tm, 128), lambda b, j: (b, 0, j, 0)),
            pl.BlockSpec((1, tm, SWA_WIDTH), row3),
            pl.BlockSpec((1, tm, SWA_KV_COLS), row3),
            pl.BlockSpec((1, MLA_HEADS, tm, MLA_QK), lambda b, j: (b, 0, j, 0)),
            pl.BlockSpec((1, tm, MLA_QK), row3),
        ],
        out_shape=[
            jax.ShapeDtypeStruct((bsz, 2, seq, 128), F32),
            jax.ShapeDtypeStruct((bsz, seq, SWA_WIDTH), BF16),
            jax.ShapeDtypeStruct((bsz, seq, SWA_KV_COLS), BF16),
            jax.ShapeDtypeStruct((bsz, MLA_HEADS, seq, MLA_QK), BF16),
            jax.ShapeDtypeStruct((bsz, seq, MLA_QK), BF16),
        ],
        compiler_params=pltpu.CompilerParams(
            dimension_semantics=("parallel", "parallel"), vmem_limit_bytes=VMEM_LIMIT),
        name="mix_in",
    )(h, head, g, w1, gq, gkv, wuq, wuk, cos_t, sin_t, qx_t, kx_t)


def _split_bf16(x):
    hi = x.astype(BF16)
    lo = (x - hi.astype(F32)).astype(BF16)
    return hi, lo


def _bdot3(a, b):
    ah, al = _split_bf16(a)
    bh, bl = _split_bf16(b)
    dn = (((2,), (2,)), ((0,), (0,)))
    f = lambda x, y: lax.dot_general(x, y, dn, preferred_element_type=F32)
    return f(ah, bh) + f(ah, bl) + f(al, bh)


def _expand_groups(a2, row_shift, col_shift, n_cols):
    rows, w = a2.shape
    tile = (lax.broadcasted_iota(jnp.int32, (w, n_cols), 1) & (w - 1)) == lax.broadcasted_iota(
        jnp.int32, (w, n_cols), 0)
    wide = jnp.dot(a2.astype(BF16), jnp.where(tile, 1.0, 0.0).astype(BF16), preferred_element_type=F32)
    same = (lax.broadcasted_iota(jnp.int32, (rows, n_cols), 0) >> row_shift) == (
        lax.broadcasted_iota(jnp.int32, (rows, n_cols), 1) >> col_shift)
    return jnp.where(same, wide, 0.0)


def _s5_operator_kernel(ar_ref, ai_ref, ls_ref, btr_ref, bti_ref, cr_ref, ci_ref,
                        r_ref, p_ref, q_ref, l8r_ref, l8i_ref, pwr_sc, pwi_sc, kk_sc):
    s = pl.program_id(1)
    ar, ai = ar_ref[0], ai_ref[0]
    step = jnp.exp(ls_ref[0])
    lr = jnp.exp(ar * step) * jnp.cos(ai * step)
    li = jnp.exp(ar * step) * jnp.sin(ai * step)

    den = ar * ar + ai * ai
    nr, ni = lr - 1.0, li
    coef_re = (nr * ar + ni * ai) / den
    coef_im = (ni * ar - nr * ai) / den
    btr, bti = btr_ref[0], bti_ref[0]
    bb_re = coef_re * btr - coef_im * bti
    bb_im = coef_re * bti + coef_im * btr
    cr, ci = cr_ref[0], ci_ref[0]

    def c_lam(k):
        qr_, qi_ = pwr_sc[k], pwi_sc[k]
        return cr * qr_ - ci * qi_, cr * qi_ + ci * qr_

    gc = SSM_GROUPS * SSM_GROUP_CH

    @pl.when(s == 0)
    def _():
        pr, pi = jnp.ones_like(lr), jnp.zeros_like(li)
        for k in range(SSM_CHUNK + 1):
            pwr_sc[k] = pr
            pwi_sc[k] = pi
            pr, pi = pr * lr - pi * li, pr * li + pi * lr
        for k in range(SSM_CHUNK):
            clr, cli = c_lam(k)
            kk_sc[k] = (_bdot3(bb_re, clr) - _bdot3(bb_im, cli)).reshape(gc, SSM_GROUP_CH)

    for t in range(SSM_CHUNK):
        lag = t - s
        blk = _expand_groups(kk_sc[jnp.maximum(lag, 0)], 4, 4, gc)
        r_ref[0, :, t * gc:(t + 1) * gc] = jnp.where(lag >= 0, blk, 0.0).astype(BF16)

    pw_r, pw_i = pwr_sc[SSM_CHUNK - 1 - s], pwi_sc[SSM_CHUNK - 1 - s]
    p_re = (pw_r * bb_re - pw_i * bb_im).reshape(gc, SSM_STATE)
    p_im = (pw_r * bb_im + pw_i * bb_re).reshape(gc, SSM_STATE)
    p_ref[0, :, 0:SSM_MODES] = _expand_groups(p_re, 4, 6, SSM_MODES).astype(BF16)
    p_ref[0, :, SSM_MODES:2 * SSM_MODES] = _expand_groups(p_im, 4, 6, SSM_MODES).astype(BF16)

    clr, cli = c_lam(s + 1)
    q_ref[0, 0:SSM_MODES, :] = _expand_groups(clr.reshape(gc, SSM_STATE), 4, 6, SSM_MODES).T.astype(BF16)
    q_ref[0, SSM_MODES:2 * SSM_MODES, :] = (
        -_expand_groups(cli.reshape(gc, SSM_STATE), 4, 6, SSM_MODES)).T.astype(BF16)

    l8r_ref[0] = pwr_sc[SSM_CHUNK]
    l8i_ref[0] = pwi_sc[SSM_CHUNK]


def _s5_operators(a_re, a_im, log_step, b_re, b_im, c_re, c_im):
    depth = a_re.shape[0]
    G, P, C, T = SSM_GROUPS, SSM_STATE, SSM_GROUP_CH, SSM_CHUNK
    lay = lambda *shape: pl.BlockSpec((1,) + shape, lambda l, s: (l,) + (0,) * len(shape))
    ls = jnp.broadcast_to(log_step[:, :, None, None], (depth, G, 1, P))
    args = (a_re[:, :, None, :], a_im[:, :, None, :], ls,
            b_re.transpose(0, 1, 3, 2), b_im.transpose(0, 1, 3, 2), c_re, c_im)
    r_mat, p_mat, q_mat, l8r, l8i = pl.pallas_call(
        _s5_operator_kernel,
        grid=(depth, T),
        in_specs=[lay(G, 1, P)] * 3 + [lay(G, C, P)] * 4,
        out_specs=[
            pl.BlockSpec((1, SSM_WIDTH, SSM_FOLD), lambda l, s: (l, s, 0)),
            pl.BlockSpec((1, SSM_WIDTH, 2 * SSM_MODES), lambda l, s: (l, s, 0)),
            pl.BlockSpec((1, 2 * SSM_MODES, SSM_WIDTH), lambda l, s: (l, 0, s)),
            lay(G, 1, P), lay(G, 1, P)],
        out_shape=[
            jax.ShapeDtypeStruct((depth, SSM_FOLD, SSM_FOLD), BF16),
            jax.ShapeDtypeStruct((depth, SSM_FOLD, 2 * SSM_MODES), BF16),
            jax.ShapeDtypeStruct((depth, 2 * SSM_MODES, SSM_FOLD), BF16),
            jax.ShapeDtypeStruct((depth, G, 1, P), F32),
            jax.ShapeDtypeStruct((depth, G, 1, P), F32),
        ],
        scratch_shapes=[pltpu.VMEM((T + 1, G, 1, P), F32), pltpu.VMEM((T + 1, G, 1, P), F32),
                        pltpu.VMEM((T, G * C, C), F32)],
        compiler_params=pltpu.CompilerParams(dimension_semantics=("parallel", "arbitrary")),
        name="s5_operators",
    )(*args)
    lam = jnp.concatenate([l8r.reshape(depth, 1, SSM_MODES), l8i.reshape(depth, 1, SSM_MODES)], axis=2)
    return r_mat, p_mat, q_mat, lam


def _group_block_diag(x):
    *lead, G, r, c = x.shape
    eye = jnp.eye(G, dtype=x.dtype)
    y = x[..., :, :, None, :] * eye[:, None, :, None]
    return y.reshape(*lead, G * r, G * c)


def _fold_tokens(u_ref, n_chunks):
    return jnp.concatenate(
        [u_ref[0, c, pl.ds(s, n_chunks, stride=SSM_CHUNK), :] for s in range(SSM_CHUNK) for c in range(2)], axis=1)


def _s5_state_kernel(u_ref, p_ref, lam_ref, x_ref, s_sc, x_sc, *, n_chunks):
    u = _fold_tokens(u_ref, n_chunks).astype(BF16)
    s_sc[...] = jnp.dot(u, p_ref[0], preferred_element_type=F32)
    lr = lam_ref[0, :, 0:SSM_MODES]
    li = lam_ref[0, :, SSM_MODES:2 * SSM_MODES]

    def body(n, carry):
        xr, xi = carry
        x_sc[pl.ds(n, 1), 0:SSM_MODES] = xr
        x_sc[pl.ds(n, 1), SSM_MODES:2 * SSM_MODES] = xi
        sr = s_sc[pl.ds(n, 1), 0:SSM_MODES]
        si = s_sc[pl.ds(n, 1), SSM_MODES:2 * SSM_MODES]
        return lr * xr - li * xi + sr, lr * xi + li * xr + si

    z = jnp.zeros((1, SSM_MODES), F32)
    lax.fori_loop(0, n_chunks, body, (z, z))
    x_ref[0] = x_sc[...].astype(BF16)


def _s5_out_kernel(u_ref, x_ref, r_ref, q_ref, d_ref, y_ref, *, n_chunks):
    u = _fold_tokens(u_ref, n_chunks)
    ub = u.astype(BF16)
    x = x_ref[0]
    for t in range(SSM_CHUNK):
        cols = slice(t * SSM_WIDTH, (t + 1) * SSM_WIDTH)
        rows = (t + 1) * SSM_WIDTH
        y = jnp.dot(ub[:, 0:rows], r_ref[0, 0:rows, cols], preferred_element_type=F32)
        y += jnp.dot(x, q_ref[0, :, cols], preferred_element_type=F32)
        y += d_ref[:, cols] * u[:, cols]
        for c in range(2):
            y_ref[0, c, pl.ds(t, n_chunks, stride=SSM_CHUNK), :] = y[:, c * 128:(c + 1) * 128]


def _s5_sequence(u, r_mat, p_mat, q_mat, lam, d_fold, layer):
    bsz, _, seq, _ = u.shape
    n_chunks = seq // SSM_CHUNK
    params = pltpu.CompilerParams(dimension_semantics=("parallel",), vmem_limit_bytes=VMEM_LIMIT)
    slabs = pl.BlockSpec((1, 2, seq, 128), lambda b: (b, 0, 0, 0))
    states = pl.BlockSpec((1, n_chunks, 2 * SSM_MODES), lambda b: (b, 0, 0))
    x_prev = pl.pallas_call(
        functools.partial(_s5_state_kernel, n_chunks=n_chunks), grid=(bsz,),
        in_specs=[slabs, _layer_spec((SSM_FOLD, 2 * SSM_MODES), layer), _layer_spec((1, 2 * SSM_MODES), layer)],
        out_specs=states,
        out_shape=jax.ShapeDtypeStruct((bsz, n_chunks, 2 * SSM_MODES), BF16),
        scratch_shapes=[pltpu.VMEM((n_chunks, 2 * SSM_MODES), F32), pltpu.VMEM((n_chunks, 2 * SSM_MODES), F32)],
        compiler_params=params, name="s5_state",
    )(u, p_mat, lam)
    return pl.pallas_call(
        functools.partial(_s5_out_kernel, n_chunks=n_chunks), grid=(bsz,),
        in_specs=[slabs, states, _layer_spec((SSM_FOLD, SSM_FOLD), layer),
                  _layer_spec((2 * SSM_MODES, SSM_FOLD), layer), _const_spec((1, SSM_FOLD))],
        out_specs=slabs,
        out_shape=jax.ShapeDtypeStruct((bsz, 2, seq, 128), F32),
        compiler_params=params, name="s5_out",
    )(u, x_prev, r_mat, q_mat, d_fold)


def _swa_kernel(q_ref, kvp_ref, kvc_ref, bias_ref, sink_ref, o_ref, *, nsub):
    i = pl.program_id(1)
    nt_dims = (((1,), (1,)), ((), ()))
    tn_dims = (((0,), (0,)), ((), ()))
    chains = [(r, hk) for r in range(nsub) for hk in range(SWA_KV_HEADS)]
    scores, vcats = [], []
    for r, hk in chains:
        rows = slice(r * BLOCK, (r + 1) * BLOCK)
        q = q_ref[0, rows, :]
        kvc = kvc_ref[0, rows, :]
        kvp = kvp_ref[0] if r == 0 else kvc_ref[0, (r - 1) * BLOCK:r * BLOCK, :]
        ks = slice(hk * HEAD_DIM, (hk + 1) * HEAD_DIM)
        vs = slice(SWA_KV_WIDTH + hk * BLOCK, SWA_KV_WIDTH + (hk + 1) * BLOCK)
        kcat = jnp.concatenate([kvp[:, ks], kvc[:, ks]], axis=0)
        vcats.append(jnp.concatenate([kvp[:, vs], kvc[:, vs]], axis=0))
        qs = jnp.concatenate(
            [q[:, (hk * SWA_GROUP + g) * HEAD_DIM:(hk * SWA_GROUP + g + 1) * HEAD_DIM]
             for g in range(SWA_GROUP)], axis=0)
        table = jnp.minimum(i * nsub + r, 2)
        scores.append(lax.dot_general(kcat, qs, nt_dims, preferred_element_type=F32) + bias_ref[table, hk])
    weights, sink_terms = [], []
    for (r, hk), s in zip(chains, scores):
        sink = sink_ref[hk]
        m = jnp.maximum(jnp.max(s, axis=0, keepdims=True), sink)
        weights.append(jnp.exp2(s - m).astype(BF16))
        sink_terms.append(jnp.exp2(sink - m))
    o_ts = []
    for v, p, sink_term in zip(vcats, weights, sink_terms):
        o_t = lax.dot_general(v, p, tn_dims, preferred_element_type=F32)
        o_ts.append(o_t[0:HEAD_DIM, :] / (o_t[HEAD_DIM:HEAD_DIM + 1, :] + sink_term))
    for r in range(nsub):
        outs = []
        for hk in range(SWA_KV_HEADS):
            o_t = o_ts[r * SWA_KV_HEADS + hk]
            outs += [o_t[:, g * BLOCK:(g + 1) * BLOCK] for g in range(SWA_GROUP)]
        o_ref[0, r * BLOCK:(r + 1) * BLOCK, :] = jnp.concatenate(outs, axis=0).T


def _swa(q, kv, bias, sink_rows, *, nsub=11):
    bsz, seq, _ = q.shape
    tm = nsub * BLOCK
    cols = SWA_GROUP * BLOCK
    return pl.pallas_call(
        functools.partial(_swa_kernel, nsub=nsub),
        grid=(bsz, seq // tm),
        in_specs=[
            pl.BlockSpec((1, tm, SWA_WIDTH), lambda b, i: (b, i, 0)),
            pl.BlockSpec((1, BLOCK, SWA_KV_COLS), lambda b, i: (b, jnp.maximum(nsub * i - 1, 0), 0)),
            pl.BlockSpec((1, tm, SWA_KV_COLS), lambda b, i: (b, i, 0)),
            _const_spec((3, SWA_KV_HEADS, 2 * BLOCK, cols)),
            _const_spec((SWA_KV_HEADS, 1, cols)),
        ],
        out_specs=pl.BlockSpec((1, tm, SWA_WIDTH), lambda b, i: (b, i, 0)),
        out_shape=jax.ShapeDtypeStruct((bsz, seq, SWA_WIDTH), F32),
        compiler_params=pltpu.CompilerParams(
            dimension_semantics=("parallel", "parallel"), vmem_limit_bytes=VMEM_LIMIT),
        name="swa",
    )(q, kv, kv, bias, sink_rows)


def _mla_kernel(q_ref, qn_ref, kraw_ref, ktail_ref, wuv_ref, o_ref, k_sc, sa_sc, sb_sc, acc_sc, mx_sc, *, tq, rows):
    i = pl.program_id(1)
    cols = MLA_HEADS * tq
    nsub = tq // BLOCK
    seq = kraw_ref.shape[1]
    q = q_ref[0].reshape(cols, MLA_QK)
    n_groups = ((i + 1) * tq + rows - 1) // rows
    nt_dims = (((1,), (1,)), ((), ()))
    tn_dims = (((0,), (0,)), ((), ()))

    def fold(x, op):
        return op(x.reshape(x.shape[0] // 8, 8, x.shape[1]), axis=0)

    n_parts = rows // MLA_PART

    def scores_part(g, buf, c, qq=q):
        start = pl.multiple_of(g * rows + c * MLA_PART, MLA_PART)
        s = lax.dot_general(k_sc[pl.ds(start, MLA_PART), :], qq, nt_dims, preferred_element_type=F32)
        buf[c * MLA_PART:(c + 1) * MLA_PART, :] = s
        return fold(s, jnp.max)

    def rescale(mx8, m_run):
        m_new = jnp.maximum(m_run, jnp.max(mx8, axis=0, keepdims=True))
        acc_sc[...] = acc_sc[...] * jnp.exp2(m_run - m_new)
        return m_new

    def weights_part(g, buf, c, m_new):
        start = pl.multiple_of(g * rows + c * MLA_PART, MLA_PART)
        p = jnp.exp2(buf[c * MLA_PART:(c + 1) * MLA_PART, :] - m_new).astype(BF16)
        v = k_sc[pl.ds(start, MLA_PART), 0:MLA_ACC]
        acc_sc[...] += lax.dot_general(v, p, tn_dims, preferred_element_type=F32)

    def scores(g, buf):
        return functools.reduce(jnp.maximum, [scores_part(g, buf, c) for c in range(n_parts)])

    def weights(g, buf, mx8, m_run):
        m_new = rescale(mx8, m_run)
        for c in range(n_parts):
            weights_part(g, buf, c, m_new)
        return m_new

    def overlapped(g_next, buf_next, g, buf, mx8, m_run):
        m_new = rescale(mx8, m_run)
        parts = []
        for c in range(n_parts):
            parts.append(scores_part(g_next, buf_next, c))
            weights_part(g, buf, c, m_new)
        return functools.reduce(jnp.maximum, parts), m_new

    @pl.when(i == 0)
    def _():
        k_sc[0:seq, :] = kraw_ref[0]
        k_sc[seq:, :] = ktail_ref[...]
        mx_sc[...] = scores(0, sa_sc)

    acc_sc[...] = jnp.zeros((MLA_ACC, cols), F32)
    m0 = jnp.full((1, cols), NEG, F32)

    def pair(k, carry):
        mxa, m_run = carry
        mxb, m_run = overlapped(2 * k + 1, sb_sc, 2 * k, sa_sc, mxa, m_run)
        mxa, m_run = overlapped(2 * k + 2, sa_sc, 2 * k + 1, sb_sc, mxb, m_run)
        return mxa, m_run

    n_pairs = (n_groups - 1) // 2
    mxa, m_run = lax.fori_loop(0, n_pairs, pair, (mx_sc[...], m0))

    def one_left(m_run):
        return weights(n_groups - 1, sa_sc, mxa, m_run)

    def two_left(m_run):
        mxb, m_run = overlapped(n_groups - 1, sb_sc, n_groups - 2, sa_sc, mxa, m_run)
        return weights(n_groups - 1, sb_sc, mxb, m_run)

    m_run = lax.cond(n_groups - 1 == 2 * n_pairs, one_left, two_left, m_run)

    lane = lax.broadcasted_iota(jnp.int32, (1, MLA_QK), 1)
    onehot = (lane >= MLA_KV_RANK + 2) & (lane < MLA_KV_RANK + 2 + MLA_BLOCKS)
    hcols = MLA_HEADS * BLOCK
    causal = lax.broadcasted_iota(jnp.int32, (BLOCK, hcols), 0) <= (
        lax.broadcasted_iota(jnp.int32, (BLOCK, hcols), 1) & (BLOCK - 1))
    row0s = [pl.multiple_of((i * nsub + r) * BLOCK, BLOCK) for r in range(nsub)]
    lanes = [[slice(hd * tq + r * BLOCK, hd * tq + (r + 1) * BLOCK) for hd in range(MLA_HEADS)]
             for r in range(nsub)]
    qn = qn_ref[0].reshape(cols, MLA_QK)
    ahead = [scores_part(0, sa_sc, 0, qn)]
    sds = []
    for r in range(nsub):
        qd = jnp.concatenate([q[ln, :] for ln in lanes[r]], axis=0)
        qd = jnp.where(onehot, jnp.zeros((), BF16), qd)
        sd = lax.dot_general(k_sc[pl.ds(row0s[r], BLOCK), :], qd, nt_dims, preferred_element_type=F32)
        sds.append(jnp.where(causal, sd, NEG))
    ps, alphas = [], []
    for r in range(nsub):
        m_old = jnp.concatenate([m_run[:, ln] for ln in lanes[r]], axis=1)
        m_new = jnp.maximum(m_old, jnp.max(sds[r], axis=0, keepdims=True))
        ps.append(jnp.exp2(sds[r] - m_new).astype(BF16))
        alphas.append(jnp.exp2(m_old - m_new))
    ahead += [scores_part(0, sa_sc, c, qn) for c in range(1, n_parts)]
    mx_sc[...] = functools.reduce(jnp.maximum, ahead)
    pvs = [lax.dot_general(k_sc[pl.ds(row0s[r], BLOCK), 0:MLA_ACC], ps[r], tn_dims,
                           preferred_element_type=F32) for r in range(nsub)]
    o_parts = []
    for r in range(nsub):
        acc = jnp.concatenate([acc_sc[:, ln] for ln in lanes[r]], axis=1) * alphas[r] + pvs[r]
        o_parts.append((acc[0:MLA_KV_RANK, :] / acc[MLA_KV_RANK:MLA_KV_RANK + 1, :]).astype(BF16))
    ys = [[lax.dot_general(o_parts[r][:, hd * BLOCK:(hd + 1) * BLOCK], wuv_ref[0, hd], tn_dims,
                           preferred_element_type=F32) for hd in range(MLA_HEADS)] for r in range(nsub)]
    o_ref[0] = jnp.concatenate([functools.reduce(jnp.add, ys[r]) for r in range(nsub)], axis=0)


def _mla(qm, kc, key_tail, wuv_heads, layer, *, tq=384, rows=MLA_KEY_ROWS):
    bsz, _, seq, _ = qm.shape
    seq_k = seq + key_tail.shape[0]
    cols = MLA_HEADS * tq
    n_blocks = seq // tq
    return pl.pallas_call(
        functools.partial(_mla_kernel, tq=tq, rows=rows),
        grid=(bsz, seq // tq),
        in_specs=[
            pl.BlockSpec((1, MLA_HEADS, tq, MLA_QK), lambda b, i: (b, 0, i, 0)),
            pl.BlockSpec((1, MLA_HEADS, tq, MLA_QK), lambda b, i: (b, 0, jnp.minimum(i + 1, n_blocks - 1), 0)),
            pl.BlockSpec((1, seq, MLA_QK), lambda b, i: (b, 0, 0)),
            _const_spec((seq_k - seq, MLA_QK)),
            _layer_spec((MLA_HEADS, MLA_KV_RANK, MLA_WIDTH), layer),
        ],
        out_specs=pl.BlockSpec((1, tq, MLA_WIDTH), lambda b, i: (b, i, 0)),
        out_shape=jax.ShapeDtypeStruct((bsz, seq, MLA_WIDTH), F32),
        scratch_shapes=[pltpu.VMEM((seq_k, MLA_QK), BF16), pltpu.VMEM((rows, cols), F32),
                        pltpu.VMEM((rows, cols), F32), pltpu.VMEM((MLA_ACC, cols), F32), pltpu.VMEM((8, cols), F32)],
        compiler_params=pltpu.CompilerParams(
            dimension_semantics=("parallel", "arbitrary"), vmem_limit_bytes=VMEM_LIMIT),
        name="mla",
    )(qm, qm, kc, key_tail, wuv_heads)


def _gelu_tanh(x):
    return 0.5 * x * (1.0 + jnp.tanh(math.sqrt(2.0 / math.pi) * (x + 0.044715 * (x * x * x))))


def _mix_out_kernel(h_ref, head_ref, ys_ref, yw_ref, ym_ref, wglu_ref, bglu_ref, gh_ref, wout_ref, gpm_ref, gpre_ref,
                    wup_ref, wdn_ref, gpost_ref, o_ref, *, ff_chunk, from_x):
    z = _gelu_tanh(jnp.concatenate([ys_ref[0, 0], ys_ref[0, 1]], axis=1))
    gate = jnp.dot(z.astype(BF16), wglu_ref[0], preferred_element_type=F32) + bglu_ref[...]
    y_ssm = z * (1.0 / (1.0 + jnp.exp(-gate)))
    gh = gh_ref[...]
    mixed = jnp.concatenate([
        _rms(y_ssm, gh[:, 0:SSM_WIDTH]),
        _rms(yw_ref[0], gh[:, SSM_WIDTH:SSM_WIDTH + SWA_WIDTH]),
        _rms(ym_ref[0], gh[:, SSM_WIDTH + SWA_WIDTH:]),
    ], axis=1).astype(BF16)
    mo = jnp.dot(mixed, wout_ref[0], preferred_element_type=F32)
    h_rows = _seq_rows(h_ref, head_ref, pl.program_id(1), 0, o_ref.shape[1]) if from_x else h_ref[0]
    h1 = h_rows + _rms(mo, gpm_ref[...])
    a = _rms(h1, gpre_ref[...]).astype(BF16)
    f = jnp.zeros_like(h1)
    for c in range(D_FF // ff_chunk):
        up = jnp.dot(a, wup_ref[0, :, c * ff_chunk:(c + 1) * ff_chunk], preferred_element_type=F32)
        hid = jnp.square(jnp.maximum(up, 0.0)).astype(BF16)
        f += jnp.dot(hid, wdn_ref[0, c * ff_chunk:(c + 1) * ff_chunk, :], preferred_element_type=F32)
    o_ref[0] = h1 + _rms(f, gpost_ref[...])


def _mix_out(h, head, ys, yw, ym, wglu, bglu, gh, wout, gpm, gpre, wup, wdn, gpost, layer, *, from_x=False, skip=0,
             tm=1056, ff_chunk=1024):
    assert not (from_x and skip)
    bsz, seq = yw.shape[0], yw.shape[1]
    out_rows = seq - skip
    if skip:
        el = pl.Element
        first = lambda i: pl.multiple_of(skip + i * tm, BLOCK)
        row = lambda w: pl.BlockSpec((el(1), el(tm), el(w)), lambda b, i: (b, first(i), 0))
        slabs = pl.BlockSpec((el(1), el(2), el(tm), el(128)), lambda b, i: (b, 0, first(i), 0))
    else:
        row = lambda w: pl.BlockSpec((1, tm, w), lambda b, i: (b, i, 0))
        slabs = pl.BlockSpec((1, 2, tm, 128), lambda b, i: (b, 0, i, 0))
    return pl.pallas_call(
        functools.partial(_mix_out_kernel, ff_chunk=ff_chunk, from_x=from_x),
        grid=(bsz, out_rows // tm),
        in_specs=[
            _seq_window_spec(tm, D_MODEL) if from_x else row(D_MODEL), _const_spec((BLOCK, D_MODEL)),
            slabs, row(SWA_WIDTH), row(MLA_WIDTH),
            _layer_spec((SSM_WIDTH, SSM_WIDTH), layer), _const_spec((1, SSM_WIDTH)), _const_spec((1, D_MODEL)),
            _layer_spec((D_MODEL, D_MODEL), layer), _const_spec((1, D_MODEL)), _const_spec((1, D_MODEL)),
            _layer_spec((D_MODEL, D_FF), layer), _layer_spec((D_FF, D_MODEL), layer), _const_spec((1, D_MODEL)),
        ],
        out_specs=pl.BlockSpec((1, tm, D_MODEL), lambda b, i: (b, i, 0)),
        out_shape=jax.ShapeDtypeStruct((bsz, out_rows, D_MODEL), F32),
        compiler_params=pltpu.CompilerParams(
            dimension_semantics=("parallel", "parallel"), vmem_limit_bytes=VMEM_LIMIT),
        name="mix_out",
    )(h, head, ys, yw, ym, wglu, bglu, gh, wout, gpm, gpre, wup, wdn, gpost)


def _rot_half_cols(w):
    half = w.shape[-1] // 2
    return jnp.concatenate([-w[..., half:], w[..., :half]], axis=-1)


def _stacked_weights(w_in, w_uq, w_uk, w_uv):
    depth = w_in.shape[0]
    z = lambda n: jnp.zeros((depth, D_MODEL, n), w_in.dtype)
    s_cq = SSM_WIDTH + SWA_WIDTH + 2 * SWA_KV_WIDTH
    s_ckv = s_cq + MLA_Q_RANK
    s_kr = s_ckv + MLA_KV_RANK
    kr = w_in[:, :, s_kr:s_kr + MLA_ROPE]
    w1 = jnp.concatenate([
        w_in[:, :, :s_ckv], z(C_CKV - C_CQ - MLA_Q_RANK),
        w_in[:, :, s_ckv:s_kr], kr, _rot_half_cols(kr), z(C_END - C_KR - 2 * MLA_ROPE)], axis=2).astype(BF16)
    uq = w_uq.reshape(depth, MLA_Q_RANK, MLA_HEADS, MLA_NOPE + MLA_ROPE)
    rope = uq[..., MLA_NOPE:]
    wuq = jnp.concatenate([
        uq[..., :MLA_NOPE].reshape(depth, MLA_Q_RANK, -1), rope.reshape(depth, MLA_Q_RANK, -1),
        _rot_half_cols(rope).reshape(depth, MLA_Q_RANK, -1)], axis=2)
    wuq = jnp.pad(wuq, ((0, 0), (0, 256 - MLA_Q_RANK), (0, 0))).astype(BF16)
    uk = w_uk.reshape(depth, MLA_KV_RANK, MLA_HEADS, MLA_NOPE).transpose(0, 2, 3, 1)
    wuk_bd = _group_block_diag(uk).astype(BF16)
    uv = w_uv.reshape(depth, MLA_KV_RANK, MLA_HEADS, MLA_V).transpose(0, 2, 1, 3)
    wuv_heads = _group_block_diag(uv).reshape(depth, MLA_HEADS, MLA_KV_RANK, MLA_WIDTH).astype(BF16)
    return w1, wuq, wuk_bd, wuv_heads


def _constant_tables(seq):
    half = MLA_ROPE // 2
    inv_freq = ROPE_THETA ** (-np.arange(half, dtype=np.float64) / half)
    ang = (np.arange(seq) - PAD)[:, None] * inv_freq[None, :]
    cos_t = np.tile(np.cos(ang), (1, 2 * MLA_HEADS)).astype(np.float32)
    sin_t = np.tile(np.sin(ang), (1, 2 * MLA_HEADS)).astype(np.float32)
    seq_k = -(-seq // MLA_KEY_ROWS) * MLA_KEY_ROWS
    kblk = np.arange(seq_k)[:, None] // BLOCK
    bcol = np.arange(MLA_BLOCKS)[None, :]
    one = np.ones((seq_k, 1))
    fill = np.zeros((seq_k, 128 - 2 - MLA_BLOCKS))
    kx_t = np.concatenate([one, np.where(np.arange(seq_k)[:, None] < PAD, NEG, 0.0), kblk == bcol, fill], axis=1)
    qx_t = np.concatenate([0.0 * one, one, np.where(bcol >= kblk, NEG, 0.0), fill], axis=1)[:seq]
    key_tail = np.zeros((seq_k - seq, MLA_QK))
    key_tail[:, MLA_KV_RANK:MLA_KV_RANK + MLA_EXT] = kx_t[seq:, :MLA_EXT]
    slopes = 2.0 ** (-8.0 * np.arange(1, SWA_HEADS + 1) / SWA_HEADS)
    slope_rows = np.repeat(slopes.reshape(SWA_KV_HEADS, SWA_GROUP), BLOCK, axis=1)[:, None, :]
    dist = (np.arange(BLOCK)[None, :] + BLOCK) - np.arange(2 * BLOCK)[:, None]
    in_window = np.tile((dist >= 0) & (dist < BLOCK), (1, SWA_GROUP))[None, None]
    alibi = (-LOG2E * slope_rows * np.tile(dist, (1, SWA_GROUP))[None])[None]
    key_pos = (np.arange(3)[:, None] - 1) * BLOCK + np.arange(2 * BLOCK)[None, :]
    key_ok = ((key_pos >= PAD) | (np.arange(3)[:, None] >= 2))[:, None, :, None]
    swa_bias = np.where(in_window & key_ok, alibi, NEG).astype(np.float32)
    bf16 = lambda t: jnp.asarray(t, dtype=BF16)
    return (jnp.asarray(cos_t), jnp.asarray(sin_t), bf16(qx_t), bf16(kx_t[:seq]), bf16(key_tail),
            jnp.asarray(swa_bias))


def kernel(x, meta_tokens, norm_pre_mix, norm_post_mix, norm_pre_mlp, norm_post_mlp, w_in, w_out, norm_heads,
           ssm_a_re, ssm_a_im, ssm_log_step, ssm_b_re, ssm_b_im, ssm_c_re, ssm_c_im, ssm_d, ssm_w_glu,
           ssm_b_glu, swa_sinks, mla_q_norm, mla_kv_norm, mla_w_uq, mla_w_uk, mla_w_uv, w_mlp_up, w_mlp_down):
    bsz, seq_real, d = x.shape
    depth = w_in.shape[0]
    head = jnp.concatenate([jnp.zeros((PAD, d), x.dtype), meta_tokens.astype(x.dtype)], axis=0)
    seq = BLOCK + seq_real
    h = x

    cos_t, sin_t, qx_t, kx_t, key_tail, swa_bias = _constant_tables(seq)
    per_row = lambda v: jnp.repeat(v.reshape(SWA_KV_HEADS, SWA_GROUP), BLOCK, axis=1)[:, None, :]

    r_mat, p_mat, q_mat, lam = _s5_operators(
        ssm_a_re, ssm_a_im, ssm_log_step, ssm_b_re, ssm_b_im, ssm_c_re, ssm_c_im)
    row = lambda v: v.reshape(1, -1).astype(F32)
    wglu, wout, wup, wdn = (w.astype(BF16) for w in (ssm_w_glu, w_out, w_mlp_up, w_mlp_down))
    w1, wuq, wuk_bd, wuv_heads = _stacked_weights(w_in, mla_w_uq, mla_w_uk, mla_w_uv)

    for l in range(depth):
        gq =jnp.concatenate([mla_q_norm[l], jnp.zeros((256 - MLA_Q_RANK,), F32)]).reshape(1, 256)
        u, q, kv, qm, kc = _mix_in(h, head, row(norm_pre_mix[l]), w1, gq, row(mla_kv_norm[l]), wuq, wuk_bd,
                                   cos_t, sin_t, qx_t, kx_t, l, from_x=l == 0)
        d_fold = jnp.tile(ssm_d[l].astype(F32), SSM_CHUNK).reshape(1, SSM_FOLD)
        y_ssm = _s5_sequence(u, r_mat, p_mat, q_mat, lam, d_fold, l)
        y_swa = _swa(q, kv, swa_bias, per_row(LOG2E * swa_sinks[l].astype(F32)))
        y_mla = _mla(qm, kc, key_tail, wuv_heads, l)
        last = l == depth - 1
        h = _mix_out(
            h, head, y_ssm, y_swa, y_mla,
            wglu, row(ssm_b_glu[l]), row(norm_heads[l]), wout, row(norm_post_mix[l]), row(norm_pre_mlp[l]),
            wup, wdn, row(norm_post_mlp[l]), l, from_x=l == 0, skip=PAD + N_META if last else 0,
            tm=1024 if last else 1056)
    return h
```

```python
import functools
import math

import jax
import jax.numpy as jnp
import numpy as np
from jax import lax
from jax.experimental import pallas as pl
from jax.experimental.pallas import tpu as pltpu

F32 = jnp.float32
BF16 = jnp.bfloat16

D_MODEL = 1024
N_META = 16
BLOCK = 128
PAD = BLOCK - N_META
SSM_GROUPS = 16
SSM_GROUP_CH = 16
SSM_WIDTH = SSM_GROUPS * SSM_GROUP_CH
SSM_STATE = 64
SWA_HEADS = 8
SWA_KV_HEADS = 2
SWA_GROUP = SWA_HEADS // SWA_KV_HEADS
HEAD_DIM = 64
SWA_WIDTH = SWA_HEADS * HEAD_DIM
SWA_KV_WIDTH = SWA_KV_HEADS * HEAD_DIM
SWA_KV_COLS = SWA_KV_WIDTH + SWA_KV_HEADS * BLOCK
MLA_HEADS = 4
MLA_Q_RANK = 192
MLA_KV_RANK = 128
MLA_NOPE = 64
MLA_ROPE = 32
MLA_V = 64
MLA_WIDTH = MLA_HEADS * MLA_V
MLA_BLOCKS = 40
MLA_EXT = 48
MLA_ACC = MLA_KV_RANK + 8
LOG2E = 1.4426950408889634
MLA_QK = MLA_KV_RANK + MLA_EXT + MLA_ROPE
ROPE_THETA = 10000.0
D_FF = 4 * D_MODEL
EPS = 1e-6
NEG = -1e30

SSM_CHUNK = 8
SSM_FOLD = SSM_CHUNK * SSM_WIDTH
SSM_MODES = SSM_GROUPS * SSM_STATE

C_U, C_Q, C_K, C_CQ, C_CKV, C_KR, C_END = 0, 256, 768, 1024, 1280, 1408, 1536

MLA_KEY_ROWS = 512
MLA_PART = 256

VMEM_LIMIT = 58 * 1024 * 1024


def _rms(x, g, n=None):
    n = x.shape[-1] if n is None else n
    ms = jnp.sum(x * x, axis=-1, keepdims=True) * (1.0 / n)
    return x * lax.rsqrt(ms + EPS) * g


def _const_spec(shape):
    nd = len(shape)
    return pl.BlockSpec(shape, lambda *_: (0,) * nd, pipeline_mode=pl.Buffered(1))


def _seq_window_spec(tm, width):
    el = pl.Element
    return pl.BlockSpec((el(1), el(tm), el(width)),
                        lambda b, i: (b, pl.multiple_of(jnp.maximum(i * tm - BLOCK, 0), 8), 0))


def _seq_rows(x_ref, head_ref, i, lo, n):
    first = jnp.where(i == 0, 1, 0)
    if lo == 0:
        top = jnp.where(i == 0, head_ref[...], x_ref[0, 0:BLOCK, :])
        rest = x_ref[0, pl.ds(pl.multiple_of(BLOCK - BLOCK * first, 8), n - BLOCK), :]
        return jnp.concatenate([top, rest], axis=0)
    return x_ref[0, pl.ds(pl.multiple_of(lo - BLOCK * first, 8), n), :]


def _layer_spec(shape, layer):
    nd = len(shape)
    return pl.BlockSpec((1,) + shape, lambda *_: (layer,) + (0,) * nd, pipeline_mode=pl.Buffered(1))


def _mix_in_kernel(h_ref, head_ref, g_ref, w1_ref, gq_ref, gkv_ref, wuq_ref, wuk_ref, cos_ref, sin_ref, qx_ref,
                   kx_ref, u_ref, q_ref, kv_ref, qm_ref, kc_ref, *, tm, nsub, from_x):
    j = pl.program_id(1)
    ts = tm // nsub
    scale = LOG2E * (MLA_NOPE + MLA_ROPE) ** -0.5
    ext = slice(MLA_KV_RANK, MLA_KV_RANK + MLA_EXT)
    rot = slice(MLA_KV_RANK + MLA_EXT, MLA_QK)
    for sub in range(nsub):
        rows = slice(sub * ts, (sub + 1) * ts)
        h_rows = _seq_rows(h_ref, head_ref, j, sub * ts, ts) if from_x else h_ref[0, rows, :]
        a = _rms(h_rows, g_ref[...]).astype(BF16)
        proj = jnp.dot(a, w1_ref[0], preferred_element_type=F32)
        row = j * tm + sub * ts + lax.broadcasted_iota(jnp.int32, (ts, 1), 0)
        for c in range(2):
            u_ref[0, c, rows, :] = jnp.where(row >= PAD, proj[:, C_U + c * 128:C_U + (c + 1) * 128], 0.0)
        q_ref[0, rows, :] = (proj[:, C_Q:C_K] * (LOG2E * HEAD_DIM ** -0.5)).astype(BF16)
        kv_ref[0, rows, 0:SWA_KV_WIDTH] = proj[:, C_K:C_K + SWA_KV_WIDTH].astype(BF16)
        ones_col = jnp.where(lax.broadcasted_iota(jnp.int32, (ts, HEAD_DIM), 1) == 0, 1.0, 0.0).astype(BF16)
        for hk in range(SWA_KV_HEADS):
            lo = SWA_KV_WIDTH + hk * BLOCK
            v_lo = C_K + SWA_KV_WIDTH + hk * HEAD_DIM
            kv_ref[0, rows, lo:lo + HEAD_DIM] = proj[:, v_lo:v_lo + HEAD_DIM].astype(BF16)
            kv_ref[0, rows, lo + HEAD_DIM:lo + BLOCK] = ones_col

        cos = cos_ref[rows, :]
        sin = sin_ref[rows, :]
        qn = _rms(proj[:, C_CQ:C_CKV], gq_ref[...], n=MLA_Q_RANK).astype(BF16)
        q2 = jnp.dot(qn, wuq_ref[0], preferred_element_type=F32)
        qr = (q2[:, 256:384] * cos + q2[:, 384:512] * sin) * scale
        qa = jnp.dot(q2[:, 0:256].astype(BF16), wuk_ref[0], preferred_element_type=F32) * scale
        c = _rms(proj[:, C_CKV:C_KR], gkv_ref[...])
        krr = proj[:, C_KR:C_END]
        kr = krr[:, 0:MLA_ROPE] * cos[:, 0:MLA_ROPE] + krr[:, MLA_ROPE:2 * MLA_ROPE] * sin[:, 0:MLA_ROPE]
        for hd in range(MLA_HEADS):
            qm_ref[0, hd, rows, 0:MLA_KV_RANK] = qa[:, hd * MLA_KV_RANK:(hd + 1) * MLA_KV_RANK].astype(BF16)
            qm_ref[0, hd, rows, ext] = qx_ref[rows, 0:MLA_EXT]
            qm_ref[0, hd, rows, rot] = qr[:, hd * MLA_ROPE:(hd + 1) * MLA_ROPE].astype(BF16)
        kc_ref[0, rows, 0:MLA_KV_RANK] = c.astype(BF16)
        kc_ref[0, rows, ext] = kx_ref[rows, 0:MLA_EXT]
        kc_ref[0, rows, rot] = kr.astype(BF16)


def _mix_in(h, head, g, w1, gq, gkv, wuq, wuk, cos_t, sin_t, qx_t, kx_t, layer, *, from_x, tm=2112, nsub=4):
    bsz = h.shape[0]
    seq = cos_t.shape[0]
    grid = (bsz, seq // tm)
    row3 = lambda b, j: (b, j, 0)
    return pl.pallas_call(
        functools.partial(_mix_in_kernel, tm=tm, nsub=nsub, from_x=from_x),
        grid=grid,
        in_specs=[
            _seq_window_spec(tm, D_MODEL) if from_x else pl.BlockSpec((1, tm, D_MODEL), row3),
            _const_spec((BLOCK, D_MODEL)),
            _const_spec((1, D_MODEL)),
            _layer_spec((D_MODEL, C_END), layer),
            _const_spec((1, 256)),
            _const_spec((1, MLA_KV_RANK)),
            _layer_spec((256, 512), layer),
            _layer_spec((256, 512), layer),
            pl.BlockSpec((tm, 128), lambda b, j: (j, 0)),
            pl.BlockSpec((tm, 128), lambda b, j: (j, 0)),
            pl.BlockSpec((tm, 128), lambda b, j: (j, 0)),
            pl.BlockSpec((tm, 128), lambda b, j: (j, 0)),
        ],
        out_specs=[
            pl.BlockSpec((1, 2, tm, 128), lambda b, j: (b, 0, j, 0)),
            pl.BlockSpec((1, tm, SWA_WIDTH), row3),
            pl.BlockSpec((1, tm, SWA_KV_COLS), row3),
            pl.BlockSpec((1, MLA_HEADS, tm, MLA_QK), lambda b, j: (b, 0, j, 0)),
            pl.BlockSpec((1, tm, MLA_QK), row3),
        ],
        out_shape=[
            jax.ShapeDtypeStruct((bsz, 2, seq, 128), F32),
            jax.ShapeDtypeStruct((bsz, seq, SWA_WIDTH), BF16),
            jax.ShapeDtypeStruct((bsz, seq, SWA_KV_COLS), BF16),
            jax.ShapeDtypeStruct((bsz, MLA_HEADS, seq, MLA_QK), BF16),
            jax.ShapeDtypeStruct((bsz, seq, MLA_QK), BF16),
        ],
        compiler_params=pltpu.CompilerParams(
            dimension_semantics=("parallel", "parallel"), vmem_limit_bytes=VMEM_LIMIT),
        name="mix_in",
    )(h, head, g, w1, gq, gkv, wuq, wuk, cos_t, sin_t, qx_t, kx_t)


def _split_bf16(x):
    hi = x.astype(BF16)
    lo = (x - hi.astype(F32)).astype(BF16)
    return hi, lo


def _bdot3(a, b):
    ah, al = _split_bf16(a)
    bh, bl = _split_bf16(b)
    dn = (((2,), (2,)), ((0,), (0,)))
    f = lambda x, y: lax.dot_general(x, y, dn, preferred_element_type=F32)
    return f(ah, bh) + f(ah, bl) + f(al, bh)


def _expand_groups(a2, row_shift, col_shift, n_cols):
    rows, w = a2.shape
    tile = (lax.broadcasted_iota(jnp.int32, (w, n_cols), 1) & (w - 1)) == lax.broadcasted_iota(
        jnp.int32, (w, n_cols), 0)
    wide = jnp.dot(a2.astype(BF16), jnp.where(tile, 1.0, 0.0).astype(BF16), preferred_element_type=F32)
    same = (lax.broadcasted_iota(jnp.int32, (rows, n_cols), 0) >> row_shift) == (
        lax.broadcasted_iota(jnp.int32, (rows, n_cols), 1) >> col_shift)
    return jnp.where(same, wide, 0.0)


def _s5_operator_kernel(ar_ref, ai_ref, ls_ref, btr_ref, bti_ref, cr_ref, ci_ref,
                        r_ref, p_ref, q_ref, l8r_ref, l8i_ref, pwr_sc, pwi_sc, kk_sc):
    s = pl.program_id(1)
    ar, ai = ar_ref[0], ai_ref[0]
    step = jnp.exp(ls_ref[0])
    lr = jnp.exp(ar * step) * jnp.cos(ai * step)
    li = jnp.exp(ar * step) * jnp.sin(ai * step)

    den = ar * ar + ai * ai
    nr, ni = lr - 1.0, li
    coef_re = (nr * ar + ni * ai) / den
    coef_im = (ni * ar - nr * ai) / den
    btr, bti = btr_ref[0], bti_ref[0]
    bb_re = coef_re * btr - coef_im * bti
    bb_im = coef_re * bti + coef_im * btr
    cr, ci = cr_ref[0], ci_ref[0]

    def c_lam(k):
        qr_, qi_ = pwr_sc[k], pwi_sc[k]
        return cr * qr_ - ci * qi_, cr * qi_ + ci * qr_

    gc = SSM_GROUPS * SSM_GROUP_CH

    @pl.when(s == 0)
    def _():
        pr, pi = jnp.ones_like(lr), jnp.zeros_like(li)
        for k in range(SSM_CHUNK + 1):
            pwr_sc[k] = pr
            pwi_sc[k] = pi
            pr, pi = pr * lr - pi * li, pr * li + pi * lr
        for k in range(SSM_CHUNK):
            clr, cli = c_lam(k)
            kk_sc[k] = (_bdot3(bb_re, clr) - _bdot3(bb_im, cli)).reshape(gc, SSM_GROUP_CH)

    for t in range(SSM_CHUNK):
        lag = t - s
        blk = _expand_groups(kk_sc[jnp.maximum(lag, 0)], 4, 4, gc)
        r_ref[0, :, t * gc:(t + 1) * gc] = jnp.where(lag >= 0, blk, 0.0).astype(BF16)

    pw_r, pw_i = pwr_sc[SSM_CHUNK - 1 - s], pwi_sc[SSM_CHUNK - 1 - s]
    p_re = (pw_r * bb_re - pw_i * bb_im).reshape(gc, SSM_STATE)
    p_im = (pw_r * bb_im + pw_i * bb_re).reshape(gc, SSM_STATE)
    p_ref[0, :, 0:SSM_MODES] = _expand_groups(p_re, 4, 6, SSM_MODES).astype(BF16)
    p_ref[0, :, SSM_MODES:2 * SSM_MODES] = _expand_groups(p_im, 4, 6, SSM_MODES).astype(BF16)

    clr, cli = c_lam(s + 1)
    q_ref[0, 0:SSM_MODES, :] = _expand_groups(clr.reshape(gc, SSM_STATE), 4, 6, SSM_MODES).T.astype(BF16)
    q_ref[0, SSM_MODES:2 * SSM_MODES, :] = (
        -_expand_groups(cli.reshape(gc, SSM_STATE), 4, 6, SSM_MODES)).T.astype(BF16)

    l8r_ref[0] = pwr_sc[SSM_CHUNK]
    l8i_ref[0] = pwi_sc[SSM_CHUNK]


def _s5_operators(a_re, a_im, log_step, b_re, b_im, c_re, c_im):
    depth = a_re.shape[0]
    G, P, C, T = SSM_GROUPS, SSM_STATE, SSM_GROUP_CH, SSM_CHUNK
    lay = lambda *shape: pl.BlockSpec((1,) + shape, lambda l, s: (l,) + (0,) * len(shape))
    ls = jnp.broadcast_to(log_step[:, :, None, None], (depth, G, 1, P))
    args = (a_re[:, :, None, :], a_im[:, :, None, :], ls,
            b_re.transpose(0, 1, 3, 2), b_im.transpose(0, 1, 3, 2), c_re, c_im)
    r_mat, p_mat, q_mat, l8r, l8i = pl.pallas_call(
        _s5_operator_kernel,
        grid=(depth, T),
        in_specs=[lay(G, 1, P)] * 3 + [lay(G, C, P)] * 4,
        out_specs=[
            pl.BlockSpec((1, SSM_WIDTH, SSM_FOLD), lambda l, s: (l, s, 0)),
            pl.BlockSpec((1, SSM_WIDTH, 2 * SSM_MODES), lambda l, s: (l, s, 0)),
            pl.BlockSpec((1, 2 * SSM_MODES, SSM_WIDTH), lambda l, s: (l, 0, s)),
            lay(G, 1, P), lay(G, 1, P)],
        out_shape=[
            jax.ShapeDtypeStruct((depth, SSM_FOLD, SSM_FOLD), BF16),
            jax.ShapeDtypeStruct((depth, SSM_FOLD, 2 * SSM_MODES), BF16),
            jax.ShapeDtypeStruct((depth, 2 * SSM_MODES, SSM_FOLD), BF16),
            jax.ShapeDtypeStruct((depth, G, 1, P), F32),
            jax.ShapeDtypeStruct((depth, G, 1, P), F32),
        ],
        scratch_shapes=[pltpu.VMEM((T + 1, G, 1, P), F32), pltpu.VMEM((T + 1, G, 1, P), F32),
                        pltpu.VMEM((T, G * C, C), F32)],
        compiler_params=pltpu.CompilerParams(dimension_semantics=("parallel", "arbitrary")),
        name="s5_operators",
    )(*args)
    lam = jnp.concatenate([l8r.reshape(depth, 1, SSM_MODES), l8i.reshape(depth, 1, SSM_MODES)], axis=2)
    return r_mat, p_mat, q_mat, lam


def _group_block_diag(x):
    *lead, G, r, c = x.shape
    eye = jnp.eye(G, dtype=x.dtype)
    y = x[..., :, :, None, :] * eye[:, None, :, None]
    return y.reshape(*lead, G * r, G * c)


def _fold_tokens(u_ref, n_chunks):
    return jnp.concatenate(
        [u_ref[0, c, pl.ds(s, n_chunks, stride=SSM_CHUNK), :] for s in range(SSM_CHUNK) for c in range(2)], axis=1)


def _s5_state_kernel(u_ref, p_ref, lam_ref, x_ref, s_sc, x_sc, *, n_chunks):
    u = _fold_tokens(u_ref, n_chunks).astype(BF16)
    s_sc[...] = jnp.dot(u, p_ref[0], preferred_element_type=F32)
    lr = lam_ref[0, :, 0:SSM_MODES]
    li = lam_ref[0, :, SSM_MODES:2 * SSM_MODES]

    def body(n, carry):
        xr, xi = carry
        x_sc[pl.ds(n, 1), 0:SSM_MODES] = xr
        x_sc[pl.ds(n, 1), SSM_MODES:2 * SSM_MODES] = xi
        sr = s_sc[pl.ds(n, 1), 0:SSM_MODES]
        si = s_sc[pl.ds(n, 1), SSM_MODES:2 * SSM_MODES]
        return lr * xr - li * xi + sr, lr * xi + li * xr + si

    z = jnp.zeros((1, SSM_MODES), F32)
    lax.fori_loop(0, n_chunks, body, (z, z))
    x_ref[0] = x_sc[...].astype(BF16)


def _s5_out_kernel(u_ref, x_ref, r_ref, q_ref, d_ref, y_ref, *, n_chunks):
    u = _fold_tokens(u_ref, n_chunks)
    ub = u.astype(BF16)
    x = x_ref[0]
    for t in range(SSM_CHUNK):
        cols = slice(t * SSM_WIDTH, (t + 1) * SSM_WIDTH)
        rows = (t + 1) * SSM_WIDTH
        y = jnp.dot(ub[:, 0:rows], r_ref[0, 0:rows, cols], preferred_element_type=F32)
        y += jnp.dot(x, q_ref[0, :, cols], preferred_element_type=F32)
        y += d_ref[:, cols] * u[:, cols]
        for c in range(2):
            y_ref[0, c, pl.ds(t, n_chunks, stride=SSM_CHUNK), :] = y[:, c * 128:(c + 1) * 128]


def _s5_sequence(u, r_mat, p_mat, q_mat, lam, d_fold, layer):
    bsz, _, seq, _ = u.shape
    n_chunks = seq // SSM_CHUNK
    params = pltpu.CompilerParams(dimension_semantics=("parallel",), vmem_limit_bytes=VMEM_LIMIT)
    slabs = pl.BlockSpec((1, 2, seq, 128), lambda b: (b, 0, 0, 0))
    states = pl.BlockSpec((1, n_chunks, 2 * SSM_MODES), lambda b: (b, 0, 0))
    x_prev = pl.pallas_call(
        functools.partial(_s5_state_kernel, n_chunks=n_chunks), grid=(bsz,),
        in_specs=[slabs, _layer_spec((SSM_FOLD, 2 * SSM_MODES), layer), _layer_spec((1, 2 * SSM_MODES), layer)],
        out_specs=states,
        out_shape=jax.ShapeDtypeStruct((bsz, n_chunks, 2 * SSM_MODES), BF16),
        scratch_shapes=[pltpu.VMEM((n_chunks, 2 * SSM_MODES), F32), pltpu.VMEM((n_chunks, 2 * SSM_MODES), F32)],
        compiler_params=params, name="s5_state",
    )(u, p_mat, lam)
    return pl.pallas_call(
        functools.partial(_s5_out_kernel, n_chunks=n_chunks), grid=(bsz,),
        in_specs=[slabs, states, _layer_spec((SSM_FOLD, SSM_FOLD), layer),
                  _layer_spec((2 * SSM_MODES, SSM_FOLD), layer), _const_spec((1, SSM_FOLD))],
        out_specs=slabs,
        out_shape=jax.ShapeDtypeStruct((bsz, 2, seq, 128), F32),
        compiler_params=params, name="s5_out",
    )(u, x_prev, r_mat, q_mat, d_fold)


def _swa_kernel(q_ref, kvp_ref, kvc_ref, bias_ref, sink_ref, o_ref, *, nsub):
    i = pl.program_id(1)
    nt_dims = (((1,), (1,)), ((), ()))
    tn_dims = (((0,), (0,)), ((), ()))
    chains = [(r, hk) for r in range(nsub) for hk in range(SWA_KV_HEADS)]
    scores, vcats = [], []
    for r, hk in chains:
        rows = slice(r * BLOCK, (r + 1) * BLOCK)
        q = q_ref[0, rows, :]
        kvc = kvc_ref[0, rows, :]
        kvp = kvp_ref[0] if r == 0 else kvc_ref[0, (r - 1) * BLOCK:r * BLOCK, :]
        ks = slice(hk * HEAD_DIM, (hk + 1) * HEAD_DIM)
        vs = slice(SWA_KV_WIDTH + hk * BLOCK, SWA_KV_WIDTH + (hk + 1) * BLOCK)
        kcat = jnp.concatenate([kvp[:, ks], kvc[:, ks]], axis=0)
        vcats.append(jnp.concatenate([kvp[:, vs], kvc[:, vs]], axis=0))
        qs = jnp.concatenate(
            [q[:, (hk * SWA_GROUP + g) * HEAD_DIM:(hk * SWA_GROUP + g + 1) * HEAD_DIM]
             for g in range(SWA_GROUP)], axis=0)
        table = jnp.minimum(i * nsub + r, 2)
        scores.append(lax.dot_general(kcat, qs, nt_dims, preferred_element_type=F32) + bias_ref[table, hk])
    weights, sink_terms = [], []
    for (r, hk), s in zip(chains, scores):
        sink = sink_ref[hk]
        m = jnp.maximum(jnp.max(s, axis=0, keepdims=True), sink)
        weights.append(jnp.exp2(s - m).astype(BF16))
        sink_terms.append(jnp.exp2(sink - m))
    o_ts = []
    for v, p, sink_term in zip(vcats, weights, sink_terms):
        o_t = lax.dot_general(v, p, tn_dims, preferred_element_type=F32)
        o_ts.append(o_t[0:HEAD_DIM, :] / (o_t[HEAD_DIM:HEAD_DIM + 1, :] + sink_term))
    for r in range(nsub):
        outs = []
        for hk in range(SWA_KV_HEADS):
            o_t = o_ts[r * SWA_KV_HEADS + hk]
            outs += [o_t[:, g * BLOCK:(g + 1) * BLOCK] for g in range(SWA_GROUP)]
        o_ref[0, r * BLOCK:(r + 1) * BLOCK, :] = jnp.concatenate(outs, axis=0).T


def _swa(q, kv, bias, sink_rows, *, nsub=11):
    bsz, seq, _ = q.shape
    tm = nsub * BLOCK
    cols = SWA_GROUP * BLOCK
    return pl.pallas_call(
        functools.partial(_swa_kernel, nsub=nsub),
        grid=(bsz, seq // tm),
        in_specs=[
            pl.BlockSpec((1, tm, SWA_WIDTH), lambda b, i: (b, i, 0)),
            pl.BlockSpec((1, BLOCK, SWA_KV_COLS), lambda b, i: (b, jnp.maximum(nsub * i - 1, 0), 0)),
            pl.BlockSpec((1, tm, SWA_KV_COLS), lambda b, i: (b, i, 0)),
            _const_spec((3, SWA_KV_HEADS, 2 * BLOCK, cols)),
            _const_spec((SWA_KV_HEADS, 1, cols)),
        ],
        out_specs=pl.BlockSpec((1, tm, SWA_WIDTH), lambda b, i: (b, i, 0)),
        out_shape=jax.ShapeDtypeStruct((bsz, seq, SWA_WIDTH), F32),
        compiler_params=pltpu.CompilerParams(
            dimension_semantics=("parallel", "parallel"), vmem_limit_bytes=VMEM_LIMIT),
        name="swa",
    )(q, kv, kv, bias, sink_rows)


def _mla_kernel(q_ref, qn_ref, kraw_ref, ktail_ref, wuv_ref, o_ref, k_sc, sa_sc, sb_sc, acc_sc, mx_sc, *, tq, rows):
    i = pl.program_id(1)
    cols = MLA_HEADS * tq
    nsub = tq // BLOCK
    seq = kraw_ref.shape[1]
    q = q_ref[0].reshape(cols, MLA_QK)
    n_groups = ((i + 1) * tq + rows - 1) // rows
    nt_dims = (((1,), (1,)), ((), ()))
    tn_dims = (((0,), (0,)), ((), ()))

    def fold(x, op):
        return op(x.reshape(x.shape[0] // 8, 8, x.shape[1]), axis=0)

    n_parts = rows // MLA_PART

    def scores_part(g, buf, c, qq=q):
        start = pl.multiple_of(g * rows + c * MLA_PART, MLA_PART)
        s = lax.dot_general(k_sc[pl.ds(start, MLA_PART), :], qq, nt_dims, preferred_element_type=F32)
        buf[c * MLA_PART:(c + 1) * MLA_PART, :] = s
        return fold(s, jnp.max)

    def rescale(mx8, m_run):
        m_new = jnp.maximum(m_run, jnp.max(mx8, axis=0, keepdims=True))
        acc_sc[...] = acc_sc[...] * jnp.exp2(m_run - m_new)
        return m_new

    def weights_part(g, buf, c, m_new):
        start = pl.multiple_of(g * rows + c * MLA_PART, MLA_PART)
        p = jnp.exp2(buf[c * MLA_PART:(c + 1) * MLA_PART, :] - m_new).astype(BF16)
        v = k_sc[pl.ds(start, MLA_PART), 0:MLA_ACC]
        acc_sc[...] += lax.dot_general(v, p, tn_dims, preferred_element_type=F32)

    def scores(g, buf):
        return functools.reduce(jnp.maximum, [scores_part(g, buf, c) for c in range(n_parts)])

    def weights(g, buf, mx8, m_run):
        m_new = rescale(mx8, m_run)
        for c in range(n_parts):
            weights_part(g, buf, c, m_new)
        return m_new

    def overlapped(g_next, buf_next, g, buf, mx8, m_run):
        m_new = rescale(mx8, m_run)
        parts = []
        for c in range(n_parts):
            parts.append(scores_part(g_next, buf_next, c))
            weights_part(g, buf, c, m_new)
        return functools.reduce(jnp.maximum, parts), m_new

    @pl.when(i == 0)
    def _():
        k_sc[0:seq, :] = kraw_ref[0]
        k_sc[seq:, :] = ktail_ref[...]
        mx_sc[...] = scores(0, sa_sc)

    acc_sc[...] = jnp.zeros((MLA_ACC, cols), F32)
    m0 = jnp.full((1, cols), NEG, F32)

    def pair(k, carry):
        mxa, m_run = carry
        mxb, m_run = overlapped(2 * k + 1, sb_sc, 2 * k, sa_sc, mxa, m_run)
        mxa, m_run = overlapped(2 * k + 2, sa_sc, 2 * k + 1, sb_sc, mxb, m_run)
        return mxa, m_run

    n_pairs = (n_groups - 1) // 2
    mxa, m_run = lax.fori_loop(0, n_pairs, pair, (mx_sc[...], m0))

    def one_left(m_run):
        return weights(n_groups - 1, sa_sc, mxa, m_run)

    def two_left(m_run):
        mxb, m_run = overlapped(n_groups - 1, sb_sc, n_groups - 2, sa_sc, mxa, m_run)
        return weights(n_groups - 1, sb_sc, mxb, m_run)

    m_run = lax.cond(n_groups - 1 == 2 * n_pairs, one_left, two_left, m_run)

    lane = lax.broadcasted_iota(jnp.int32, (1, MLA_QK), 1)
    onehot = (lane >= MLA_KV_RANK + 2) & (lane < MLA_KV_RANK + 2 + MLA_BLOCKS)
    hcols = MLA_HEADS * BLOCK
    causal = lax.broadcasted_iota(jnp.int32, (BLOCK, hcols), 0) <= (
        lax.broadcasted_iota(jnp.int32, (BLOCK, hcols), 1) & (BLOCK - 1))
    row0s = [pl.multiple_of((i * nsub + r) * BLOCK, BLOCK) for r in range(nsub)]
    lanes = [[slice(hd * tq + r * BLOCK, hd * tq + (r + 1) * BLOCK) for hd in range(MLA_HEADS)]
             for r in range(nsub)]
    qn = qn_ref[0].reshape(cols, MLA_QK)
    ahead = [scores_part(0, sa_sc, 0, qn)]
    sds = []
    for r in range(nsub):
        qd = jnp.concatenate([q[ln, :] for ln in lanes[r]], axis=0)
        qd = jnp.where(onehot, jnp.zeros((), BF16), qd)
        sd = lax.dot_general(k_sc[pl.ds(row0s[r], BLOCK), :], qd, nt_dims, preferred_element_type=F32)
        sds.append(jnp.where(causal, sd, NEG))
    ps, alphas = [], []
    for r in range(nsub):
        m_old = jnp.concatenate([m_run[:, ln] for ln in lanes[r]], axis=1)
        m_new = jnp.maximum(m_old, jnp.max(sds[r], axis=0, keepdims=True))
        ps.append(jnp.exp2(sds[r] - m_new).astype(BF16))
        alphas.append(jnp.exp2(m_old - m_new))
    ahead += [scores_part(0, sa_sc, c, qn) for c in range(1, n_parts)]
    mx_sc[...] = functools.reduce(jnp.maximum, ahead)
    pvs = [lax.dot_general(k_sc[pl.ds(row0s[r], BLOCK), 0:MLA_ACC], ps[r], tn_dims,
                           preferred_element_type=F32) for r in range(nsub)]
    o_parts = []
    for r in range(nsub):
        acc = jnp.concatenate([acc_sc[:, ln] for ln in lanes[r]], axis=1) * alphas[r] + pvs[r]
        o_parts.append((acc[0:MLA_KV_RANK, :] / acc[MLA_KV_RANK:MLA_KV_RANK + 1, :]).astype(BF16))
    ys = [[lax.dot_general(o_parts[r][:, hd * BLOCK:(hd + 1) * BLOCK], wuv_ref[0, hd], tn_dims,
                           preferred_element_type=F32) for hd in range(MLA_HEADS)] for r in range(nsub)]
    o_ref[0] = jnp.concatenate([functools.reduce(jnp.add, ys[r]) for r in range(nsub)], axis=0)


def _mla(qm, kc, key_tail, wuv_heads, layer, *, tq=384, rows=MLA_KEY_ROWS):
    bsz, _, seq, _ = qm.shape
    seq_k = seq + key_tail.shape[0]
    cols = MLA_HEADS * tq
    n_blocks = seq // tq
    return pl.pallas_call(
        functools.partial(_mla_kernel, tq=tq, rows=rows),
        grid=(bsz, seq // tq),
        in_specs=[
            pl.BlockSpec((1, MLA_HEADS, tq, MLA_QK), lambda b, i: (b, 0, i, 0)),
            pl.BlockSpec((1, MLA_HEADS, tq, MLA_QK), lambda b, i: (b, 0, jnp.minimum(i + 1, n_blocks - 1), 0)),
            pl.BlockSpec((1, seq, MLA_QK), lambda b, i: (b, 0, 0)),
            _const_spec((seq_k - seq, MLA_QK)),
            _layer_spec((MLA_HEADS, MLA_KV_RANK, MLA_WIDTH), layer),
        ],
        out_specs=pl.BlockSpec((1, tq, MLA_WIDTH), lambda b, i: (b, i, 0)),
        out_shape=jax.ShapeDtypeStruct((bsz, seq, MLA_WIDTH), F32),
        scratch_shapes=[pltpu.VMEM((seq_k, MLA_QK), BF16), pltpu.VMEM((rows, cols), F32),
                        pltpu.VMEM((rows, cols), F32), pltpu.VMEM((MLA_ACC, cols), F32), pltpu.VMEM((8, cols), F32)],
        compiler_params=pltpu.CompilerParams(
            dimension_semantics=("parallel", "arbitrary"), vmem_limit_bytes=VMEM_LIMIT),
        name="mla",
    )(qm, qm, kc, key_tail, wuv_heads)


def _gelu_tanh(x):
    return 0.5 * x * (1.0 + jnp.tanh(math.sqrt(2.0 / math.pi) * (x + 0.044715 * (x * x * x))))


def _mix_out_kernel(h_ref, head_ref, ys_ref, yw_ref, ym_ref, wglu_ref, bglu_ref, gh_ref, wout_ref, gpm_ref, gpre_ref,
                    wup_ref, wdn_ref, gpost_ref, o_ref, *, ff_chunk, from_x):
    z = _gelu_tanh(jnp.concatenate([ys_ref[0, 0], ys_ref[0, 1]], axis=1))
    gate = jnp.dot(z.astype(BF16), wglu_ref[0], preferred_element_type=F32) + bglu_ref[...]
    y_ssm = z * (1.0 / (1.0 + jnp.exp(-gate)))
    gh = gh_ref[...]
    mixed = jnp.concatenate([
        _rms(y_ssm, gh[:, 0:SSM_WIDTH]),
        _rms(yw_ref[0], gh[:, SSM_WIDTH:SSM_WIDTH + SWA_WIDTH]),
        _rms(ym_ref[0], gh[:, SSM_WIDTH + SWA_WIDTH:]),
    ], axis=1).astype(BF16)
    mo = jnp.dot(mixed, wout_ref[0], preferred_element_type=F32)
    h_rows = _seq_rows(h_ref, head_ref, pl.program_id(1), 0, o_ref.shape[1]) if from_x else h_ref[0]
    h1 = h_rows + _rms(mo, gpm_ref[...])
    a = _rms(h1, gpre_ref[...]).astype(BF16)
    f = jnp.zeros_like(h1)
    for c in range(D_FF // ff_chunk):
        up = jnp.dot(a, wup_ref[0, :, c * ff_chunk:(c + 1) * ff_chunk], preferred_element_type=F32)
        hid = jnp.square(jnp.maximum(up, 0.0)).astype(BF16)
        f += jnp.dot(hid, wdn_ref[0, c * ff_chunk:(c + 1) * ff_chunk, :], preferred_element_type=F32)
    o_ref[0] = h1 + _rms(f, gpost_ref[...])


def _mix_out(h, head, ys, yw, ym, wglu, bglu, gh, wout, gpm, gpre, wup, wdn, gpost, layer, *, from_x=False, skip=0,
             tm=1056, ff_chunk=1024):
    assert not (from_x and skip)
    bsz, seq = yw.shape[0], yw.shape[1]
    out_rows = seq - skip
    if skip:
        el = pl.Element
        first = lambda i: pl.multiple_of(skip + i * tm, BLOCK)
        row = lambda w: pl.BlockSpec((el(1), el(tm), el(w)), lambda b, i: (b, first(i), 0))
        slabs = pl.BlockSpec((el(1), el(2), el(tm), el(128)), lambda b, i: (b, 0, first(i), 0))
    else:
        row = lambda w: pl.BlockSpec((1, tm, w), lambda b, i: (b, i, 0))
        slabs = pl.BlockSpec((1, 2, tm, 128), lambda b, i: (b, 0, i, 0))
    return pl.pallas_call(
        functools.partial(_mix_out_kernel, ff_chunk=ff_chunk, from_x=from_x),
        grid=(bsz, out_rows // tm),
        in_specs=[
            _seq_window_spec(tm, D_MODEL) if from_x else row(D_MODEL), _const_spec((BLOCK, D_MODEL)),
            slabs, row(SWA_WIDTH), row(MLA_WIDTH),
            _layer_spec((SSM_WIDTH, SSM_WIDTH), layer), _const_spec((1, SSM_WIDTH)), _const_spec((1, D_MODEL)),
            _layer_spec((D_MODEL, D_MODEL), layer), _const_spec((1, D_MODEL)), _const_spec((1, D_MODEL)),
            _layer_spec((D_MODEL, D_FF), layer), _layer_spec((D_FF, D_MODEL), layer), _const_spec((1, D_MODEL)),
        ],
        out_specs=pl.BlockSpec((1, tm, D_MODEL), lambda b, i: (b, i, 0)),
        out_shape=jax.ShapeDtypeStruct((bsz, out_rows, D_MODEL), F32),
        compiler_params=pltpu.CompilerParams(
            dimension_semantics=("parallel", "parallel"), vmem_limit_bytes=VMEM_LIMIT),
        name="mix_out",
    )(h, head, ys, yw, ym, wglu, bglu, gh, wout, gpm, gpre, wup, wdn, gpost)


def _rot_half_cols(w):
    half = w.shape[-1] // 2
    return jnp.concatenate([-w[..., half:], w[..., :half]], axis=-1)


def _stacked_weights(w_in, w_uq, w_uk, w_uv):
    depth = w_in.shape[0]
    w_in, w_uq, w_uk, w_uv = (w.astype(BF16) for w in (w_in, w_uq, w_uk, w_uv))
    z = lambda n: jnp.zeros((depth, D_MODEL, n), BF16)
    s_cq = SSM_WIDTH + SWA_WIDTH + 2 * SWA_KV_WIDTH
    s_ckv = s_cq + MLA_Q_RANK
    s_kr = s_ckv + MLA_KV_RANK
    kr = w_in[:, :, s_kr:s_kr + MLA_ROPE]
    w1 = jnp.concatenate([
        w_in[:, :, :s_ckv], z(C_CKV - C_CQ - MLA_Q_RANK),
        w_in[:, :, s_ckv:s_kr], kr, _rot_half_cols(kr), z(C_END - C_KR - 2 * MLA_ROPE)], axis=2)
    uq = w_uq.reshape(depth, MLA_Q_RANK, MLA_HEADS, MLA_NOPE + MLA_ROPE)
    rope = uq[..., MLA_NOPE:]
    wuq = jnp.concatenate([
        uq[..., :MLA_NOPE].reshape(depth, MLA_Q_RANK, -1), rope.reshape(depth, MLA_Q_RANK, -1),
        _rot_half_cols(rope).reshape(depth, MLA_Q_RANK, -1)], axis=2)
    wuq = jnp.pad(wuq, ((0, 0), (0, 256 - MLA_Q_RANK), (0, 0)))
    uk = w_uk.reshape(depth, MLA_KV_RANK, MLA_HEADS, MLA_NOPE).transpose(0, 2, 3, 1)
    wuk_bd = _group_block_diag(uk)
    uv = w_uv.reshape(depth, MLA_KV_RANK, MLA_HEADS, MLA_V).transpose(0, 2, 1, 3)
    wuv_heads = _group_block_diag(uv).reshape(depth, MLA_HEADS, MLA_KV_RANK, MLA_WIDTH)
    return w1, wuq, wuk_bd, wuv_heads


def _constant_tables(seq):
    half = MLA_ROPE // 2
    inv_freq = ROPE_THETA ** (-np.arange(half, dtype=np.float64) / half)
    ang = (np.arange(seq) - PAD)[:, None] * inv_freq[None, :]
    cos_t = np.tile(np.cos(ang), (1, 2 * MLA_HEADS)).astype(np.float32)
    sin_t = np.tile(np.sin(ang), (1, 2 * MLA_HEADS)).astype(np.float32)
    seq_k = -(-seq // MLA_KEY_ROWS) * MLA_KEY_ROWS
    kblk = np.arange(seq_k)[:, None] // BLOCK
    bcol = np.arange(MLA_BLOCKS)[None, :]
    one = np.ones((seq_k, 1))
    fill = np.zeros((seq_k, 128 - 2 - MLA_BLOCKS))
    kx_t = np.concatenate([one, np.where(np.arange(seq_k)[:, None] < PAD, NEG, 0.0), kblk == bcol, fill], axis=1)
    qx_t = np.concatenate([0.0 * one, one, np.where(bcol >= kblk, NEG, 0.0), fill], axis=1)[:seq]
    key_tail = np.zeros((seq_k - seq, MLA_QK))
    key_tail[:, MLA_KV_RANK:MLA_KV_RANK + MLA_EXT] = kx_t[seq:, :MLA_EXT]
    slopes = 2.0 ** (-8.0 * np.arange(1, SWA_HEADS + 1) / SWA_HEADS)
    slope_rows = np.repeat(slopes.reshape(SWA_KV_HEADS, SWA_GROUP), BLOCK, axis=1)[:, None, :]
    dist = (np.arange(BLOCK)[None, :] + BLOCK) - np.arange(2 * BLOCK)[:, None]
    in_window = np.tile((dist >= 0) & (dist < BLOCK), (1, SWA_GROUP))[None, None]
    alibi = (-LOG2E * slope_rows * np.tile(dist, (1, SWA_GROUP))[None])[None]
    key_pos = (np.arange(3)[:, None] - 1) * BLOCK + np.arange(2 * BLOCK)[None, :]
    key_ok = ((key_pos >= PAD) | (np.arange(3)[:, None] >= 2))[:, None, :, None]
    swa_bias = np.where(in_window & key_ok, alibi, NEG).astype(np.float32)
    bf16 = lambda t: jnp.asarray(t, dtype=BF16)
    return (jnp.asarray(cos_t), jnp.asarray(sin_t), bf16(qx_t), bf16(kx_t[:seq]), bf16(key_tail),
            jnp.asarray(swa_bias))


def kernel(x, meta_tokens, norm_pre_mix, norm_post_mix, norm_pre_mlp, norm_post_mlp, w_in, w_out, norm_heads,
           ssm_a_re, ssm_a_im, ssm_log_step, ssm_b_re, ssm_b_im, ssm_c_re, ssm_c_im, ssm_d, ssm_w_glu,
           ssm_b_glu, swa_sinks, mla_q_norm, mla_kv_norm, mla_w_uq, mla_w_uk, mla_w_uv, w_mlp_up, w_mlp_down):
    bsz, seq_real, d = x.shape
    depth = w_in.shape[0]
    head = jnp.concatenate([jnp.zeros((PAD, d), x.dtype), meta_tokens.astype(x.dtype)], axis=0)
    seq = BLOCK + seq_real
    h = x

    cos_t, sin_t, qx_t, kx_t, key_tail, swa_bias = _constant_tables(seq)
    per_row = lambda v: jnp.repeat(v.reshape(SWA_KV_HEADS, SWA_GROUP), BLOCK, axis=1)[:, None, :]

    r_mat, p_mat, q_mat, lam = _s5_operators(
        ssm_a_re, ssm_a_im, ssm_log_step, ssm_b_re, ssm_b_im, ssm_c_re, ssm_c_im)
    row = lambda v: v.reshape(1, -1).astype(F32)
    wglu, wout, wup, wdn = (w.astype(BF16) for w in (ssm_w_glu, w_out, w_mlp_up, w_mlp_down))
    w1, wuq, wuk_bd, wuv_heads = _stacked_weights(w_in, mla_w_uq, mla_w_uk, mla_w_uv)

    for l in range(depth):
        gq = jnp.concatenate([mla_q_norm[l], jnp.zeros((256 - MLA_Q_RANK,), F32)]).reshape(1, 256)
        u, q, kv, qm, kc = _mix_in(h, head, row(norm_pre_mix[l]), w1, gq, row(mla_kv_norm[l]), wuq, wuk_bd,
                                   cos_t, sin_t, qx_t, kx_t, l, from_x=l == 0)
        d_fold = jnp.tile(ssm_d[l].astype(F32), SSM_CHUNK).reshape(1, SSM_FOLD)
        y_ssm = _s5_sequence(u, r_mat, p_mat, q_mat, lam, d_fold, l)
        y_swa = _swa(q, kv, swa_bias, per_row(LOG2E * swa_sinks[l].astype(F32)))
        y_mla = _mla(qm, kc, key_tail, wuv_heads, l)
        last = l == depth - 1
        h = _mix_out(
            h, head, y_ssm, y_swa, y_mla,
            wglu, row(ssm_b_glu[l]), row(norm_heads[l]), wout, row(norm_post_mix[l]), row(norm_pre_mlp[l]),
            wup, wdn, row(norm_post_mlp[l]), l, from_x=l == 0, skip=PAD + N_META if last else 0,
            tm=1024 if last else 1056)
    return h
```

```python
import functools
import math

import jax
import jax.numpy as jnp
import numpy as np
from jax import lax
from jax.experimental import pallas as pl
from jax.experimental.pallas import tpu as pltpu

F32 = jnp.float32
BF16 = jnp.bfloat16

D_MODEL = 1024
N_META = 16
BLOCK = 128
PAD = BLOCK - N_META
SSM_GROUPS = 16
SSM_GROUP_CH = 16
SSM_WIDTH = SSM_GROUPS * SSM_GROUP_CH
SSM_STATE = 64
SWA_HEADS = 8
SWA_KV_HEADS = 2
SWA_GROUP = SWA_HEADS // SWA_KV_HEADS
HEAD_DIM = 64
SWA_WIDTH = SWA_HEADS * HEAD_DIM
SWA_KV_WIDTH = SWA_KV_HEADS * HEAD_DIM
SWA_KV_COLS = SWA_KV_WIDTH + SWA_KV_HEADS * BLOCK
MLA_HEADS = 4
MLA_Q_RANK = 192
MLA_KV_RANK = 128
MLA_NOPE = 64
MLA_ROPE = 32
MLA_V = 64
MLA_WIDTH = MLA_HEADS * MLA_V
MLA_BLOCKS = 40
MLA_EXT = 48
MLA_ACC = MLA_KV_RANK + 8
LOG2E = 1.4426950408889634
MLA_QK = MLA_KV_RANK + MLA_EXT + MLA_ROPE
ROPE_THETA = 10000.0
D_FF = 4 * D_MODEL
EPS = 1e-6
NEG = -1e30

SSM_CHUNK = 8
SSM_FOLD = SSM_CHUNK * SSM_WIDTH
SSM_MODES = SSM_GROUPS * SSM_STATE

C_U, C_Q, C_K, C_CQ, C_CKV, C_KR, C_END = 0, 256, 768, 1024, 1280, 1408, 1536

MLA_KEY_ROWS = 512
MLA_PART = 256

VMEM_LIMIT = 58 * 1024 * 1024


def _rms(x, g, n=None):
    n = x.shape[-1] if n is None else n
    ms = jnp.sum(x * x, axis=-1, keepdims=True) * (1.0 / n)
    return x * lax.rsqrt(ms + EPS) * g


def _const_spec(shape):
    nd = len(shape)
    return pl.BlockSpec(shape, lambda *_: (0,) * nd, pipeline_mode=pl.Buffered(1))


def _seq_window_spec(tm, width):
    el = pl.Element
    return pl.BlockSpec((el(1), el(tm), el(width)),
                        lambda b, i: (b, pl.multiple_of(jnp.maximum(i * tm - BLOCK, 0), 8), 0))


def _seq_rows(x_ref, head_ref, i, lo, n):
    first = jnp.where(i == 0, 1, 0)
    if lo == 0:
        top = jnp.where(i == 0, head_ref[...], x_ref[0, 0:BLOCK, :])
        rest = x_ref[0, pl.ds(pl.multiple_of(BLOCK - BLOCK * first, 8), n - BLOCK), :]
        return jnp.concatenate([top, rest], axis=0)
    return x_ref[0, pl.ds(pl.multiple_of(lo - BLOCK * first, 8), n), :]


def _layer_spec(shape, layer):
    nd = len(shape)
    return pl.BlockSpec((1,) + shape, lambda *_: (layer,) + (0,) * nd, pipeline_mode=pl.Buffered(1))


def _mix_in_kernel(h_ref, head_ref, g_ref, w1_ref, gq_ref, gkv_ref, wuq_ref, wuk_ref, cos_ref, sin_ref, qx_ref,
                   kx_ref, u_ref, q_ref, kv_ref, qm_ref, kc_ref, *, tm, nsub, from_x):
    j = pl.program_id(1)
    ts = tm // nsub
    scale = LOG2E * (MLA_NOPE + MLA_ROPE) ** -0.5
    ext = slice(MLA_KV_RANK, MLA_KV_RANK + MLA_EXT)
    rot = slice(MLA_KV_RANK + MLA_EXT, MLA_QK)
    for sub in range(nsub):
        rows = slice(sub * ts, (sub + 1) * ts)
        h_rows = _seq_rows(h_ref, head_ref, j, sub * ts, ts) if from_x else h_ref[0, rows, :]
        a = _rms(h_rows, g_ref[...]).astype(BF16)
        proj = jnp.dot(a, w1_ref[0], preferred_element_type=F32)
        row = j * tm + sub * ts + lax.broadcasted_iota(jnp.int32, (ts, 1), 0)
        for c in range(2):
            u_ref[0, c, rows, :] = jnp.where(row >= PAD, proj[:, C_U + c * 128:C_U + (c + 1) * 128], 0.0)
        q_ref[0, rows, :] = (proj[:, C_Q:C_K] * (LOG2E * HEAD_DIM ** -0.5)).astype(BF16)
        kv_ref[0, rows, 0:SWA_KV_WIDTH] = proj[:, C_K:C_K + SWA_KV_WIDTH].astype(BF16)
        ones_col = jnp.where(lax.broadcasted_iota(jnp.int32, (ts, HEAD_DIM), 1) == 0, 1.0, 0.0).astype(BF16)
        for hk in range(SWA_KV_HEADS):
            lo = SWA_KV_WIDTH + hk * BLOCK
            v_lo = C_K + SWA_KV_WIDTH + hk * HEAD_DIM
            kv_ref[0, rows, lo:lo + HEAD_DIM] = proj[:, v_lo:v_lo + HEAD_DIM].astype(BF16)
            kv_ref[0, rows, lo + HEAD_DIM:lo + BLOCK] = ones_col

        cos = cos_ref[rows, :]
        sin = sin_ref[rows, :]
        qn = _rms(proj[:, C_CQ:C_CKV], gq_ref[...], n=MLA_Q_RANK).astype(BF16)
        q2 = jnp.dot(qn, wuq_ref[0], preferred_element_type=F32)
        qr = (q2[:, 256:384] * cos + q2[:, 384:512] * sin) * scale
        qa = jnp.dot(q2[:, 0:256].astype(BF16), wuk_ref[0], preferred_element_type=F32) * scale
        c = _rms(proj[:, C_CKV:C_KR], gkv_ref[...])
        krr = proj[:, C_KR:C_END]
        kr = krr[:, 0:MLA_ROPE] * cos[:, 0:MLA_ROPE] + krr[:, MLA_ROPE:2 * MLA_ROPE] * sin[:, 0:MLA_ROPE]
        for hd in range(MLA_HEADS):
            qm_ref[0, hd, rows, 0:MLA_KV_RANK] = qa[:, hd * MLA_KV_RANK:(hd + 1) * MLA_KV_RANK].astype(BF16)
            qm_ref[0, hd, rows, ext] = qx_ref[rows, 0:MLA_EXT]
            qm_ref[0, hd, rows, rot] = qr[:, hd * MLA_ROPE:(hd + 1) * MLA_ROPE].astype(BF16)
        kc_ref[0, rows, 0:MLA_KV_RANK] = c.astype(BF16)
        kc_ref[0, rows, ext] = kx_ref[rows, 0:MLA_EXT]
        kc_ref[0, rows, rot] = kr.astype(BF16)


def _mix_in(h, head, g, w1, gq, gkv, wuq, wuk, cos_t, sin_t, qx_t, kx_t, layer, *, from_x, tm=2112, nsub=4):
    bsz = h.shape[0]
    seq = cos_t.shape[0]
    grid = (bsz, seq // tm)
    row3 = lambda b, j: (b, j, 0)
    return pl.pallas_call(
        functools.partial(_mix_in_kernel, tm=tm, nsub=nsub, from_x=from_x),
        grid=grid,
        in_specs=[
            _seq_window_spec(tm, D_MODEL) if from_x else pl.BlockSpec((1, tm, D_MODEL), row3),
            _const_spec((BLOCK, D_MODEL)),
            _const_spec((1, D_MODEL)),
            _layer_spec((D_MODEL, C_END), layer),
            _const_spec((1, 256)),
            _const_spec((1, MLA_KV_RANK)),
            _layer_spec((256, 512), layer),
            _layer_spec((256, 512), layer),
            pl.BlockSpec((tm, 128), lambda b, j: (j, 0)),
            pl.BlockSpec((tm, 128), lambda b, j: (j, 0)),
            pl.BlockSpec((tm, 128), lambda b, j: (j, 0)),
            pl.BlockSpec((tm, 128), lambda b, j: (j, 0)),
        ],
        out_specs=[
            pl.BlockSpec((1, 2, tm, 128), lambda b, j: (b, 0, j, 0)),
            pl.BlockSpec((1, tm, SWA_WIDTH), row3),
            pl.BlockSpec((1, tm, SWA_KV_COLS), row3),
            pl.BlockSpec((1, MLA_HEADS, tm, MLA_QK), lambda b, j: (b, 0, j, 0)),
            pl.BlockSpec((1, tm, MLA_QK), row3),
        ],
        out_shape=[
            jax.ShapeDtypeStruct((bsz, 2, seq, 128), F32),
            jax.ShapeDtypeStruct((bsz, seq, SWA_WIDTH), BF16),
            jax.ShapeDtypeStruct((bsz, seq, SWA_KV_COLS), BF16),
            jax.ShapeDtypeStruct((bsz, MLA_HEADS, seq, MLA_QK), BF16),
            jax.ShapeDtypeStruct((bsz, seq, MLA_QK), BF16),
        ],
        compiler_params=pltpu.CompilerParams(
            dimension_semantics=("parallel", "parallel"), vmem_limit_bytes=VMEM_LIMIT),
        name="mix_in",
    )(h, head, g, w1, gq, gkv, wuq, wuk, cos_t, sin_t, qx_t, kx_t)


def _split_bf16(x):
    hi = x.astype(BF16)
    lo = (x - hi.astype(F32)).astype(BF16)
    return hi, lo


def _bdot3(a, b):
    ah, al = _split_bf16(a)
    bh, bl = _split_bf16(b)
    dn = (((2,), (2,)), ((0,), (0,)))
    f = lambda x, y: lax.dot_general(x, y, dn, preferred_element_type=F32)
    return f(ah, bh) + f(ah, bl) + f(al, bh)


def _expand_groups(a2, row_shift, col_shift, n_cols):
    rows, w = a2.shape
    tile = (lax.broadcasted_iota(jnp.int32, (w, n_cols), 1) & (w - 1)) == lax.broadcasted_iota(
        jnp.int32, (w, n_cols), 0)
    wide = jnp.dot(a2.astype(BF16), jnp.where(tile, 1.0, 0.0).astype(BF16), preferred_element_type=F32)
    same = (lax.broadcasted_iota(jnp.int32, (rows, n_cols), 0) >> row_shift) == (
        lax.broadcasted_iota(jnp.int32, (rows, n_cols), 1) >> col_shift)
    return jnp.where(same, wide, 0.0)


def _s5_operator_kernel(ar_ref, ai_ref, ls_ref, btr_ref, bti_ref, cr_ref, ci_ref,
                        r_ref, p_ref, q_ref, l8r_ref, l8i_ref, pwr_sc, pwi_sc, kk_sc):
    s = pl.program_id(1)
    ar, ai = ar_ref[0], ai_ref[0]
    step = jnp.exp(ls_ref[0])
    lr = jnp.exp(ar * step) * jnp.cos(ai * step)
    li = jnp.exp(ar * step) * jnp.sin(ai * step)

    den = ar * ar + ai * ai
    nr, ni = lr - 1.0, li
    coef_re = (nr * ar + ni * ai) / den
    coef_im = (ni * ar - nr * ai) / den
    btr, bti = btr_ref[0], bti_ref[0]
    bb_re = coef_re * btr - coef_im * bti
    bb_im = coef_re * bti + coef_im * btr
    cr, ci = cr_ref[0], ci_ref[0]

    def c_lam(k):
        qr_, qi_ = pwr_sc[k], pwi_sc[k]
        return cr * qr_ - ci * qi_, cr * qi_ + ci * qr_

    gc = SSM_GROUPS * SSM_GROUP_CH

    @pl.when(s == 0)
    def _():
        pr, pi = jnp.ones_like(lr), jnp.zeros_like(li)
        for k in range(SSM_CHUNK + 1):
            pwr_sc[k] = pr
            pwi_sc[k] = pi
            pr, pi = pr * lr - pi * li, pr * li + pi * lr
        for k in range(SSM_CHUNK):
            clr, cli = c_lam(k)
            kk_sc[k] = (_bdot3(bb_re, clr) - _bdot3(bb_im, cli)).reshape(gc, SSM_GROUP_CH)

    for t in range(SSM_CHUNK):
        lag = t - s
        blk = _expand_groups(kk_sc[jnp.maximum(lag, 0)], 4, 4, gc)
        r_ref[0, :, t * gc:(t + 1) * gc] = jnp.where(lag >= 0, blk, 0.0).astype(BF16)

    pw_r, pw_i = pwr_sc[SSM_CHUNK - 1 - s], pwi_sc[SSM_CHUNK - 1 - s]
    p_re = (pw_r * bb_re - pw_i * bb_im).reshape(gc, SSM_STATE)
    p_im = (pw_r * bb_im + pw_i * bb_re).reshape(gc, SSM_STATE)
    p_ref[0, :, 0:SSM_MODES] = _expand_groups(p_re, 4, 6, SSM_MODES).astype(BF16)
    p_ref[0, :, SSM_MODES:2 * SSM_MODES] = _expand_groups(p_im, 4, 6, SSM_MODES).astype(BF16)

    clr, cli = c_lam(s + 1)
    q_ref[0, 0:SSM_MODES, :] = _expand_groups(clr.reshape(gc, SSM_STATE), 4, 6, SSM_MODES).T.astype(BF16)
    q_ref[0, SSM_MODES:2 * SSM_MODES, :] = (
        -_expand_groups(cli.reshape(gc, SSM_STATE), 4, 6, SSM_MODES)).T.astype(BF16)

    l8r_ref[0] = pwr_sc[SSM_CHUNK]
    l8i_ref[0] = pwi_sc[SSM_CHUNK]


def _s5_operators(a_re, a_im, log_step, b_re, b_im, c_re, c_im):
    depth = a_re.shape[0]
    G, P, C, T = SSM_GROUPS, SSM_STATE, SSM_GROUP_CH, SSM_CHUNK
    lay = lambda *shape: pl.BlockSpec((1,) + shape, lambda l, s: (l,) + (0,) * len(shape))
    ls = jnp.broadcast_to(log_step[:, :, None, None], (depth, G, 1, P))
    args = (a_re[:, :, None, :], a_im[:, :, None, :], ls,
            b_re.transpose(0, 1, 3, 2), b_im.transpose(0, 1, 3, 2), c_re, c_im)
    r_mat, p_mat, q_mat, l8r, l8i = pl.pallas_call(
        _s5_operator_kernel,
        grid=(depth, T),
        in_specs=[lay(G, 1, P)] * 3 + [lay(G, C, P)] * 4,
        out_specs=[
            pl.BlockSpec((1, SSM_WIDTH, SSM_FOLD), lambda l, s: (l, s, 0)),
            pl.BlockSpec((1, SSM_WIDTH, 2 * SSM_MODES), lambda l, s: (l, s, 0)),
            pl.BlockSpec((1, 2 * SSM_MODES, SSM_WIDTH), lambda l, s: (l, 0, s)),
            lay(G, 1, P), lay(G, 1, P)],
        out_shape=[
            jax.ShapeDtypeStruct((depth, SSM_FOLD, SSM_FOLD), BF16),
            jax.ShapeDtypeStruct((depth, SSM_FOLD, 2 * SSM_MODES), BF16),
            jax.ShapeDtypeStruct((depth, 2 * SSM_MODES, SSM_FOLD), BF16),
            jax.ShapeDtypeStruct((depth, G, 1, P), F32),
            jax.ShapeDtypeStruct((depth, G, 1, P), F32),
        ],
        scratch_shapes=[pltpu.VMEM((T + 1, G, 1, P), F32), pltpu.VMEM((T + 1, G, 1, P), F32),
                        pltpu.VMEM((T, G * C, C), F32)],
        compiler_params=pltpu.CompilerParams(dimension_semantics=("parallel", "arbitrary")),
        name="s5_operators",
    )(*args)
    lam = jnp.concatenate([l8r.reshape(depth, 1, SSM_MODES), l8i.reshape(depth, 1, SSM_MODES)], axis=2)
    return r_mat, p_mat, q_mat, lam


def _group_block_diag(x):
    *lead, G, r, c = x.shape
    eye = jnp.eye(G, dtype=x.dtype)
    y = x[..., :, :, None, :] * eye[:, None, :, None]
    return y.reshape(*lead, G * r, G * c)


def _fold_tokens(u_ref, n_chunks):
    return jnp.concatenate(
        [u_ref[0, c, pl.ds(s, n_chunks, stride=SSM_CHUNK), :] for s in range(SSM_CHUNK) for c in range(2)], axis=1)


def _s5_state_kernel(u_ref, p_ref, lam_ref, x_ref, s_sc, x_sc, *, n_chunks):
    u = _fold_tokens(u_ref, n_chunks).astype(BF16)
    s_sc[...] = jnp.dot(u, p_ref[0], preferred_element_type=F32)
    lr = lam_ref[0, :, 0:SSM_MODES]
    li = lam_ref[0, :, SSM_MODES:2 * SSM_MODES]

    def body(n, carry):
        xr, xi = carry
        x_sc[pl.ds(n, 1), 0:SSM_MODES] = xr
        x_sc[pl.ds(n, 1), SSM_MODES:2 * SSM_MODES] = xi
        sr = s_sc[pl.ds(n, 1), 0:SSM_MODES]
        si = s_sc[pl.ds(n, 1), SSM_MODES:2 * SSM_MODES]
        return lr * xr - li * xi + sr, lr * xi + li * xr + si

    z = jnp.zeros((1, SSM_MODES), F32)
    lax.fori_loop(0, n_chunks, body, (z, z))
    x_ref[0] = x_sc[...].astype(BF16)


def _s5_out_kernel(u_ref, x_ref, r_ref, q_ref, d_ref, y_ref, *, n_chunks):
    u = _fold_tokens(u_ref, n_chunks)
    ub = u.astype(BF16)
    x = x_ref[0]
    for t in range(SSM_CHUNK):
        cols = slice(t * SSM_WIDTH, (t + 1) * SSM_WIDTH)
        rows = (t + 1) * SSM_WIDTH
        y = jnp.dot(ub[:, 0:rows], r_ref[0, 0:rows, cols], preferred_element_type=F32)
        y += jnp.dot(x, q_ref[0, :, cols], preferred_element_type=F32)
        y += d_ref[:, cols] * u[:, cols]
        for c in range(2):
            y_ref[0, c, pl.ds(t, n_chunks, stride=SSM_CHUNK), :] = y[:, c * 128:(c + 1) * 128]


def _s5_sequence(u, r_mat, p_mat, q_mat, lam, d_fold, layer):
    bsz, _, seq, _ = u.shape
    n_chunks = seq // SSM_CHUNK
    params = pltpu.CompilerParams(dimension_semantics=("parallel",), vmem_limit_bytes=VMEM_LIMIT)
    slabs = pl.BlockSpec((1, 2, seq, 128), lambda b: (b, 0, 0, 0))
    states = pl.BlockSpec((1, n_chunks, 2 * SSM_MODES), lambda b: (b, 0, 0))
    x_prev = pl.pallas_call(
        functools.partial(_s5_state_kernel, n_chunks=n_chunks), grid=(bsz,),
        in_specs=[slabs, _layer_spec((SSM_FOLD, 2 * SSM_MODES), layer), _layer_spec((1, 2 * SSM_MODES), layer)],
        out_specs=states,
        out_shape=jax.ShapeDtypeStruct((bsz, n_chunks, 2 * SSM_MODES), BF16),
        scratch_shapes=[pltpu.VMEM((n_chunks, 2 * SSM_MODES), F32), pltpu.VMEM((n_chunks, 2 * SSM_MODES), F32)],
        compiler_params=params, name="s5_state",
    )(u, p_mat, lam)
    return pl.pallas_call(
        functools.partial(_s5_out_kernel, n_chunks=n_chunks), grid=(bsz,),
        in_specs=[slabs, states, _layer_spec((SSM_FOLD, SSM_FOLD), layer),
                  _layer_spec((2 * SSM_MODES, SSM_FOLD), layer), _const_spec((1, SSM_FOLD))],
        out_specs=slabs,
        out_shape=jax.ShapeDtypeStruct((bsz, 2, seq, 128), F32),
        compiler_params=params, name="s5_out",
    )(u, x_prev, r_mat, q_mat, d_fold)


def _swa_kernel(q_ref, kvp_ref, kvc_ref, bias_ref, sink_ref, o_ref, *, nsub):
    i = pl.program_id(1)
    nt_dims = (((1,), (1,)), ((), ()))
    tn_dims = (((0,), (0,)), ((), ()))
    chains = [(r, hk) for r in range(nsub) for hk in range(SWA_KV_HEADS)]
    scores, vcats = [], []
    for r, hk in chains:
        rows = slice(r * BLOCK, (r + 1) * BLOCK)
        q = q_ref[0, rows, :]
        kvc = kvc_ref[0, rows, :]
        kvp = kvp_ref[0] if r == 0 else kvc_ref[0, (r - 1) * BLOCK:r * BLOCK, :]
        ks = slice(hk * HEAD_DIM, (hk + 1) * HEAD_DIM)
        vs = slice(SWA_KV_WIDTH + hk * BLOCK, SWA_KV_WIDTH + hk * BLOCK + HEAD_DIM + 8)
        kcat = jnp.concatenate([kvp[:, ks], kvc[:, ks]], axis=0)
        vcats.append(jnp.concatenate([kvp[:, vs], kvc[:, vs]], axis=0))
        qs = jnp.concatenate(
            [q[:, (hk * SWA_GROUP + g) * HEAD_DIM:(hk * SWA_GROUP + g + 1) * HEAD_DIM]
             for g in range(SWA_GROUP)], axis=0)
        table = jnp.minimum(i * nsub + r, 2)
        scores.append(lax.dot_general(kcat, qs, nt_dims, preferred_element_type=F32) + bias_ref[table, hk])
    weights, sink_terms = [], []
    for (r, hk), s in zip(chains, scores):
        sink = sink_ref[hk]
        m = jnp.maximum(jnp.max(s, axis=0, keepdims=True), sink)
        weights.append(jnp.exp2(s - m).astype(BF16))
        sink_terms.append(jnp.exp2(sink - m))
    o_ts = []
    for v, p, sink_term in zip(vcats, weights, sink_terms):
        o_t = lax.dot_general(v, p, tn_dims, preferred_element_type=F32)
        o_ts.append(o_t[0:HEAD_DIM, :] / (o_t[HEAD_DIM:HEAD_DIM + 1, :] + sink_term))
    for r in range(nsub):
        outs = []
        for hk in range(SWA_KV_HEADS):
            o_t = o_ts[r * SWA_KV_HEADS + hk]
            outs += [o_t[:, g * BLOCK:(g + 1) * BLOCK] for g in range(SWA_GROUP)]
        o_ref[0, r * BLOCK:(r + 1) * BLOCK, :] = jnp.concatenate(outs, axis=0).T


def _swa(q, kv, bias, sink_rows, *, nsub=11):
    bsz, seq, _ = q.shape
    tm = nsub * BLOCK
    cols = SWA_GROUP * BLOCK
    return pl.pallas_call(
        functools.partial(_swa_kernel, nsub=nsub),
        grid=(bsz, seq // tm),
        in_specs=[
            pl.BlockSpec((1, tm, SWA_WIDTH), lambda b, i: (b, i, 0)),
            pl.BlockSpec((1, BLOCK, SWA_KV_COLS), lambda b, i: (b, jnp.maximum(nsub * i - 1, 0), 0)),
            pl.BlockSpec((1, tm, SWA_KV_COLS), lambda b, i: (b, i, 0)),
            _const_spec((3, SWA_KV_HEADS, 2 * BLOCK, cols)),
            _const_spec((SWA_KV_HEADS, 1, cols)),
        ],
        out_specs=pl.BlockSpec((1, tm, SWA_WIDTH), lambda b, i: (b, i, 0)),
        out_shape=jax.ShapeDtypeStruct((bsz, seq, SWA_WIDTH), F32),
        compiler_params=pltpu.CompilerParams(
            dimension_semantics=("parallel", "parallel"), vmem_limit_bytes=VMEM_LIMIT),
        name="swa",
    )(q, kv, kv, bias, sink_rows)


def _mla_kernel(q_ref, qn_ref, kraw_ref, ktail_ref, wuv_ref, o_ref, k_sc, sa_sc, sb_sc, acc_sc, mx_sc, *, tq, rows):
    i = pl.program_id(1)
    cols = MLA_HEADS * tq
    nsub = tq // BLOCK
    seq = kraw_ref.shape[1]
    q = q_ref[0].reshape(cols, MLA_QK)
    n_groups = ((i + 1) * tq + rows - 1) // rows
    nt_dims = (((1,), (1,)), ((), ()))
    tn_dims = (((0,), (0,)), ((), ()))

    def fold(x, op):
        return op(x.reshape(x.shape[0] // 8, 8, x.shape[1]), axis=0)

    n_parts = rows // MLA_PART

    def scores_part(g, buf, c, qq=q):
        start = pl.multiple_of(g * rows + c * MLA_PART, MLA_PART)
        s = lax.dot_general(k_sc[pl.ds(start, MLA_PART), :], qq, nt_dims, preferred_element_type=F32)
        buf[c * MLA_PART:(c + 1) * MLA_PART, :] = s
        return fold(s, jnp.max)

    def rescale(mx8, m_run):
        m_new = jnp.maximum(m_run, jnp.max(mx8, axis=0, keepdims=True))
        acc_sc[...] = acc_sc[...] * jnp.exp2(m_run - m_new)
        return m_new

    def weights_part(g, buf, c, m_new):
        start = pl.multiple_of(g * rows + c * MLA_PART, MLA_PART)
        p = jnp.exp2(buf[c * MLA_PART:(c + 1) * MLA_PART, :] - m_new).astype(BF16)
        v = k_sc[pl.ds(start, MLA_PART), 0:MLA_ACC]
        acc_sc[...] += lax.dot_general(v, p, tn_dims, preferred_element_type=F32)

    def scores(g, buf):
        return functools.reduce(jnp.maximum, [scores_part(g, buf, c) for c in range(n_parts)])

    def weights(g, buf, mx8, m_run):
        m_new = rescale(mx8, m_run)
        for c in range(n_parts):
            weights_part(g, buf, c, m_new)
        return m_new

    def overlapped(g_next, buf_next, g, buf, mx8, m_run):
        m_new = rescale(mx8, m_run)
        parts = []
        for c in range(n_parts):
            parts.append(scores_part(g_next, buf_next, c))
            weights_part(g, buf, c, m_new)
        return functools.reduce(jnp.maximum, parts), m_new

    @pl.when(i == 0)
    def _():
        k_sc[0:seq, :] = kraw_ref[0]
        k_sc[seq:, :] = ktail_ref[...]
        mx_sc[...] = scores(0, sa_sc)

    acc_sc[...] = jnp.zeros((MLA_ACC, cols), F32)
    m0 = jnp.full((1, cols), NEG, F32)

    def pair(k, carry):
        mxa, m_run = carry
        mxb, m_run = overlapped(2 * k + 1, sb_sc, 2 * k, sa_sc, mxa, m_run)
        mxa, m_run = overlapped(2 * k + 2, sa_sc, 2 * k + 1, sb_sc, mxb, m_run)
        return mxa, m_run

    n_pairs = (n_groups - 1) // 2
    mxa, m_run = lax.fori_loop(0, n_pairs, pair, (mx_sc[...], m0))

    def one_left(m_run):
        return weights(n_groups - 1, sa_sc, mxa, m_run)

    def two_left(m_run):
        mxb, m_run = overlapped(n_groups - 1, sb_sc, n_groups - 2, sa_sc, mxa, m_run)
        return weights(n_groups - 1, sb_sc, mxb, m_run)

    m_run = lax.cond(n_groups - 1 == 2 * n_pairs, one_left, two_left, m_run)

    lane = lax.broadcasted_iota(jnp.int32, (1, MLA_QK), 1)
    onehot = (lane >= MLA_KV_RANK + 2) & (lane < MLA_KV_RANK + 2 + MLA_BLOCKS)
    hcols = MLA_HEADS * BLOCK
    causal = lax.broadcasted_iota(jnp.int32, (BLOCK, hcols), 0) <= (
        lax.broadcasted_iota(jnp.int32, (BLOCK, hcols), 1) & (BLOCK - 1))
    row0s = [pl.multiple_of((i * nsub + r) * BLOCK, BLOCK) for r in range(nsub)]
    lanes = [[slice(hd * tq + r * BLOCK, hd * tq + (r + 1) * BLOCK) for hd in range(MLA_HEADS)]
             for r in range(nsub)]
    qn = qn_ref[0].reshape(cols, MLA_QK)
    ahead = [scores_part(0, sa_sc, 0, qn)]
    sds = []
    for r in range(nsub):
        qd = jnp.concatenate([q[ln, :] for ln in lanes[r]], axis=0)
        qd = jnp.where(onehot, jnp.zeros((), BF16), qd)
        sd = lax.dot_general(k_sc[pl.ds(row0s[r], BLOCK), :], qd, nt_dims, preferred_element_type=F32)
        sds.append(jnp.where(causal, sd, NEG))
    ps, alphas = [], []
    for r in range(nsub):
        m_old = jnp.concatenate([m_run[:, ln] for ln in lanes[r]], axis=1)
        m_new = jnp.maximum(m_old, jnp.max(sds[r], axis=0, keepdims=True))
        ps.append(jnp.exp2(sds[r] - m_new).astype(BF16))
        alphas.append(jnp.exp2(m_old - m_new))
    ahead += [scores_part(0, sa_sc, c, qn) for c in range(1, n_parts)]
    mx_sc[...] = functools.reduce(jnp.maximum, ahead)
    pvs = [lax.dot_general(k_sc[pl.ds(row0s[r], BLOCK), 0:MLA_ACC], ps[r], tn_dims,
                           preferred_element_type=F32) for r in range(nsub)]
    o_parts = []
    for r in range(nsub):
        acc = jnp.concatenate([acc_sc[:, ln] for ln in lanes[r]], axis=1) * alphas[r] + pvs[r]
        o_parts.append((acc[0:MLA_KV_RANK, :] / acc[MLA_KV_RANK:MLA_KV_RANK + 1, :]).astype(BF16))
    ys = [[lax.dot_general(o_parts[r][:, hd * BLOCK:(hd + 1) * BLOCK], wuv_ref[0, hd], tn_dims,
                           preferred_element_type=F32) for hd in range(MLA_HEADS)] for r in range(nsub)]
    o_ref[0] = jnp.concatenate([functools.reduce(jnp.add, ys[r]) for r in range(nsub)], axis=0)


def _mla(qm, kc, key_tail, wuv_heads, layer, *, tq=384, rows=MLA_KEY_ROWS):
    bsz, _, seq, _ = qm.shape
    seq_k = seq + key_tail.shape[0]
    cols = MLA_HEADS * tq
    n_blocks = seq // tq
    return pl.pallas_call(
        functools.partial(_mla_kernel, tq=tq, rows=rows),
        grid=(bsz, seq // tq),
        in_specs=[
            pl.BlockSpec((1, MLA_HEADS, tq, MLA_QK), lambda b, i: (b, 0, i, 0)),
            pl.BlockSpec((1, MLA_HEADS, tq, MLA_QK), lambda b, i: (b, 0, jnp.minimum(i + 1, n_blocks - 1), 0)),
            pl.BlockSpec((1, seq, MLA_QK), lambda b, i: (b, 0, 0)),
            _const_spec((seq_k - seq, MLA_QK)),
            _layer_spec((MLA_HEADS, MLA_KV_RANK, MLA_WIDTH), layer),
        ],
        out_specs=pl.BlockSpec((1, tq, MLA_WIDTH), lambda b, i: (b, i, 0)),
        out_shape=jax.ShapeDtypeStruct((bsz, seq, MLA_WIDTH), F32),
        scratch_shapes=[pltpu.VMEM((seq_k, MLA_QK), BF16), pltpu.VMEM((rows, cols), F32),
                        pltpu.VMEM((rows, cols), F32), pltpu.VMEM((MLA_ACC, cols), F32), pltpu.VMEM((8, cols), F32)],
        compiler_params=pltpu.CompilerParams(
            dimension_semantics=("parallel", "arbitrary"), vmem_limit_bytes=VMEM_LIMIT),
        name="mla",
    )(qm, qm, kc, key_tail, wuv_heads)


def _gelu_tanh(x):
    return 0.5 * x * (1.0 + jnp.tanh(math.sqrt(2.0 / math.pi) * (x + 0.044715 * (x * x * x))))


def _mix_out_kernel(h_ref, head_ref, ys_ref, yw_ref, ym_ref, wglu_ref, bglu_ref, gh_ref, wout_ref, gpm_ref, gpre_ref,
                    wup_ref, wdn_ref, gpost_ref, o_ref, *, ff_chunk, from_x, nsub):
    tm = o_ref.shape[1]
    ts = tm // nsub
    subs = [slice(sub * ts, (sub + 1) * ts) for sub in range(nsub)]
    gh = gh_ref[...]
    zs = [_gelu_tanh(jnp.concatenate([ys_ref[0, 0, rows, :], ys_ref[0, 1, rows, :]], axis=1)) for rows in subs]
    gates = [jnp.dot(z.astype(BF16), wglu_ref[0], preferred_element_type=F32) + bglu_ref[...] for z in zs]
    mixeds = []
    for rows, z, gate in zip(subs, zs, gates):
        y_ssm = z * (1.0 / (1.0 + jnp.exp(-gate)))
        mixeds.append(jnp.concatenate([
            _rms(y_ssm, gh[:, 0:SSM_WIDTH]),
            _rms(yw_ref[0, rows, :], gh[:, SSM_WIDTH:SSM_WIDTH + SWA_WIDTH]),
            _rms(ym_ref[0, rows, :], gh[:, SSM_WIDTH + SWA_WIDTH:]),
        ], axis=1).astype(BF16))
    mos = [jnp.dot(mixed, wout_ref[0], preferred_element_type=F32) for mixed in mixeds]
    h1s = []
    for sub, (rows, mo) in enumerate(zip(subs, mos)):
        h_rows = _seq_rows(h_ref, head_ref, pl.program_id(1), sub * ts, ts) if from_x else h_ref[0, rows, :]
        h1s.append(h_rows + _rms(mo, gpm_ref[...]))
    acts = [_rms(h1, gpre_ref[...]).astype(BF16) for h1 in h1s]
    fs = [jnp.zeros_like(h1) for h1 in h1s]
    for c in range(D_FF // ff_chunk):
        ups = [jnp.dot(a, wup_ref[0, :, c * ff_chunk:(c + 1) * ff_chunk], preferred_element_type=F32) for a in acts]
        hids = [jnp.square(jnp.maximum(up, 0.0)).astype(BF16) for up in ups]
        fs = [f + jnp.dot(hid, wdn_ref[0, c * ff_chunk:(c + 1) * ff_chunk, :], preferred_element_type=F32)
              for f, hid in zip(fs, hids)]
    for rows, h1, f in zip(subs, h1s, fs):
        o_ref[0, rows, :] = h1 + _rms(f, gpost_ref[...])


def _mix_out(h, head, ys, yw, ym, wglu, bglu, gh, wout, gpm, gpre, wup, wdn, gpost, layer, *, from_x=False, skip=0,
             tm=1056, ff_chunk=1024, nsub=2):
    assert not (from_x and skip)
    bsz, seq = yw.shape[0], yw.shape[1]
    out_rows = seq - skip
    if skip:
        el = pl.Element
        first = lambda i: pl.multiple_of(skip + i * tm, BLOCK)
        row = lambda w: pl.BlockSpec((el(1), el(tm), el(w)), lambda b, i: (b, first(i), 0))
        slabs = pl.BlockSpec((el(1), el(2), el(tm), el(128)), lambda b, i: (b, 0, first(i), 0))
    else:
        row = lambda w: pl.BlockSpec((1, tm, w), lambda b, i: (b, i, 0))
        slabs = pl.BlockSpec((1, 2, tm, 128), lambda b, i: (b, 0, i, 0))
    return pl.pallas_call(
        functools.partial(_mix_out_kernel, ff_chunk=ff_chunk, from_x=from_x, nsub=nsub),
        grid=(bsz, out_rows // tm),
        in_specs=[
            _seq_window_spec(tm, D_MODEL) if from_x else row(D_MODEL), _const_spec((BLOCK, D_MODEL)),
            slabs, row(SWA_WIDTH), row(MLA_WIDTH),
            _layer_spec((SSM_WIDTH, SSM_WIDTH), layer), _const_spec((1, SSM_WIDTH)), _const_spec((1, D_MODEL)),
            _layer_spec((D_MODEL, D_MODEL), layer), _const_spec((1, D_MODEL)), _const_spec((1, D_MODEL)),
            _layer_spec((D_MODEL, D_FF), layer), _layer_spec((D_FF, D_MODEL), layer), _const_spec((1, D_MODEL)),
        ],
        out_specs=pl.BlockSpec((1, tm, D_MODEL), lambda b, i: (b, i, 0)),
        out_shape=jax.ShapeDtypeStruct((bsz, out_rows, D_MODEL), F32),
        compiler_params=pltpu.CompilerParams(
            dimension_semantics=("parallel", "parallel"), vmem_limit_bytes=VMEM_LIMIT),
        name="mix_out",
    )(h, head, ys, yw, ym, wglu, bglu, gh, wout, gpm, gpre, wup, wdn, gpost)


def _rot_half_cols(w):
    half = w.shape[-1] // 2
    return jnp.concatenate([-w[..., half:], w[..., :half]], axis=-1)


def _stacked_weights(w_in, w_uq, w_uk, w_uv):
    depth = w_in.shape[0]
    w_in, w_uq, w_uk, w_uv = (w.astype(BF16) for w in (w_in, w_uq, w_uk, w_uv))
    z = lambda n: jnp.zeros((depth, D_MODEL, n), BF16)
    s_cq = SSM_WIDTH + SWA_WIDTH + 2 * SWA_KV_WIDTH
    s_ckv = s_cq + MLA_Q_RANK
    s_kr = s_ckv + MLA_KV_RANK
    kr = w_in[:, :, s_kr:s_kr + MLA_ROPE]
    w1 = jnp.concatenate([
        w_in[:, :, :s_ckv], z(C_CKV - C_CQ - MLA_Q_RANK),
        w_in[:, :, s_ckv:s_kr], kr, _rot_half_cols(kr), z(C_END - C_KR - 2 * MLA_ROPE)], axis=2)
    uq = w_uq.reshape(depth, MLA_Q_RANK, MLA_HEADS, MLA_NOPE + MLA_ROPE)
    rope = uq[..., MLA_NOPE:]
    wuq = jnp.concatenate([
        uq[..., :MLA_NOPE].reshape(depth, MLA_Q_RANK, -1), rope.reshape(depth, MLA_Q_RANK, -1),
        _rot_half_cols(rope).reshape(depth, MLA_Q_RANK, -1)], axis=2)
    wuq = jnp.pad(wuq, ((0, 0), (0, 256 - MLA_Q_RANK), (0, 0)))
    uk = w_uk.reshape(depth, MLA_KV_RANK, MLA_HEADS, MLA_NOPE).transpose(0, 2, 3, 1)
    wuk_bd = _group_block_diag(uk)
    uv = w_uv.reshape(depth, MLA_KV_RANK, MLA_HEADS, MLA_V).transpose(0, 2, 1, 3)
    wuv_heads = _group_block_diag(uv).reshape(depth, MLA_HEADS, MLA_KV_RANK, MLA_WIDTH)
    return w1, wuq, wuk_bd, wuv_heads


def _constant_tables(seq):
    half = MLA_ROPE // 2
    inv_freq = ROPE_THETA ** (-np.arange(half, dtype=np.float64) / half)
    ang = (np.arange(seq) - PAD)[:, None] * inv_freq[None, :]
    cos_t = np.tile(np.cos(ang), (1, 2 * MLA_HEADS)).astype(np.float32)
    sin_t = np.tile(np.sin(ang), (1, 2 * MLA_HEADS)).astype(np.float32)
    seq_k = -(-seq // MLA_KEY_ROWS) * MLA_KEY_ROWS
    kblk = np.arange(seq_k)[:, None] // BLOCK
    bcol = np.arange(MLA_BLOCKS)[None, :]
    one = np.ones((seq_k, 1))
    fill = np.zeros((seq_k, 128 - 2 - MLA_BLOCKS))
    kx_t = np.concatenate([one, np.where(np.arange(seq_k)[:, None] < PAD, NEG, 0.0), kblk == bcol, fill], axis=1)
    qx_t = np.concatenate([0.0 * one, one, np.where(bcol >= kblk, NEG, 0.0), fill], axis=1)[:seq]
    key_tail = np.zeros((seq_k - seq, MLA_QK))
    key_tail[:, MLA_KV_RANK:MLA_KV_RANK + MLA_EXT] = kx_t[seq:, :MLA_EXT]
    slopes = 2.0 ** (-8.0 * np.arange(1, SWA_HEADS + 1) / SWA_HEADS)
    slope_rows = np.repeat(slopes.reshape(SWA_KV_HEADS, SWA_GROUP), BLOCK, axis=1)[:, None, :]
    dist = (np.arange(BLOCK)[None, :] + BLOCK) - np.arange(2 * BLOCK)[:, None]
    in_window = np.tile((dist >= 0) & (dist < BLOCK), (1, SWA_GROUP))[None, None]
    alibi = (-LOG2E * slope_rows * np.tile(dist, (1, SWA_GROUP))[None])[None]
    key_pos = (np.arange(3)[:, None] - 1) * BLOCK + np.arange(2 * BLOCK)[None, :]
    key_ok = ((key_pos >= PAD) | (np.arange(3)[:, None] >= 2))[:, None, :, None]
    swa_bias = np.where(in_window & key_ok, alibi, NEG).astype(np.float32)
    bf16 = lambda t: jnp.asarray(t, dtype=BF16)
    return (jnp.asarray(cos_t), jnp.asarray(sin_t), bf16(qx_t), bf16(kx_t[:seq]), bf16(key_tail),
            jnp.asarray(swa_bias))


def kernel(x, meta_tokens, norm_pre_mix, norm_post_mix, norm_pre_mlp, norm_post_mlp, w_in, w_out, norm_heads,
           ssm_a_re, ssm_a_im, ssm_log_step, ssm_b_re, ssm_b_im, ssm_c_re, ssm_c_im, ssm_d, ssm_w_glu,
           ssm_b_glu, swa_sinks, mla_q_norm, mla_kv_norm, mla_w_uq, mla_w_uk, mla_w_uv, w_mlp_up, w_mlp_down):
    bsz, seq_real, d = x.shape
    depth = w_in.shape[0]
    head = jnp.concatenate([jnp.zeros((PAD, d), x.dtype), meta_tokens.astype(x.dtype)], axis=0)
    seq = BLOCK + seq_real
    h = x

    cos_t, sin_t, qx_t, kx_t, key_tail, swa_bias = _constant_tables(seq)
    per_row = lambda v: jnp.repeat(v.reshape(SWA_KV_HEADS, SWA_GROUP), BLOCK, axis=1)[:, None, :]

    r_mat, p_mat, q_mat, lam = _s5_operators(
        ssm_a_re, ssm_a_im, ssm_log_step, ssm_b_re, ssm_b_im, ssm_c_re, ssm_c_im)
    row = lambda v: v.reshape(1, -1).astype(F32)
    wglu, wout, wup, wdn = (w.astype(BF16) for w in (ssm_w_glu, w_out, w_mlp_up, w_mlp_down))
    w1, wuq, wuk_bd, wuv_heads = _stacked_weights(w_in, mla_w_uq, mla_w_uk, mla_w_uv)

    for l in range(depth):
        gq = jnp.concatenate([mla_q_norm[l], jnp.zeros((256 - MLA_Q_RANK,), F32)]).reshape(1, 256)
        u, q, kv, qm, kc = _mix_in(h, head, row(norm_pre_mix[l]), w1, gq, row(mla_kv_norm[l]), wuq, wuk_bd,
                                   cos_t, sin_t, qx_t, kx_t, l, from_x=l == 0)
        d_fold = jnp.tile(ssm_d[l].astype(F32), SSM_CHUNK).reshape(1, SSM_FOLD)
        y_ssm = _s5_sequence(u, r_mat, p_mat, q_mat, lam, d_fold, l)
        y_swa = _swa(q, kv, swa_bias, per_row(LOG2E * swa_sinks[l].astype(F32)))
        y_mla = _mla(qm, kc, key_tail, wuv_heads, l)
        last = l == depth - 1
        h = _mix_out(
            h, head, y_ssm, y_swa, y_mla,
            wglu, row(ssm_b_glu[l]), row(norm_heads[l]), wout, row(norm_post_mix[l]), row(norm_pre_mlp[l]),
            wup, wdn, row(norm_post_mlp[l]), l, from_x=l == 0, skip=PAD + N_META if last else 0,
            tm=1024 if last else 1056)
    return h
```

```python
import functools
import math

import jax
import jax.numpy as jnp
import numpy as np
from jax import lax
from jax.experimental import pallas as pl
from jax.experimental.pallas import tpu as pltpu

F32 = jnp.float32
BF16 = jnp.bfloat16

D_MODEL = 1024
N_META = 16
BLOCK = 128
PAD = BLOCK - N_META
SSM_GROUPS = 16
SSM_GROUP_CH = 16
SSM_WIDTH = SSM_GROUPS * SSM_GROUP_CH
SSM_STATE = 64
SWA_HEADS = 8
SWA_KV_HEADS = 2
SWA_GROUP = SWA_HEADS // SWA_KV_HEADS
HEAD_DIM = 64
SWA_WIDTH = SWA_HEADS * HEAD_DIM
SWA_KV_WIDTH = SWA_KV_HEADS * HEAD_DIM
SWA_KV_COLS = SWA_KV_WIDTH + SWA_KV_HEADS * BLOCK
MLA_HEADS = 4
MLA_Q_RANK = 192
MLA_KV_RANK = 128
MLA_NOPE = 64
MLA_ROPE = 32
MLA_V = 64
MLA_WIDTH = MLA_HEADS * MLA_V
MLA_BLOCKS = 40
MLA_EXT = 48
MLA_ACC = MLA_KV_RANK + 8
LOG2E = 1.4426950408889634
MLA_QK = MLA_KV_RANK + MLA_EXT + MLA_ROPE
ROPE_THETA = 10000.0
D_FF = 4 * D_MODEL
EPS = 1e-6
NEG = -1e30

SSM_CHUNK = 8
SSM_FOLD = SSM_CHUNK * SSM_WIDTH
SSM_MODES = SSM_GROUPS * SSM_STATE

C_U, C_Q, C_K, C_CQ, C_CKV, C_KR, C_END = 0, 256, 768, 1024, 1280, 1408, 1536

MLA_KEY_ROWS = 512
MLA_PART = 256

VMEM_LIMIT = 58 * 1024 * 1024


def _rms(x, g, n=None):
    n = x.shape[-1] if n is None else n
    ms = jnp.sum(x * x, axis=-1, keepdims=True) * (1.0 / n)
    return x * lax.rsqrt(ms + EPS) * g


def _const_spec(shape):
    nd = len(shape)
    return pl.BlockSpec(shape, lambda *_: (0,) * nd, pipeline_mode=pl.Buffered(1))


def _seq_window_spec(tm, width):
    el = pl.Element
    return pl.BlockSpec((el(1), el(tm), el(width)),
                        lambda b, i: (b, pl.multiple_of(jnp.maximum(i * tm - BLOCK, 0), 8), 0))


def _seq_rows(x_ref, head_ref, i, lo, n):
    first = jnp.where(i == 0, 1, 0)
    if lo == 0:
        top = jnp.where(i == 0, head_ref[...], x_ref[0, 0:BLOCK, :])
        rest = x_ref[0, pl.ds(pl.multiple_of(BLOCK - BLOCK * first, 8), n - BLOCK), :]
        return jnp.concatenate([top, rest], axis=0)
    return x_ref[0, pl.ds(pl.multiple_of(lo - BLOCK * first, 8), n), :]


def _layer_spec(shape, layer):
    nd = len(shape)
    return pl.BlockSpec((1,) + shape, lambda *_: (layer,) + (0,) * nd, pipeline_mode=pl.Buffered(1))


def _mix_in_kernel(h_ref, head_ref, g_ref, w1_ref, gq_ref, gkv_ref, wuq_ref, wuk_ref, cos_ref, sin_ref, qx_ref,
                   kx_ref, u_ref, q_ref, kv_ref, qm_ref, kc_ref, *, tm, nsub, from_x):
    j = pl.program_id(1)
    ts = tm // nsub
    scale = LOG2E * (MLA_NOPE + MLA_ROPE) ** -0.5
    ext = slice(MLA_KV_RANK, MLA_KV_RANK + MLA_EXT)
    rot = slice(MLA_KV_RANK + MLA_EXT, MLA_QK)
    for sub in range(nsub):
        rows = slice(sub * ts, (sub + 1) * ts)
        h_rows = _seq_rows(h_ref, head_ref, j, sub * ts, ts) if from_x else h_ref[0, rows, :]
        a = _rms(h_rows, g_ref[...]).astype(BF16)
        proj = jnp.dot(a, w1_ref[0], preferred_element_type=F32)
        row = j * tm + sub * ts + lax.broadcasted_iota(jnp.int32, (ts, 1), 0)
        for c in range(2):
            u_ref[0, c, rows, :] = jnp.where(row >= PAD, proj[:, C_U + c * 128:C_U + (c + 1) * 128], 0.0)
        q_ref[0, rows, :] = (proj[:, C_Q:C_K] * (LOG2E * HEAD_DIM ** -0.5)).astype(BF16)
        kv_ref[0, rows, 0:SWA_KV_WIDTH] = proj[:, C_K:C_K + SWA_KV_WIDTH].astype(BF16)
        ones_col = jnp.where(lax.broadcasted_iota(jnp.int32, (ts, HEAD_DIM), 1) == 0, 1.0, 0.0).astype(BF16)
        for hk in range(SWA_KV_HEADS):
            lo = SWA_KV_WIDTH + hk * BLOCK
            v_lo = C_K + SWA_KV_WIDTH + hk * HEAD_DIM
            kv_ref[0, rows, lo:lo + HEAD_DIM] = proj[:, v_lo:v_lo + HEAD_DIM].astype(BF16)
            kv_ref[0, rows, lo + HEAD_DIM:lo + BLOCK] = ones_col

        cos = cos_ref[rows, :]
        sin = sin_ref[rows, :]
        qn = _rms(proj[:, C_CQ:C_CKV], gq_ref[...], n=MLA_Q_RANK).astype(BF16)
        q2 = jnp.dot(qn, wuq_ref[0], preferred_element_type=F32)
        qr = (q2[:, 256:384] * cos + q2[:, 384:512] * sin) * scale
        qa = jnp.dot(q2[:, 0:256].astype(BF16), wuk_ref[0], preferred_element_type=F32) * scale
        c = _rms(proj[:, C_CKV:C_KR], gkv_ref[...])
        krr = proj[:, C_KR:C_END]
        kr = krr[:, 0:MLA_ROPE] * cos[:, 0:MLA_ROPE] + krr[:, MLA_ROPE:2 * MLA_ROPE] * sin[:, 0:MLA_ROPE]
        for hd in range(MLA_HEADS):
            qm_ref[0, hd, rows, 0:MLA_KV_RANK] = qa[:, hd * MLA_KV_RANK:(hd + 1) * MLA_KV_RANK].astype(BF16)
            qm_ref[0, hd, rows, ext] = qx_ref[rows, 0:MLA_EXT]
            qm_ref[0, hd, rows, rot] = qr[:, hd * MLA_ROPE:(hd + 1) * MLA_ROPE].astype(BF16)
        kc_ref[0, rows, 0:MLA_KV_RANK] = c.astype(BF16)
        kc_ref[0, rows, ext] = kx_ref[rows, 0:MLA_EXT]
        kc_ref[0, rows, rot] = kr.astype(BF16)


def _mix_in(h, head, g, w1, gq, gkv, wuq, wuk, cos_t, sin_t, qx_t, kx_t, layer, *, from_x, tm=2112, nsub=4):
    bsz = h.shape[0]
    seq = cos_t.shape[0]
    grid = (bsz, seq // tm)
    row3 = lambda b, j: (b, j, 0)
    return pl.pallas_call(
        functools.partial(_mix_in_kernel, tm=tm, nsub=nsub, from_x=from_x),
        grid=grid,
        in_specs=[
            _seq_window_spec(tm, D_MODEL) if from_x else pl.BlockSpec((1, tm, D_MODEL), row3),
            _const_spec((BLOCK, D_MODEL)),
            _const_spec((1, D_MODEL)),
            _layer_spec((D_MODEL, C_END), layer),
            _const_spec((1, 256)),
            _const_spec((1, MLA_KV_RANK)),
            _layer_spec((256, 512), layer),
            _layer_spec((256, 512), layer),
            pl.BlockSpec((tm, 128), lambda b, j: (j, 0)),
            pl.BlockSpec((tm, 128), lambda b, j: (j, 0)),
            pl.BlockSpec((tm, 128), lambda b, j: (j, 0)),
            pl.BlockSpec((tm, 128), lambda b, j: (j, 0)),
        ],
        out_specs=[
            pl.BlockSpec((1, 2, tm, 128), lambda b, j: (b, 0, j, 0)),
            pl.BlockSpec((1, tm, SWA_WIDTH), row3),
            pl.BlockSpec((1, tm, SWA_KV_COLS), row3),
            pl.BlockSpec((1, MLA_HEADS, tm, MLA_QK), lambda b, j: (b, 0, j, 0)),
            pl.BlockSpec((1, tm, MLA_QK), row3),
        ],
        out_shape=[
            jax.ShapeDtypeStruct((bsz, 2, seq, 128), F32),
            jax.ShapeDtypeStruct((bsz, seq, SWA_WIDTH), BF16),
            jax.ShapeDtypeStruct((bsz, seq, SWA_KV_COLS), BF16),
            jax.ShapeDtypeStruct((bsz, MLA_HEADS, seq, MLA_QK), BF16),
            jax.ShapeDtypeStruct((bsz, seq, MLA_QK), BF16),
        ],
        compiler_params=pltpu.CompilerParams(
            dimension_semantics=("parallel", "parallel"), vmem_limit_bytes=VMEM_LIMIT),
        name="mix_in",
    )(h, head, g, w1, gq, gkv, wuq, wuk, cos_t, sin_t, qx_t, kx_t)


def _split_bf16(x):
    hi = x.astype(BF16)
    lo = (x - hi.astype(F32)).astype(BF16)
    return hi, lo


def _bdot3(a, b):
    ah, al = _split_bf16(a)
    bh, bl = _split_bf16(b)
    dn = (((2,), (2,)), ((0,), (0,)))
    f = lambda x, y: lax.dot_general(x, y, dn, preferred_element_type=F32)
    return f(ah, bh) + f(ah, bl) + f(al, bh)


def _expand_groups(a2, row_shift, col_shift, n_cols):
    rows, w = a2.shape
    tile = (lax.broadcasted_iota(jnp.int32, (w, n_cols), 1) & (w - 1)) == lax.broadcasted_iota(
        jnp.int32, (w, n_cols), 0)
    wide = jnp.dot(a2.astype(BF16), jnp.where(tile, 1.0, 0.0).astype(BF16), preferred_element_type=F32)
    same = (lax.broadcasted_iota(jnp.int32, (rows, n_cols), 0) >> row_shift) == (
        lax.broadcasted_iota(jnp.int32, (rows, n_cols), 1) >> col_shift)
    return jnp.where(same, wide, 0.0)


def _s5_operator_kernel(ar_ref, ai_ref, ls_ref, btr_ref, bti_ref, cr_ref, ci_ref,
                        r_ref, p_ref, q_ref, l8r_ref, l8i_ref, pwr_sc, pwi_sc, kk_sc):
    s = pl.program_id(1)
    ar, ai = ar_ref[0], ai_ref[0]
    step = jnp.exp(ls_ref[0])
    lr = jnp.exp(ar * step) * jnp.cos(ai * step)
    li = jnp.exp(ar * step) * jnp.sin(ai * step)

    den = ar * ar + ai * ai
    nr, ni = lr - 1.0, li
    coef_re = (nr * ar + ni * ai) / den
    coef_im = (ni * ar - nr * ai) / den
    btr, bti = btr_ref[0], bti_ref[0]
    bb_re = coef_re * btr - coef_im * bti
    bb_im = coef_re * bti + coef_im * btr
    cr, ci = cr_ref[0], ci_ref[0]

    def c_lam(k):
        qr_, qi_ = pwr_sc[k], pwi_sc[k]
        return cr * qr_ - ci * qi_, cr * qi_ + ci * qr_

    gc = SSM_GROUPS * SSM_GROUP_CH

    @pl.when(s == 0)
    def _():
        pr, pi = jnp.ones_like(lr), jnp.zeros_like(li)
        for k in range(SSM_CHUNK + 1):
            pwr_sc[k] = pr
            pwi_sc[k] = pi
            pr, pi = pr * lr - pi * li, pr * li + pi * lr
        for k in range(SSM_CHUNK):
            clr, cli = c_lam(k)
            kk_sc[k] = (_bdot3(bb_re, clr) - _bdot3(bb_im, cli)).reshape(gc, SSM_GROUP_CH)

    for t in range(SSM_CHUNK):
        lag = t - s
        blk = _expand_groups(kk_sc[jnp.maximum(lag, 0)], 4, 4, gc)
        r_ref[0, :, t * gc:(t + 1) * gc] = jnp.where(lag >= 0, blk, 0.0).astype(BF16)

    pw_r, pw_i = pwr_sc[SSM_CHUNK - 1 - s], pwi_sc[SSM_CHUNK - 1 - s]
    p_re = (pw_r * bb_re - pw_i * bb_im).reshape(gc, SSM_STATE)
    p_im = (pw_r * bb_im + pw_i * bb_re).reshape(gc, SSM_STATE)
    p_ref[0, :, 0:SSM_MODES] = _expand_groups(p_re, 4, 6, SSM_MODES).astype(BF16)
    p_ref[0, :, SSM_MODES:2 * SSM_MODES] = _expand_groups(p_im, 4, 6, SSM_MODES).astype(BF16)

    clr, cli = c_lam(s + 1)
    q_ref[0, 0:SSM_MODES, :] = _expand_groups(clr.reshape(gc, SSM_STATE), 4, 6, SSM_MODES).T.astype(BF16)
    q_ref[0, SSM_MODES:2 * SSM_MODES, :] = (
        -_expand_groups(cli.reshape(gc, SSM_STATE), 4, 6, SSM_MODES)).T.astype(BF16)

    l8r_ref[0] = pwr_sc[SSM_CHUNK]
    l8i_ref[0] = pwi_sc[SSM_CHUNK]


def _s5_operators(a_re, a_im, log_step, b_re, b_im, c_re, c_im):
    depth = a_re.shape[0]
    G, P, C, T = SSM_GROUPS, SSM_STATE, SSM_GROUP_CH, SSM_CHUNK
    lay = lambda *shape: pl.BlockSpec((1,) + shape, lambda l, s: (l,) + (0,) * len(shape))
    ls = jnp.broadcast_to(log_step[:, :, None, None], (depth, G, 1, P))
    args = (a_re[:, :, None, :], a_im[:, :, None, :], ls,
            b_re.transpose(0, 1, 3, 2), b_im.transpose(0, 1, 3, 2), c_re, c_im)
    r_mat, p_mat, q_mat, l8r, l8i = pl.pallas_call(
        _s5_operator_kernel,
        grid=(depth, T),
        in_specs=[lay(G, 1, P)] * 3 + [lay(G, C, P)] * 4,
        out_specs=[
            pl.BlockSpec((1, SSM_WIDTH, SSM_FOLD), lambda l, s: (l, s, 0)),
            pl.BlockSpec((1, SSM_WIDTH, 2 * SSM_MODES), lambda l, s: (l, s, 0)),
            pl.BlockSpec((1, 2 * SSM_MODES, SSM_WIDTH), lambda l, s: (l, 0, s)),
            lay(G, 1, P), lay(G, 1, P)],
        out_shape=[
            jax.ShapeDtypeStruct((depth, SSM_FOLD, SSM_FOLD), BF16),
            jax.ShapeDtypeStruct((depth, SSM_FOLD, 2 * SSM_MODES), BF16),
            jax.ShapeDtypeStruct((depth, 2 * SSM_MODES, SSM_FOLD), BF16),
            jax.ShapeDtypeStruct((depth, G, 1, P), F32),
            jax.ShapeDtypeStruct((depth, G, 1, P), F32),
        ],
        scratch_shapes=[pltpu.VMEM((T + 1, G, 1, P), F32), pltpu.VMEM((T + 1, G, 1, P), F32),
                        pltpu.VMEM((T, G * C, C), F32)],
        compiler_params=pltpu.CompilerParams(dimension_semantics=("parallel", "arbitrary")),
        name="s5_operators",
    )(*args)
    lam = jnp.concatenate([l8r.reshape(depth, 1, SSM_MODES), l8i.reshape(depth, 1, SSM_MODES)], axis=2)
    return r_mat, p_mat, q_mat, lam


def _group_block_diag(x):
    *lead, G, r, c = x.shape
    eye = jnp.eye(G, dtype=x.dtype)
    y = x[..., :, :, None, :] * eye[:, None, :, None]
    return y.reshape(*lead, G * r, G * c)


def _fold_tokens(u_ref, n_chunks):
    return jnp.concatenate(
        [u_ref[0, c, pl.ds(s, n_chunks, stride=SSM_CHUNK), :] for s in range(SSM_CHUNK) for c in range(2)], axis=1)


def _s5_kernel(u_ref, p_ref, lam_ref, r_ref, q_ref, d_ref, y_ref, s_sc, *, n_chunks):
    u = _fold_tokens(u_ref, n_chunks)
    ub = u.astype(BF16)
    s_sc[...] = jnp.dot(ub, p_ref[0], preferred_element_type=F32)
    lr = lam_ref[0, :, 0:SSM_MODES]
    li = lam_ref[0, :, SSM_MODES:2 * SSM_MODES]

    def body(n, carry):
        xr, xi = carry
        sr = s_sc[pl.ds(n, 1), 0:SSM_MODES]
        si = s_sc[pl.ds(n, 1), SSM_MODES:2 * SSM_MODES]
        s_sc[pl.ds(n, 1), 0:SSM_MODES] = xr
        s_sc[pl.ds(n, 1), SSM_MODES:2 * SSM_MODES] = xi
        return lr * xr - li * xi + sr, lr * xi + li * xr + si

    z = jnp.zeros((1, SSM_MODES), F32)
    lax.fori_loop(0, n_chunks, body, (z, z))
    x = s_sc[...].astype(BF16)
    for t in range(SSM_CHUNK):
        cols = slice(t * SSM_WIDTH, (t + 1) * SSM_WIDTH)
        rows = (t + 1) * SSM_WIDTH
        y = jnp.dot(ub[:, 0:rows], r_ref[0, 0:rows, cols], preferred_element_type=F32)
        y += jnp.dot(x, q_ref[0, :, cols], preferred_element_type=F32)
        y += d_ref[:, cols] * u[:, cols]
        for c in range(2):
            y_ref[0, c, pl.ds(t, n_chunks, stride=SSM_CHUNK), :] = y[:, c * 128:(c + 1) * 128]


def _s5_sequence(u, r_mat, p_mat, q_mat, lam, d_fold, layer):
    bsz, _, seq, _ = u.shape
    n_chunks = seq // SSM_CHUNK
    slabs = pl.BlockSpec((1, 2, seq, 128), lambda b: (b, 0, 0, 0))
    return pl.pallas_call(
        functools.partial(_s5_kernel, n_chunks=n_chunks), grid=(bsz,),
        in_specs=[slabs, _layer_spec((SSM_FOLD, 2 * SSM_MODES), layer), _layer_spec((1, 2 * SSM_MODES), layer),
                  _layer_spec((SSM_FOLD, SSM_FOLD), layer), _layer_spec((2 * SSM_MODES, SSM_FOLD), layer),
                  _const_spec((1, SSM_FOLD))],
        out_specs=slabs,
        out_shape=jax.ShapeDtypeStruct((bsz, 2, seq, 128), F32),
        scratch_shapes=[pltpu.VMEM((n_chunks, 2 * SSM_MODES), F32)],
        compiler_params=pltpu.CompilerParams(dimension_semantics=("parallel",), vmem_limit_bytes=VMEM_LIMIT),
        name="s5_sequence",
    )(u, p_mat, lam, r_mat, q_mat, d_fold)


def _swa_kernel(q_ref, kvp_ref, kvc_ref, bias_ref, sink_ref, o_ref, *, nsub):
    i = pl.program_id(1)
    nt_dims = (((1,), (1,)), ((), ()))
    tn_dims = (((0,), (0,)), ((), ()))
    chains = [(r, hk) for r in range(nsub) for hk in range(SWA_KV_HEADS)]
    scores, vcats = [], []
    for r, hk in chains:
        rows = slice(r * BLOCK, (r + 1) * BLOCK)
        q = q_ref[0, rows, :]
        kvc = kvc_ref[0, rows, :]
        kvp = kvp_ref[0] if r == 0 else kvc_ref[0, (r - 1) * BLOCK:r * BLOCK, :]
        ks = slice(hk * HEAD_DIM, (hk + 1) * HEAD_DIM)
        vs = slice(SWA_KV_WIDTH + hk * BLOCK, SWA_KV_WIDTH + (hk + 1) * BLOCK)
        kcat = jnp.concatenate([kvp[:, ks], kvc[:, ks]], axis=0)
        vcats.append(jnp.concatenate([kvp[:, vs], kvc[:, vs]], axis=0))
        qs = jnp.concatenate(
            [q[:, (hk * SWA_GROUP + g) * HEAD_DIM:(hk * SWA_GROUP + g + 1) * HEAD_DIM]
             for g in range(SWA_GROUP)], axis=0)
        table = jnp.minimum(i * nsub + r, 2)
        scores.append(lax.dot_general(kcat, qs, nt_dims, preferred_element_type=F32) + bias_ref[table, hk])
    weights, sink_terms = [], []
    for (r, hk), s in zip(chains, scores):
        sink = sink_ref[hk]
        m = jnp.maximum(jnp.max(s, axis=0, keepdims=True), sink)
        weights.append(jnp.exp2(s - m).astype(BF16))
        sink_terms.append(jnp.exp2(sink - m))
    o_ts = []
    for v, p, sink_term in zip(vcats, weights, sink_terms):
        o_t = lax.dot_general(v, p, tn_dims, preferred_element_type=F32)
        o_ts.append(o_t[0:HEAD_DIM, :] / (o_t[HEAD_DIM:HEAD_DIM + 1, :] + sink_term))
    for r in range(nsub):
        outs = []
        for hk in range(SWA_KV_HEADS):
            o_t = o_ts[r * SWA_KV_HEADS + hk]
            outs += [o_t[:, g * BLOCK:(g + 1) * BLOCK] for g in range(SWA_GROUP)]
        o_ref[0, r * BLOCK:(r + 1) * BLOCK, :] = jnp.concatenate(outs, axis=0).T


def _swa(q, kv, bias, sink_rows, *, nsub=11):
    bsz, seq, _ = q.shape
    tm = nsub * BLOCK
    cols = SWA_GROUP * BLOCK
    return pl.pallas_call(
        functools.partial(_swa_kernel, nsub=nsub),
        grid=(bsz, seq // tm),
        in_specs=[
            pl.BlockSpec((1, tm, SWA_WIDTH), lambda b, i: (b, i, 0)),
            pl.BlockSpec((1, BLOCK, SWA_KV_COLS), lambda b, i: (b, jnp.maximum(nsub * i - 1, 0), 0)),
            pl.BlockSpec((1, tm, SWA_KV_COLS), lambda b, i: (b, i, 0)),
            _const_spec((3, SWA_KV_HEADS, 2 * BLOCK, cols)),
            _const_spec((SWA_KV_HEADS, 1, cols)),
        ],
        out_specs=pl.BlockSpec((1, tm, SWA_WIDTH), lambda b, i: (b, i, 0)),
        out_shape=jax.ShapeDtypeStruct((bsz, seq, SWA_WIDTH), F32),
        compiler_params=pltpu.CompilerParams(
            dimension_semantics=("parallel", "parallel"), vmem_limit_bytes=VMEM_LIMIT),
        name="swa",
    )(q, kv, kv, bias, sink_rows)


def _mla_kernel(q_ref, qn_ref, kraw_ref, ktail_ref, wuv_ref, o_ref, k_sc, sa_sc, sb_sc, acc_sc, mx_sc, *, tq, rows):
    i = pl.program_id(1)
    cols = MLA_HEADS * tq
    nsub = tq // BLOCK
    seq = kraw_ref.shape[1]
    q = q_ref[0].reshape(cols, MLA_QK)
    n_groups = ((i + 1) * tq + rows - 1) // rows
    nt_dims = (((1,), (1,)), ((), ()))
    tn_dims = (((0,), (0,)), ((), ()))

    def fold(x, op):
        return op(x.reshape(x.shape[0] // 8, 8, x.shape[1]), axis=0)

    n_parts = rows // MLA_PART

    def scores_part(g, buf, c, qq=q):
        start = pl.multiple_of(g * rows + c * MLA_PART, MLA_PART)
        s = lax.dot_general(k_sc[pl.ds(start, MLA_PART), :], qq, nt_dims, preferred_element_type=F32)
        buf[c * MLA_PART:(c + 1) * MLA_PART, :] = s
        return fold(s, jnp.max)

    def rescale(mx8, m_run):
        m_new = jnp.maximum(m_run, jnp.max(mx8, axis=0, keepdims=True))
        acc_sc[...] = acc_sc[...] * jnp.exp2(m_run - m_new)
        return m_new

    def weights_part(g, buf, c, m_new):
        start = pl.multiple_of(g * rows + c * MLA_PART, MLA_PART)
        p = jnp.exp2(buf[c * MLA_PART:(c + 1) * MLA_PART, :] - m_new).astype(BF16)
        v = k_sc[pl.ds(start, MLA_PART), 0:MLA_ACC]
        acc_sc[...] += lax.dot_general(v, p, tn_dims, preferred_element_type=F32)

    def scores(g, buf):
        return functools.reduce(jnp.maximum, [scores_part(g, buf, c) for c in range(n_parts)])

    def weights(g, buf, mx8, m_run):
        m_new = rescale(mx8, m_run)
        for c in range(n_parts):
            weights_part(g, buf, c, m_new)
        return m_new

    def overlapped(g_next, buf_next, g, buf, mx8, m_run):
        m_new = rescale(mx8, m_run)
        parts = []
        for c in range(n_parts):
            parts.append(scores_part(g_next, buf_next, c))
            weights_part(g, buf, c, m_new)
        return functools.reduce(jnp.maximum, parts), m_new

    @pl.when(i == 0)
    def _():
        k_sc[0:seq, :] = kraw_ref[0]
        k_sc[seq:, :] = ktail_ref[...]
        mx_sc[...] = scores(0, sa_sc)

    acc_sc[...] = jnp.zeros((MLA_ACC, cols), F32)
    m0 = jnp.full((1, cols), NEG, F32)

    def pair(k, carry):
        mxa, m_run = carry
        mxb, m_run = overlapped(2 * k + 1, sb_sc, 2 * k, sa_sc, mxa, m_run)
        mxa, m_run = overlapped(2 * k + 2, sa_sc, 2 * k + 1, sb_sc, mxb, m_run)
        return mxa, m_run

    n_pairs = (n_groups - 1) // 2
    mxa, m_run = lax.fori_loop(0, n_pairs, pair, (mx_sc[...], m0))

    def one_left(m_run):
        return weights(n_groups - 1, sa_sc, mxa, m_run)

    def two_left(m_run):
        mxb, m_run = overlapped(n_groups - 1, sb_sc, n_groups - 2, sa_sc, mxa, m_run)
        return weights(n_groups - 1, sb_sc, mxb, m_run)

    m_run = lax.cond(n_groups - 1 == 2 * n_pairs, one_left, two_left, m_run)

    lane = lax.broadcasted_iota(jnp.int32, (1, MLA_QK), 1)
    onehot = (lane >= MLA_KV_RANK + 2) & (lane < MLA_KV_RANK + 2 + MLA_BLOCKS)
    hcols = MLA_HEADS * BLOCK
    causal = lax.broadcasted_iota(jnp.int32, (BLOCK, hcols), 0) <= (
        lax.broadcasted_iota(jnp.int32, (BLOCK, hcols), 1) & (BLOCK - 1))
    row0s = [pl.multiple_of((i * nsub + r) * BLOCK, BLOCK) for r in range(nsub)]
    lanes = [[slice(hd * tq + r * BLOCK, hd * tq + (r + 1) * BLOCK) for hd in range(MLA_HEADS)]
             for r in range(nsub)]
    qn = qn_ref[0].reshape(cols, MLA_QK)
    ahead = [scores_part(0, sa_sc, 0, qn)]
    sds = []
    for r in range(nsub):
        qd = jnp.concatenate([q[ln, :] for ln in lanes[r]], axis=0)
        qd = jnp.where(onehot, jnp.zeros((), BF16), qd)
        sd = lax.dot_general(k_sc[pl.ds(row0s[r], BLOCK), :], qd, nt_dims, preferred_element_type=F32)
        sds.append(jnp.where(causal, sd, NEG))
    ps, alphas = [], []
    for r in range(nsub):
        m_old = jnp.concatenate([m_run[:, ln] for ln in lanes[r]], axis=1)
        m_new = jnp.maximum(m_old, jnp.max(sds[r], axis=0, keepdims=True))
        ps.append(jnp.exp2(sds[r] - m_new).astype(BF16))
        alphas.append(jnp.exp2(m_old - m_new))
    ahead += [scores_part(0, sa_sc, c, qn) for c in range(1, n_parts)]
    mx_sc[...] = functools.reduce(jnp.maximum, ahead)
    pvs = [lax.dot_general(k_sc[pl.ds(row0s[r], BLOCK), 0:MLA_ACC], ps[r], tn_dims,
                           preferred_element_type=F32) for r in range(nsub)]
    o_parts = []
    for r in range(nsub):
        acc = jnp.concatenate([acc_sc[:, ln] for ln in lanes[r]], axis=1) * alphas[r] + pvs[r]
        o_parts.append((acc[0:MLA_KV_RANK, :] / acc[MLA_KV_RANK:MLA_KV_RANK + 1, :]).astype(BF16))
    ys = [[lax.dot_general(o_parts[r][:, hd * BLOCK:(hd + 1) * BLOCK], wuv_ref[0, hd], tn_dims,
                           preferred_element_type=F32) for hd in range(MLA_HEADS)] for r in range(nsub)]
    o_ref[0] = jnp.concatenate([functools.reduce(jnp.add, ys[r]) for r in range(nsub)], axis=0)


def _mla(qm, kc, key_tail, wuv_heads, layer, *, tq=384, rows=MLA_KEY_ROWS):
    bsz, _, seq, _ = qm.shape
    seq_k = seq + key_tail.shape[0]
    cols = MLA_HEADS * tq
    n_blocks = seq // tq
    return pl.pallas_call(
        functools.partial(_mla_kernel, tq=tq, rows=rows),
        grid=(bsz, seq // tq),
        in_specs=[
            pl.BlockSpec((1, MLA_HEADS, tq, MLA_QK), lambda b, i: (b, 0, i, 0)),
            pl.BlockSpec((1, MLA_HEADS, tq, MLA_QK), lambda b, i: (b, 0, jnp.minimum(i + 1, n_blocks - 1), 0)),
            pl.BlockSpec((1, seq, MLA_QK), lambda b, i: (b, 0, 0)),
            _const_spec((seq_k - seq, MLA_QK)),
            _layer_spec((MLA_HEADS, MLA_KV_RANK, MLA_WIDTH), layer),
        ],
        out_specs=pl.BlockSpec((1, tq, MLA_WIDTH), lambda b, i: (b, i, 0)),
        out_shape=jax.ShapeDtypeStruct((bsz, seq, MLA_WIDTH), F32),
        scratch_shapes=[pltpu.VMEM((seq_k, MLA_QK), BF16), pltpu.VMEM((rows, cols), F32),
                        pltpu.VMEM((rows, cols), F32), pltpu.VMEM((MLA_ACC, cols), F32), pltpu.VMEM((8, cols), F32)],
        compiler_params=pltpu.CompilerParams(
            dimension_semantics=("parallel", "arbitrary"), vmem_limit_bytes=VMEM_LIMIT),
        name="mla",
    )(qm, qm, kc, key_tail, wuv_heads)


def _gelu_tanh(x):
    return 0.5 * x * (1.0 + jnp.tanh(math.sqrt(2.0 / math.pi) * (x + 0.044715 * (x * x * x))))


def _mix_out_kernel(h_ref, head_ref, ys_ref, yw_ref, ym_ref, wglu_ref, bglu_ref, gh_ref, wout_ref, gpm_ref, gpre_ref,
                    wup_ref, wdn_ref, gpost_ref, o_ref, *, ff_chunk, from_x):
    z = _gelu_tanh(jnp.concatenate([ys_ref[0, 0], ys_ref[0, 1]], axis=1))
    gate = jnp.dot(z.astype(BF16), wglu_ref[0], preferred_element_type=F32) + bglu_ref[...]
    y_ssm = z * (1.0 / (1.0 + jnp.exp(-gate)))
    gh = gh_ref[...]
    mixed = jnp.concatenate([
        _rms(y_ssm, gh[:, 0:SSM_WIDTH]),
        _rms(yw_ref[0], gh[:, SSM_WIDTH:SSM_WIDTH + SWA_WIDTH]),
        _rms(ym_ref[0], gh[:, SSM_WIDTH + SWA_WIDTH:]),
    ], axis=1).astype(BF16)
    mo = jnp.dot(mixed, wout_ref[0], preferred_element_type=F32)
    h_rows = _seq_rows(h_ref, head_ref, pl.program_id(1), 0, o_ref.shape[1]) if from_x else h_ref[0]
    h1 = h_rows + _rms(mo, gpm_ref[...])
    a = _rms(h1, gpre_ref[...]).astype(BF16)
    f = jnp.zeros_like(h1)
    for c in range(D_FF // ff_chunk):
        up = jnp.dot(a, wup_ref[0, :, c * ff_chunk:(c + 1) * ff_chunk], preferred_element_type=F32)
        hid = jnp.square(jnp.maximum(up, 0.0)).astype(BF16)
        f += jnp.dot(hid, wdn_ref[0, c * ff_chunk:(c + 1) * ff_chunk, :], preferred_element_type=F32)
    o_ref[0] = h1 + _rms(f, gpost_ref[...])


def _mix_out(h, head, ys, yw, ym, wglu, bglu, gh, wout, gpm, gpre, wup, wdn, gpost, layer, *, from_x=False, skip=0,
             tm=1056, ff_chunk=1024):
    assert not (from_x and skip)
    bsz, seq = yw.shape[0], yw.shape[1]
    out_rows = seq - skip
    if skip:
        el = pl.Element
        first = lambda i: pl.multiple_of(skip + i * tm, BLOCK)
        row = lambda w: pl.BlockSpec((el(1), el(tm), el(w)), lambda b, i: (b, first(i), 0))
        slabs = pl.BlockSpec((el(1), el(2), el(tm), el(128)), lambda b, i: (b, 0, first(i), 0))
    else:
        row = lambda w: pl.BlockSpec((1, tm, w), lambda b, i: (b, i, 0))
        slabs = pl.BlockSpec((1, 2, tm, 128), lambda b, i: (b, 0, i, 0))
    return pl.pallas_call(
        functools.partial(_mix_out_kernel, ff_chunk=ff_chunk, from_x=from_x),
        grid=(bsz, out_rows // tm),
        in_specs=[
            _seq_window_spec(tm, D_MODEL) if from_x else row(D_MODEL), _const_spec((BLOCK, D_MODEL)),
            slabs, row(SWA_WIDTH), row(MLA_WIDTH),
            _layer_spec((SSM_WIDTH, SSM_WIDTH), layer), _const_spec((1, SSM_WIDTH)), _const_spec((1, D_MODEL)),
            _layer_spec((D_MODEL, D_MODEL), layer), _const_spec((1, D_MODEL)), _const_spec((1, D_MODEL)),
            _layer_spec((D_MODEL, D_FF), layer), _layer_spec((D_FF, D_MODEL), layer), _const_spec((1, D_MODEL)),
        ],
        out_specs=pl.BlockSpec((1, tm, D_MODEL), lambda b, i: (b, i, 0)),
        out_shape=jax.ShapeDtypeStruct((bsz, out_rows, D_MODEL), F32),
        compiler_params=pltpu.CompilerParams(
            dimension_semantics=("parallel", "parallel"), vmem_limit_bytes=VMEM_LIMIT),
        name="mix_out",
    )(h, head, ys, yw, ym, wglu, bglu, gh, wout, gpm, gpre, wup, wdn, gpost)


def _rot_half_cols(w):
    half = w.shape[-1] // 2
    return jnp.concatenate([-w[..., half:], w[..., :half]], axis=-1)


def _stacked_weights(w_in, w_uq, w_uk, w_uv):
    depth = w_in.shape[0]
    w_in, w_uq, w_uk, w_uv = (w.astype(BF16) for w in (w_in, w_uq, w_uk, w_uv))
    z = lambda n: jnp.zeros((depth, D_MODEL, n), BF16)
    s_cq = SSM_WIDTH + SWA_WIDTH + 2 * SWA_KV_WIDTH
    s_ckv = s_cq + MLA_Q_RANK
    s_kr = s_ckv + MLA_KV_RANK
    kr = w_in[:, :, s_kr:s_kr + MLA_ROPE]
    w1 = jnp.concatenate([
        w_in[:, :, :s_ckv], z(C_CKV - C_CQ - MLA_Q_RANK),
        w_in[:, :, s_ckv:s_kr], kr, _rot_half_cols(kr), z(C_END - C_KR - 2 * MLA_ROPE)], axis=2)
    uq = w_uq.reshape(depth, MLA_Q_RANK, MLA_HEADS, MLA_NOPE + MLA_ROPE)
    rope = uq[..., MLA_NOPE:]
    wuq = jnp.concatenate([
        uq[..., :MLA_NOPE].reshape(depth, MLA_Q_RANK, -1), rope.reshape(depth, MLA_Q_RANK, -1),
        _rot_half_cols(rope).reshape(depth, MLA_Q_RANK, -1)], axis=2)
    wuq = jnp.pad(wuq, ((0, 0), (0, 256 - MLA_Q_RANK), (0, 0)))
    uk = w_uk.reshape(depth, MLA_KV_RANK, MLA_HEADS, MLA_NOPE).transpose(0, 2, 3, 1)
    wuk_bd = _group_block_diag(uk)
    uv = w_uv.reshape(depth, MLA_KV_RANK, MLA_HEADS, MLA_V).transpose(0, 2, 1, 3)
    wuv_heads = _group_block_diag(uv).reshape(depth, MLA_HEADS, MLA_KV_RANK, MLA_WIDTH)
    return w1, wuq, wuk_bd, wuv_heads


def _constant_tables(seq):
    half = MLA_ROPE // 2
    inv_freq = ROPE_THETA ** (-np.arange(half, dtype=np.float64) / half)
    ang = (np.arange(seq) - PAD)[:, None] * inv_freq[None, :]
    cos_t = np.tile(np.cos(ang), (1, 2 * MLA_HEADS)).astype(np.float32)
    sin_t = np.tile(np.sin(ang), (1, 2 * MLA_HEADS)).astype(np.float32)
    seq_k = -(-seq // MLA_KEY_ROWS) * MLA_KEY_ROWS
    kblk = np.arange(seq_k)[:, None] // BLOCK
    bcol = np.arange(MLA_BLOCKS)[None, :]
    one = np.ones((seq_k, 1))
    fill = np.zeros((seq_k, 128 - 2 - MLA_BLOCKS))
    kx_t = np.concatenate([one, np.where(np.arange(seq_k)[:, None] < PAD, NEG, 0.0), kblk == bcol, fill], axis=1)
    qx_t = np.concatenate([0.0 * one, one, np.where(bcol >= kblk, NEG, 0.0), fill], axis=1)[:seq]
    key_tail = np.zeros((seq_k - seq, MLA_QK))
    key_tail[:, MLA_KV_RANK:MLA_KV_RANK + MLA_EXT] = kx_t[seq:, :MLA_EXT]
    slopes = 2.0 ** (-8.0 * np.arange(1, SWA_HEADS + 1) / SWA_HEADS)
    slope_rows = np.repeat(slopes.reshape(SWA_KV_HEADS, SWA_GROUP), BLOCK, axis=1)[:, None, :]
    dist = (np.arange(BLOCK)[None, :] + BLOCK) - np.arange(2 * BLOCK)[:, None]
    in_window = np.tile((dist >= 0) & (dist < BLOCK), (1, SWA_GROUP))[None, None]
    alibi = (-LOG2E * slope_rows * np.tile(dist, (1, SWA_GROUP))[None])[None]
    key_pos = (np.arange(3)[:, None] - 1) * BLOCK + np.arange(2 * BLOCK)[None, :]
    key_ok = ((key_pos >= PAD) | (np.arange(3)[:, None] >= 2))[:, None, :, None]
    swa_bias = np.where(in_window & key_ok, alibi, NEG).astype(np.float32)
    bf16 = lambda t: jnp.asarray(t, dtype=BF16)
    return (jnp.asarray(cos_t), jnp.asarray(sin_t), bf16(qx_t), bf16(kx_t[:seq]), bf16(key_tail),
            jnp.asarray(swa_bias))


def kernel(x, meta_tokens, norm_pre_mix, norm_post_mix, norm_pre_mlp, norm_post_mlp, w_in, w_out, norm_heads,
           ssm_a_re, ssm_a_im, ssm_log_step, ssm_b_re, ssm_b_im, ssm_c_re, ssm_c_im, ssm_d, ssm_w_glu,
           ssm_b_glu, swa_sinks, mla_q_norm, mla_kv_norm, mla_w_uq, mla_w_uk, mla_w_uv, w_mlp_up, w_mlp_down):
    bsz, seq_real, d = x.shape
    depth = w_in.shape[0]
    head = jnp.concatenate([jnp.zeros((PAD, d), x.dtype), meta_tokens.astype(x.dtype)], axis=0)
    seq = BLOCK + seq_real
    h = x

    cos_t, sin_t, qx_t, kx_t, key_tail, swa_bias = _constant_tables(seq)
    per_row = lambda v: jnp.repeat(v.reshape(SWA_KV_HEADS, SWA_GROUP), BLOCK, axis=1)[:, None, :]

    r_mat, p_mat, q_mat, lam = _s5_operators(
        ssm_a_re, ssm_a_im, ssm_log_step, ssm_b_re, ssm_b_im, ssm_c_re, ssm_c_im)
    row = lambda v: v.reshape(1, -1).astype(F32)
    wglu, wout, wup, wdn = (w.astype(BF16) for w in (ssm_w_glu, w_out, w_mlp_up, w_mlp_down))
    w1, wuq, wuk_bd, wuv_heads = _stacked_weights(w_in, mla_w_uq, mla_w_uk, mla_w_uv)

    for l in range(depth):
        gq = jnp.concatenate([mla_q_norm[l], jnp.zeros((256 - MLA_Q_RANK,), F32)]).reshape(1, 256)
        u, q, kv, qm, kc = _mix_in(h, head, row(norm_pre_mix[l]), w1, gq, row(mla_kv_norm[l]), wuq, wuk_bd,
                                   cos_t, sin_t, qx_t, kx_t, l, from_x=l == 0)
        d_fold = jnp.tile(ssm_d[l].astype(F32), SSM_CHUNK).reshape(1, SSM_FOLD)
        y_ssm = _s5_sequence(u, r_mat, p_mat, q_mat, lam, d_fold, l)
        y_swa = _swa(q, kv, swa_bias, per_row(LOG2E * swa_sinks[l].astype(F32)))
        y_mla = _mla(qm, kc, key_tail, wuv_heads, l)
        last = l == depth - 1
        h = _mix_out(
            h, head, y_ssm, y_swa, y_mla,
            wglu, row(ssm_b_glu[l]), row(norm_heads[l]), wout, row(norm_post_mix[l]), row(norm_pre_mlp[l]),
            wup, wdn, row(norm_post_mlp[l]), l, from_x=l == 0, skip=PAD + N_META if last else 0,
            tm=1024 if last else 1056)
    return h
```

```python
import functools
import math

import jax
import jax.numpy as jnp
import numpy as np
from jax import lax
from jax.experimental import pallas as pl
from jax.experimental.pallas import tpu as pltpu

F32 = jnp.float32
BF16 = jnp.bfloat16

D_MODEL = 1024
N_META = 16
BLOCK = 128
PAD = BLOCK - N_META
SSM_GROUPS = 16
SSM_GROUP_CH = 16
SSM_WIDTH = SSM_GROUPS * SSM_GROUP_CH
SSM_STATE = 64
SWA_HEADS = 8
SWA_KV_HEADS = 2
SWA_GROUP = SWA_HEADS // SWA_KV_HEADS
HEAD_DIM = 64
SWA_WIDTH = SWA_HEADS * HEAD_DIM
SWA_KV_WIDTH = SWA_KV_HEADS * HEAD_DIM
SWA_KV_COLS = SWA_KV_WIDTH + SWA_KV_HEADS * BLOCK
MLA_HEADS = 4
MLA_Q_RANK = 192
MLA_KV_RANK = 128
MLA_NOPE = 64
MLA_ROPE = 32
MLA_V = 64
MLA_WIDTH = MLA_HEADS * MLA_V
MLA_BLOCKS = 40
MLA_EXT = 48
MLA_ACC = MLA_KV_RANK + 8
LOG2E = 1.4426950408889634
MLA_QK = MLA_KV_RANK + MLA_EXT + MLA_ROPE
ROPE_THETA = 10000.0
D_FF = 4 * D_MODEL
EPS = 1e-6
NEG = -1e30

SSM_CHUNK = 8
SSM_FOLD = SSM_CHUNK * SSM_WIDTH
SSM_MODES = SSM_GROUPS * SSM_STATE

C_U, C_Q, C_K, C_CQ, C_CKV, C_KR, C_END = 0, 256, 768, 1024, 1280, 1408, 1536

MLA_KEY_ROWS = 512
MLA_PART = 256

VMEM_LIMIT = 58 * 1024 * 1024


def _rms(x, g, n=None):
    n = x.shape[-1] if n is None else n
    ms = jnp.sum(x * x, axis=-1, keepdims=True) * (1.0 / n)
    return x * lax.rsqrt(ms + EPS) * g


def _const_spec(shape):
    nd = len(shape)
    return pl.BlockSpec(shape, lambda *_: (0,) * nd, pipeline_mode=pl.Buffered(1))


def _seq_window_spec(tm, width):
    el = pl.Element
    return pl.BlockSpec((el(1), el(tm), el(width)),
                        lambda b, i: (b, pl.multiple_of(jnp.maximum(i * tm - BLOCK, 0), 8), 0))


def _seq_rows(x_ref, head_ref, i, lo, n):
    first = jnp.where(i == 0, 1, 0)
    if lo == 0:
        top = jnp.where(i == 0, head_ref[...], x_ref[0, 0:BLOCK, :])
        rest = x_ref[0, pl.ds(pl.multiple_of(BLOCK - BLOCK * first, 8), n - BLOCK), :]
        return jnp.concatenate([top, rest], axis=0)
    return x_ref[0, pl.ds(pl.multiple_of(lo - BLOCK * first, 8), n), :]


def _layer_spec(shape, layer):
    nd = len(shape)
    return pl.BlockSpec((1,) + shape, lambda *_: (layer,) + (0,) * nd, pipeline_mode=pl.Buffered(1))


def _mix_in_kernel(h_ref, head_ref, g_ref, w1_ref, gq_ref, gkv_ref, wuq_ref, wuk_ref, cos_ref, sin_ref, qx_ref,
                   kx_ref, u_ref, q_ref, kv_ref, qm_ref, kc_ref, *, tm, nsub, from_x):
    j = pl.program_id(1)
    ts = tm // nsub
    scale = LOG2E * (MLA_NOPE + MLA_ROPE) ** -0.5
    ext = slice(MLA_KV_RANK, MLA_KV_RANK + MLA_EXT)
    rot = slice(MLA_KV_RANK + MLA_EXT, MLA_QK)
    for sub in range(nsub):
        rows = slice(sub * ts, (sub + 1) * ts)
        h_rows = _seq_rows(h_ref, head_ref, j, sub * ts, ts) if from_x else h_ref[0, rows, :]
        a = _rms(h_rows, g_ref[...]).astype(BF16)
        proj = jnp.dot(a, w1_ref[0], preferred_element_type=F32)
        row = j * tm + sub * ts + lax.broadcasted_iota(jnp.int32, (ts, 1), 0)
        for c in range(2):
            u_ref[0, c, rows, :] = jnp.where(row >= PAD, proj[:, C_U + c * 128:C_U + (c + 1) * 128], 0.0)
        q_ref[0, rows, :] = (proj[:, C_Q:C_K] * (LOG2E * HEAD_DIM ** -0.5)).astype(BF16)
        kv_ref[0, rows, 0:SWA_KV_WIDTH] = proj[:, C_K:C_K + SWA_KV_WIDTH].astype(BF16)
        ones_col = jnp.where(lax.broadcasted_iota(jnp.int32, (ts, HEAD_DIM), 1) == 0, 1.0, 0.0).astype(BF16)
        for hk in range(SWA_KV_HEADS):
            lo = SWA_KV_WIDTH + hk * BLOCK
            v_lo = C_K + SWA_KV_WIDTH + hk * HEAD_DIM
            kv_ref[0, rows, lo:lo + HEAD_DIM] = proj[:, v_lo:v_lo + HEAD_DIM].astype(BF16)
            kv_ref[0, rows, lo + HEAD_DIM:lo + BLOCK] = ones_col

        cos = cos_ref[rows, :]
        sin = sin_ref[rows, :]
        qn = _rms(proj[:, C_CQ:C_CKV], gq_ref[...], n=MLA_Q_RANK).astype(BF16)
        q2 = jnp.dot(qn, wuq_ref[0], preferred_element_type=F32)
        qr = (q2[:, 256:384] * cos + q2[:, 384:512] * sin) * scale
        qa = jnp.dot(q2[:, 0:256].astype(BF16), wuk_ref[0], preferred_element_type=F32) * scale
        c = _rms(proj[:, C_CKV:C_KR], gkv_ref[...])
        krr = proj[:, C_KR:C_END]
        kr = krr[:, 0:MLA_ROPE] * cos[:, 0:MLA_ROPE] + krr[:, MLA_ROPE:2 * MLA_ROPE] * sin[:, 0:MLA_ROPE]
        for hd in range(MLA_HEADS):
            qm_ref[0, hd, rows, 0:MLA_KV_RANK] = qa[:, hd * MLA_KV_RANK:(hd + 1) * MLA_KV_RANK].astype(BF16)
            qm_ref[0, hd, rows, ext] = qx_ref[rows, 0:MLA_EXT]
            qm_ref[0, hd, rows, rot] = qr[:, hd * MLA_ROPE:(hd + 1) * MLA_ROPE].astype(BF16)
        kc_ref[0, rows, 0:MLA_KV_RANK] = c.astype(BF16)
        kc_ref[0, rows, ext] = kx_ref[rows, 0:MLA_EXT]
        kc_ref[0, rows, rot] = kr.astype(BF16)


def _mix_in(h, head, g, w1, gq, gkv, wuq, wuk, cos_t, sin_t, qx_t, kx_t, layer, *, from_x, tm=2112, nsub=4):
    bsz = h.shape[0]
    seq = cos_t.shape[0]
    grid = (bsz, seq // tm)
    row3 = lambda b, j: (b, j, 0)
    return pl.pallas_call(
        functools.partial(_mix_in_kernel, tm=tm, nsub=nsub, from_x=from_x),
        grid=grid,
        in_specs=[
            _seq_window_spec(tm, D_MODEL) if from_x else pl.BlockSpec((1, tm, D_MODEL), row3),
            _const_spec((BLOCK, D_MODEL)),
            _const_spec((1, D_MODEL)),
            _layer_spec((D_MODEL, C_END), layer),
            _const_spec((1, 256)),
            _const_spec((1, MLA_KV_RANK)),
            _layer_spec((256, 512), layer),
            _layer_spec((256, 512), layer),
            pl.BlockSpec((tm, 128), lambda b, j: (j, 0)),
            pl.BlockSpec((tm, 128), lambda b, j: (j, 0)),
            pl.BlockSpec((tm, 128), lambda b, j: (j, 0)),
            pl.BlockSpec((tm, 128), lambda b, j: (j, 0)),
        ],
        out_specs=[
            pl.BlockSpec((1, 2, tm, 128), lambda b, j: (b, 0, j, 0)),
            pl.BlockSpec((1, tm, SWA_WIDTH), row3),
            pl.BlockSpec((1, tm, SWA_KV_COLS), row3),
            pl.BlockSpec((1, MLA_HEADS, tm, MLA_QK), lambda b, j: (b, 0, j, 0)),
            pl.BlockSpec((1, tm, MLA_QK), row3),
        ],
        out_shape=[
            jax.ShapeDtypeStruct((bsz, 2, seq, 128), F32),
            jax.ShapeDtypeStruct((bsz, seq, SWA_WIDTH), BF16),
            jax.ShapeDtypeStruct((bsz, seq, SWA_KV_COLS), BF16),
            jax.ShapeDtypeStruct((bsz, MLA_HEADS, seq, MLA_QK), BF16),
            jax.ShapeDtypeStruct((bsz, seq, MLA_QK), BF16),
        ],
        compiler_params=pltpu.CompilerParams(
            dimension_semantics=("parallel", "parallel"), vmem_limit_bytes=VMEM_LIMIT),
        name="mix_in",
    )(h, head, g, w1, gq, gkv, wuq, wuk, cos_t, sin_t, qx_t, kx_t)


def _split_bf16(x):
    hi = x.astype(BF16)
    lo = (x - hi.astype(F32)).astype(BF16)
    return hi, lo


def _bdot3(a, b):
    ah, al = _split_bf16(a)
    bh, bl = _split_bf16(b)
    dn = (((2,), (2,)), ((0,), (0,)))
    f = lambda x, y: lax.dot_general(x, y, dn, preferred_element_type=F32)
    return f(ah, bh) + f(ah, bl) + f(al, bh)


def _expand_groups(a2, row_shift, col_shift, n_cols):
    rows, w = a2.shape
    tile = (lax.broadcasted_iota(jnp.int32, (w, n_cols), 1) & (w - 1)) == lax.broadcasted_iota(
        jnp.int32, (w, n_cols), 0)
    wide = jnp.dot(a2.astype(BF16), jnp.where(tile, 1.0, 0.0).astype(BF16), preferred_element_type=F32)
    same = (lax.broadcasted_iota(jnp.int32, (rows, n_cols), 0) >> row_shift) == (
        lax.broadcasted_iota(jnp.int32, (rows, n_cols), 1) >> col_shift)
    return jnp.where(same, wide, 0.0)


def _s5_operator_kernel(ar_ref, ai_ref, ls_ref, btr_ref, bti_ref, cr_ref, ci_ref,
                        r_ref, p_ref, q_ref, l8r_ref, l8i_ref, pwr_sc, pwi_sc, kk_sc):
    s = pl.program_id(1)
    ar, ai = ar_ref[0], ai_ref[0]
    step = jnp.exp(ls_ref[0])
    lr = jnp.exp(ar * step) * jnp.cos(ai * step)
    li = jnp.exp(ar * step) * jnp.sin(ai * step)

    den = ar * ar + ai * ai
    nr, ni = lr - 1.0, li
    coef_re = (nr * ar + ni * ai) / den
    coef_im = (ni * ar - nr * ai) / den
    btr, bti = btr_ref[0], bti_ref[0]
    bb_re = coef_re * btr - coef_im * bti
    bb_im = coef_re * bti + coef_im * btr
    cr, ci = cr_ref[0], ci_ref[0]

    def c_lam(k):
        qr_, qi_ = pwr_sc[k], pwi_sc[k]
        return cr * qr_ - ci * qi_, cr * qi_ + ci * qr_

    gc = SSM_GROUPS * SSM_GROUP_CH

    @pl.when(s == 0)
    def _():
        pr, pi = jnp.ones_like(lr), jnp.zeros_like(li)
        for k in range(SSM_CHUNK + 1):
            pwr_sc[k] = pr
            pwi_sc[k] = pi
            pr, pi = pr * lr - pi * li, pr * li + pi * lr
        for k in range(SSM_CHUNK):
            clr, cli = c_lam(k)
            kk_sc[k] = (_bdot3(bb_re, clr) - _bdot3(bb_im, cli)).reshape(gc, SSM_GROUP_CH)

    for t in range(SSM_CHUNK):
        lag = t - s
        blk = _expand_groups(kk_sc[jnp.maximum(lag, 0)], 4, 4, gc)
        r_ref[0, :, t * gc:(t + 1) * gc] = jnp.where(lag >= 0, blk, 0.0).astype(BF16)

    pw_r, pw_i = pwr_sc[SSM_CHUNK - 1 - s], pwi_sc[SSM_CHUNK - 1 - s]
    p_re = (pw_r * bb_re - pw_i * bb_im).reshape(gc, SSM_STATE)
    p_im = (pw_r * bb_im + pw_i * bb_re).reshape(gc, SSM_STATE)
    p_ref[0, :, 0:SSM_MODES] = _expand_groups(p_re, 4, 6, SSM_MODES).astype(BF16)
    p_ref[0, :, SSM_MODES:2 * SSM_MODES] = _expand_groups(p_im, 4, 6, SSM_MODES).astype(BF16)

    clr, cli = c_lam(s + 1)
    q_ref[0, 0:SSM_MODES, :] = _expand_groups(clr.reshape(gc, SSM_STATE), 4, 6, SSM_MODES).T.astype(BF16)
    q_ref[0, SSM_MODES:2 * SSM_MODES, :] = (
        -_expand_groups(cli.reshape(gc, SSM_STATE), 4, 6, SSM_MODES)).T.astype(BF16)

    l8r_ref[0] = pwr_sc[SSM_CHUNK]
    l8i_ref[0] = pwi_sc[SSM_CHUNK]


def _s5_operators(a_re, a_im, log_step, b_re, b_im, c_re, c_im):
    depth = a_re.shape[0]
    G, P, C, T = SSM_GROUPS, SSM_STATE, SSM_GROUP_CH, SSM_CHUNK
    lay = lambda *shape: pl.BlockSpec((1,) + shape, lambda l, s: (l,) + (0,) * len(shape))
    ls = jnp.broadcast_to(log_step[:, :, None, None], (depth, G, 1, P))
    args = (a_re[:, :, None, :], a_im[:, :, None, :], ls,
            b_re.transpose(0, 1, 3, 2), b_im.transpose(0, 1, 3, 2), c_re, c_im)
    r_mat, p_mat, q_mat, l8r, l8i = pl.pallas_call(
        _s5_operator_kernel,
        grid=(depth, T),
        in_specs=[lay(G, 1, P)] * 3 + [lay(G, C, P)] * 4,
        out_specs=[
            pl.BlockSpec((1, SSM_WIDTH, SSM_FOLD), lambda l, s: (l, s, 0)),
            pl.BlockSpec((1, SSM_WIDTH, 2 * SSM_MODES), lambda l, s: (l, s, 0)),
            pl.BlockSpec((1, 2 * SSM_MODES, SSM_WIDTH), lambda l, s: (l, 0, s)),
            lay(G, 1, P), lay(G, 1, P)],
        out_shape=[
            jax.ShapeDtypeStruct((depth, SSM_FOLD, SSM_FOLD), BF16),
            jax.ShapeDtypeStruct((depth, SSM_FOLD, 2 * SSM_MODES), BF16),
            jax.ShapeDtypeStruct((depth, 2 * SSM_MODES, SSM_FOLD), BF16),
            jax.ShapeDtypeStruct((depth, G, 1, P), F32),
            jax.ShapeDtypeStruct((depth, G, 1, P), F32),
        ],
        scratch_shapes=[pltpu.VMEM((T + 1, G, 1, P), F32), pltpu.VMEM((T + 1, G, 1, P), F32),
                        pltpu.VMEM((T, G * C, C), F32)],
        compiler_params=pltpu.CompilerParams(dimension_semantics=("parallel", "arbitrary")),
        name="s5_operators",
    )(*args)
    lam = jnp.concatenate([l8r.reshape(depth, 1, SSM_MODES), l8i.reshape(depth, 1, SSM_MODES)], axis=2)
    return r_mat, p_mat, q_mat, lam


def _group_block_diag(x):
    *lead, G, r, c = x.shape
    eye = jnp.eye(G, dtype=x.dtype)
    y = x[..., :, :, None, :] * eye[:, None, :, None]
    return y.reshape(*lead, G * r, G * c)


def _fold_tokens(u_ref, n_chunks):
    return jnp.concatenate(
        [u_ref[0, c, pl.ds(s, n_chunks, stride=SSM_CHUNK), :] for s in range(SSM_CHUNK) for c in range(2)], axis=1)


def _s5_kernel(u_ref, p_ref, lam_ref, r_ref, q_ref, d_ref, y_ref, s_sc, *, n_chunks):
    u = _fold_tokens(u_ref, n_chunks)
    ub = u.astype(BF16)
    s_sc[...] = jnp.dot(ub, p_ref[0], preferred_element_type=F32)
    lr = lam_ref[0, :, 0:SSM_MODES]
    li = lam_ref[0, :, SSM_MODES:2 * SSM_MODES]

    def body(n, carry):
        xr, xi = carry
        sr = s_sc[pl.ds(n, 1), 0:SSM_MODES]
        si = s_sc[pl.ds(n, 1), SSM_MODES:2 * SSM_MODES]
        s_sc[pl.ds(n, 1), 0:SSM_MODES] = xr
        s_sc[pl.ds(n, 1), SSM_MODES:2 * SSM_MODES] = xi
        return lr * xr - li * xi + sr, lr * xi + li * xr + si

    z = jnp.zeros((1, SSM_MODES), F32)
    lax.fori_loop(0, n_chunks, body, (z, z))
    x = s_sc[...].astype(BF16)
    for t in range(SSM_CHUNK):
        cols = slice(t * SSM_WIDTH, (t + 1) * SSM_WIDTH)
        rows = (t + 1) * SSM_WIDTH
        y = jnp.dot(ub[:, 0:rows], r_ref[0, 0:rows, cols], preferred_element_type=F32)
        y += jnp.dot(x, q_ref[0, :, cols], preferred_element_type=F32)
        y += d_ref[:, cols] * u[:, cols]
        for c in range(2):
            y_ref[0, c, pl.ds(t, n_chunks, stride=SSM_CHUNK), :] = y[:, c * 128:(c + 1) * 128]


def _s5_sequence(u, r_mat, p_mat, q_mat, lam, d_fold, layer):
    bsz, _, seq, _ = u.shape
    n_chunks = seq // SSM_CHUNK
    slabs = pl.BlockSpec((1, 2, seq, 128), lambda b: (b, 0, 0, 0))
    return pl.pallas_call(
        functools.partial(_s5_kernel, n_chunks=n_chunks), grid=(bsz,),
        in_specs=[slabs, _layer_spec((SSM_FOLD, 2 * SSM_MODES), layer), _layer_spec((1, 2 * SSM_MODES), layer),
                  _layer_spec((SSM_FOLD, SSM_FOLD), layer), _layer_spec((2 * SSM_MODES, SSM_FOLD), layer),
                  _const_spec((1, SSM_FOLD))],
        out_specs=slabs,
        out_shape=jax.ShapeDtypeStruct((bsz, 2, seq, 128), F32),
        scratch_shapes=[pltpu.VMEM((n_chunks, 2 * SSM_MODES), F32)],
        compiler_params=pltpu.CompilerParams(dimension_semantics=("parallel",), vmem_limit_bytes=VMEM_LIMIT),
        name="s5_sequence",
    )(u, p_mat, lam, r_mat, q_mat, d_fold)


def _swa_kernel(q_ref, kvp_ref, kvc_ref, bias_ref, sink_ref, o_ref, *, nsub):
    i = pl.program_id(1)
    nt_dims = (((1,), (1,)), ((), ()))
    tn_dims = (((0,), (0,)), ((), ()))
    chains = [(r, hk) for r in range(nsub) for hk in range(SWA_KV_HEADS)]
    scores, vcats = [], []
    for r, hk in chains:
        rows = slice(r * BLOCK, (r + 1) * BLOCK)
        q = q_ref[0, rows, :]
        kvc = kvc_ref[0, rows, :]
        kvp = kvp_ref[0] if r == 0 else kvc_ref[0, (r - 1) * BLOCK:r * BLOCK, :]
        ks = slice(hk * HEAD_DIM, (hk + 1) * HEAD_DIM)
        vs = slice(SWA_KV_WIDTH + hk * BLOCK, SWA_KV_WIDTH + (hk + 1) * BLOCK)
        kcat = jnp.concatenate([kvp[:, ks], kvc[:, ks]], axis=0)
        vcats.append(jnp.concatenate([kvp[:, vs], kvc[:, vs]], axis=0))
        qs = jnp.concatenate(
            [q[:, (hk * SWA_GROUP + g) * HEAD_DIM:(hk * SWA_GROUP + g + 1) * HEAD_DIM]
             for g in range(SWA_GROUP)], axis=0)
        table = jnp.minimum(i * nsub + r, 2)
        scores.append(lax.dot_general(kcat, qs, nt_dims, preferred_element_type=F32) + bias_ref[table, hk])
    weights, sink_terms = [], []
    for (r, hk), s in zip(chains, scores):
        sink = sink_ref[hk]
        m = jnp.maximum(jnp.max(s, axis=0, keepdims=True), sink)
        weights.append(jnp.exp2(s - m).astype(BF16))
        sink_terms.append(jnp.exp2(sink - m))
    o_ts = []
    for v, p, sink_term in zip(vcats, weights, sink_terms):
        o_t = lax.dot_general(v, p, tn_dims, preferred_element_type=F32)
        o_ts.append(o_t[0:HEAD_DIM, :] / (o_t[HEAD_DIM:HEAD_DIM + 1, :] + sink_term))
    for r in range(nsub):
        outs = []
        for hk in range(SWA_KV_HEADS):
            o_t = o_ts[r * SWA_KV_HEADS + hk]
            outs += [o_t[:, g * BLOCK:(g + 1) * BLOCK] for g in range(SWA_GROUP)]
        o_ref[0, r * BLOCK:(r + 1) * BLOCK, :] = jnp.concatenate(outs, axis=0).T


def _swa(q, kv, bias, sink_rows, *, nsub=11):
    bsz, seq, _ = q.shape
    tm = nsub * BLOCK
    cols = SWA_GROUP * BLOCK
    return pl.pallas_call(
        functools.partial(_swa_kernel, nsub=nsub),
        grid=(bsz, seq // tm),
        in_specs=[
            pl.BlockSpec((1, tm, SWA_WIDTH), lambda b, i: (b, i, 0)),
            pl.BlockSpec((1, BLOCK, SWA_KV_COLS), lambda b, i: (b, jnp.maximum(nsub * i - 1, 0), 0)),
            pl.BlockSpec((1, tm, SWA_KV_COLS), lambda b, i: (b, i, 0)),
            _const_spec((3, SWA_KV_HEADS, 2 * BLOCK, cols)),
            _const_spec((SWA_KV_HEADS, 1, cols)),
        ],
        out_specs=pl.BlockSpec((1, tm, SWA_WIDTH), lambda b, i: (b, i, 0)),
        out_shape=jax.ShapeDtypeStruct((bsz, seq, SWA_WIDTH), F32),
        compiler_params=pltpu.CompilerParams(
            dimension_semantics=("parallel", "parallel"), vmem_limit_bytes=VMEM_LIMIT),
        name="swa",
    )(q, kv, kv, bias, sink_rows)


def _mla_kernel(q_ref, qn_ref, kraw_ref, ktail_ref, wuv_ref, o_ref, k_sc, sa_sc, sb_sc, acc_sc, mx_sc, *, tq, rows):
    i = pl.program_id(1)
    cols = MLA_HEADS * tq
    nsub = tq // BLOCK
    seq = kraw_ref.shape[1]
    q = q_ref[0].reshape(cols, MLA_QK)
    n_groups = ((i + 1) * tq + rows - 1) // rows
    nt_dims = (((1,), (1,)), ((), ()))
    tn_dims = (((0,), (0,)), ((), ()))

    def fold(x, op):
        return op(x.reshape(x.shape[0] // 8, 8, x.shape[1]), axis=0)

    n_parts = rows // MLA_PART

    def scores_part(g, buf, c, qq=q):
        start = pl.multiple_of(g * rows + c * MLA_PART, MLA_PART)
        s = lax.dot_general(k_sc[pl.ds(start, MLA_PART), :], qq, nt_dims, preferred_element_type=F32)
        buf[c * MLA_PART:(c + 1) * MLA_PART, :] = s
        return fold(s, jnp.max)

    def rescale(mx8, m_run):
        m_new = jnp.maximum(m_run, jnp.max(mx8, axis=0, keepdims=True))
        acc_sc[...] = acc_sc[...] * jnp.exp2(m_run - m_new)
        return m_new

    def weights_part(g, buf, c, m_new):
        start = pl.multiple_of(g * rows + c * MLA_PART, MLA_PART)
        p = jnp.exp2(buf[c * MLA_PART:(c + 1) * MLA_PART, :] - m_new).astype(BF16)
        v = k_sc[pl.ds(start, MLA_PART), 0:MLA_ACC]
        acc_sc[...] += lax.dot_general(v, p, tn_dims, preferred_element_type=F32)

    def scores(g, buf):
        return functools.reduce(jnp.maximum, [scores_part(g, buf, c) for c in range(n_parts)])

    def weights(g, buf, mx8, m_run):
        m_new = rescale(mx8, m_run)
        for c in range(n_parts):
            weights_part(g, buf, c, m_new)
        return m_new

    def overlapped(g_next, buf_next, g, buf, mx8, m_run):
        m_new = rescale(mx8, m_run)
        parts = []
        for c in range(n_parts):
            parts.append(scores_part(g_next, buf_next, c))
            weights_part(g, buf, c, m_new)
        return functools.reduce(jnp.maximum, parts), m_new

    @pl.when(i == 0)
    def _():
        k_sc[0:seq, :] = kraw_ref[0]
        k_sc[seq:, :] = ktail_ref[...]
        mx_sc[...] = scores(0, sa_sc)

    acc_sc[...] = jnp.zeros((MLA_ACC, cols), F32)
    m0 = jnp.full((1, cols), NEG, F32)

    def pair(k, carry):
        mxa, m_run = carry
        mxb, m_run = overlapped(2 * k + 1, sb_sc, 2 * k, sa_sc, mxa, m_run)
        mxa, m_run = overlapped(2 * k + 2, sa_sc, 2 * k + 1, sb_sc, mxb, m_run)
        return mxa, m_run

    n_pairs = (n_groups - 1) // 2
    mxa, m_run = lax.fori_loop(0, n_pairs, pair, (mx_sc[...], m0))

    def one_left(m_run):
        return weights(n_groups - 1, sa_sc, mxa, m_run)

    def two_left(m_run):
        mxb, m_run = overlapped(n_groups - 1, sb_sc, n_groups - 2, sa_sc, mxa, m_run)
        return weights(n_groups - 1, sb_sc, mxb, m_run)

    m_run = lax.cond(n_groups - 1 == 2 * n_pairs, one_left, two_left, m_run)

    lane = lax.broadcasted_iota(jnp.int32, (1, MLA_QK), 1)
    onehot = (lane >= MLA_KV_RANK + 2) & (lane < MLA_KV_RANK + 2 + MLA_BLOCKS)
    hcols = MLA_HEADS * BLOCK
    causal = lax.broadcasted_iota(jnp.int32, (BLOCK, hcols), 0) <= (
        lax.broadcasted_iota(jnp.int32, (BLOCK, hcols), 1) & (BLOCK - 1))
    row0s = [pl.multiple_of((i * nsub + r) * BLOCK, BLOCK) for r in range(nsub)]
    lanes = [[slice(hd * tq + r * BLOCK, hd * tq + (r + 1) * BLOCK) for hd in range(MLA_HEADS)]
             for r in range(nsub)]
    qn = qn_ref[0].reshape(cols, MLA_QK)
    ahead = [scores_part(0, sa_sc, 0, qn)]
    sds = []
    for r in range(nsub):
        qd = jnp.concatenate([q[ln, :] for ln in lanes[r]], axis=0)
        qd = jnp.where(onehot, jnp.zeros((), BF16), qd)
        sd = lax.dot_general(k_sc[pl.ds(row0s[r], BLOCK), :], qd, nt_dims, preferred_element_type=F32)
        sds.append(jnp.where(causal, sd, NEG))
    ps, alphas = [], []
    for r in range(nsub):
        m_old = jnp.concatenate([m_run[:, ln] for ln in lanes[r]], axis=1)
        m_new = jnp.maximum(m_old, jnp.max(sds[r], axis=0, keepdims=True))
        ps.append(jnp.exp2(sds[r] - m_new).astype(BF16))
        alphas.append(jnp.exp2(m_old - m_new))
    ahead += [scores_part(0, sa_sc, c, qn) for c in range(1, n_parts)]
    mx_sc[...] = functools.reduce(jnp.maximum, ahead)
    pvs = [lax.dot_general(k_sc[pl.ds(row0s[r], BLOCK), 0:MLA_ACC], ps[r], tn_dims,
                           preferred_element_type=F32) for r in range(nsub)]
    o_parts = []
    for r in range(nsub):
        acc = jnp.concatenate([acc_sc[:, ln] for ln in lanes[r]], axis=1) * alphas[r] + pvs[r]
        o_parts.append((acc[0:MLA_KV_RANK, :] / acc[MLA_KV_RANK:MLA_KV_RANK + 1, :]).astype(BF16))
    ys = [[lax.dot_general(o_parts[r][:, hd * BLOCK:(hd + 1) * BLOCK], wuv_ref[0, hd], tn_dims,
                           preferred_element_type=F32) for hd in range(MLA_HEADS)] for r in range(nsub)]
    o_ref[0] = jnp.concatenate([functools.reduce(jnp.add, ys[r]) for r in range(nsub)], axis=0)


def _mla(qm, kc, key_tail, wuv_heads, layer, *, tq=384, rows=MLA_KEY_ROWS):
    bsz, _, seq, _ = qm.shape
    seq_k = seq + key_tail.shape[0]
    cols = MLA_HEADS * tq
    n_blocks = seq // tq
    return pl.pallas_call(
        functools.partial(_mla_kernel, tq=tq, rows=rows),
        grid=(bsz, seq // tq),
        in_specs=[
            pl.BlockSpec((1, MLA_HEADS, tq, MLA_QK), lambda b, i: (b, 0, i, 0)),
            pl.BlockSpec((1, MLA_HEADS, tq, MLA_QK), lambda b, i: (b, 0, jnp.minimum(i + 1, n_blocks - 1), 0)),
            pl.BlockSpec((1, seq, MLA_QK), lambda b, i: (b, 0, 0)),
            _const_spec((seq_k - seq, MLA_QK)),
            _layer_spec((MLA_HEADS, MLA_KV_RANK, MLA_WIDTH), layer),
        ],
        out_specs=pl.BlockSpec((1, tq, MLA_WIDTH), lambda b, i: (b, i, 0)),
        out_shape=jax.ShapeDtypeStruct((bsz, seq, MLA_WIDTH), F32),
        scratch_shapes=[pltpu.VMEM((seq_k, MLA_QK), BF16), pltpu.VMEM((rows, cols), F32),
                        pltpu.VMEM((rows, cols), F32), pltpu.VMEM((MLA_ACC, cols), F32), pltpu.VMEM((8, cols), F32)],
        compiler_params=pltpu.CompilerParams(
            dimension_semantics=("parallel", "arbitrary"), vmem_limit_bytes=VMEM_LIMIT),
        name="mla",
    )(qm, qm, kc, key_tail, wuv_heads)


def _gelu_tanh(x):
    return 0.5 * x * (1.0 + jnp.tanh(math.sqrt(2.0 / math.pi) * (x + 0.044715 * (x * x * x))))


def _mix_out_kernel(h_ref, head_ref, ys_ref, yw_ref, ym_ref, wglu_ref, bglu_ref, gh_ref, wout_ref, gpm_ref, gpre_ref,
                    wup_ref, wdn_ref, gpost_ref, o_ref, *, ff_chunk, from_x, nsub):
    tm = o_ref.shape[1]
    ts = tm // nsub
    subs = [slice(sub * ts, (sub + 1) * ts) for sub in range(nsub)]
    gh = gh_ref[...]
    zs = [_gelu_tanh(jnp.concatenate([ys_ref[0, 0, rows, :], ys_ref[0, 1, rows, :]], axis=1)) for rows in subs]
    gates = [jnp.dot(z.astype(BF16), wglu_ref[0], preferred_element_type=F32) + bglu_ref[...] for z in zs]
    mixeds = []
    for rows, z, gate in zip(subs, zs, gates):
        y_ssm = z * (1.0 / (1.0 + jnp.exp(-gate)))
        mixeds.append(jnp.concatenate([
            _rms(y_ssm, gh[:, 0:SSM_WIDTH]),
            _rms(yw_ref[0, rows, :], gh[:, SSM_WIDTH:SSM_WIDTH + SWA_WIDTH]),
            _rms(ym_ref[0, rows, :], gh[:, SSM_WIDTH + SWA_WIDTH:]),
        ], axis=1).astype(BF16))
    mos = [jnp.dot(mixed, wout_ref[0], preferred_element_type=F32) for mixed in mixeds]
    h1s = []
    for sub, (rows, mo) in enumerate(zip(subs, mos)):
        h_rows = _seq_rows(h_ref, head_ref, pl.program_id(1), sub * ts, ts) if from_x else h_ref[0, rows, :]
        h1s.append(h_rows + _rms(mo, gpm_ref[...]))
    acts = [_rms(h1, gpre_ref[...]).astype(BF16) for h1 in h1s]
    fs = [jnp.zeros_like(h1) for h1 in h1s]
    for c in range(D_FF // ff_chunk):
        ups = [jnp.dot(a, wup_ref[0, :, c * ff_chunk:(c + 1) * ff_chunk], preferred_element_type=F32) for a in acts]
        hids = [jnp.square(jnp.maximum(up, 0.0)).astype(BF16) for up in ups]
        fs = [f + jnp.dot(hid, wdn_ref[0, c * ff_chunk:(c + 1) * ff_chunk, :], preferred_element_type=F32)
              for f, hid in zip(fs, hids)]
    for rows, h1, f in zip(subs, h1s, fs):
        o_ref[0, rows, :] = h1 + _rms(f, gpost_ref[...])


def _mix_out(h, head, ys, yw, ym, wglu, bglu, gh, wout, gpm, gpre, wup, wdn, gpost, layer, *, from_x=False, skip=0,
             tm=1056, ff_chunk=1024, nsub=2):
    assert not (from_x and skip)
    bsz, seq = yw.shape[0], yw.shape[1]
    out_rows = seq - skip
    if skip:
        el = pl.Element
        first = lambda i: pl.multiple_of(skip + i * tm, BLOCK)
        row = lambda w: pl.BlockSpec((el(1), el(tm), el(w)), lambda b, i: (b, first(i), 0))
        slabs = pl.BlockSpec((el(1), el(2), el(tm), el(128)), lambda b, i: (b, 0, first(i), 0))
    else:
        row = lambda w: pl.BlockSpec((1, tm, w), lambda b, i: (b, i, 0))
        slabs = pl.BlockSpec((1, 2, tm, 128), lambda b, i: (b, 0, i, 0))
    return pl.pallas_call(
        functools.partial(_mix_out_kernel, ff_chunk=ff_chunk, from_x=from_x, nsub=nsub),
        grid=(bsz, out_rows // tm),
        in_specs=[
            _seq_window_spec(tm, D_MODEL) if from_x else row(D_MODEL), _const_spec((BLOCK, D_MODEL)),
            slabs, row(SWA_WIDTH), row(MLA_WIDTH),
            _layer_spec((SSM_WIDTH, SSM_WIDTH), layer), _const_spec((1, SSM_WIDTH)), _const_spec((1, D_MODEL)),
            _layer_spec((D_MODEL, D_MODEL), layer), _const_spec((1, D_MODEL)), _const_spec((1, D_MODEL)),
            _layer_spec((D_MODEL, D_FF), layer), _layer_spec((D_FF, D_MODEL), layer), _const_spec((1, D_MODEL)),
        ],
        out_specs=pl.BlockSpec((1, tm, D_MODEL), lambda b, i: (b, i, 0)),
        out_shape=jax.ShapeDtypeStruct((bsz, out_rows, D_MODEL), F32),
        compiler_params=pltpu.CompilerParams(
            dimension_semantics=("parallel", "parallel"), vmem_limit_bytes=VMEM_LIMIT),
        name="mix_out",
    )(h, head, ys, yw, ym, wglu, bglu, gh, wout, gpm, gpre, wup, wdn, gpost)


def _rot_half_cols(w):
    half = w.shape[-1] // 2
    return jnp.concatenate([-w[..., half:], w[..., :half]], axis=-1)


def _stacked_weights(w_in, w_uq, w_uk, w_uv):
    depth = w_in.shape[0]
    w_in, w_uq, w_uk, w_uv = (w.astype(BF16) for w in (w_in, w_uq, w_uk, w_uv))
    z = lambda n: jnp.zeros((depth, D_MODEL, n), BF16)
    s_cq = SSM_WIDTH + SWA_WIDTH + 2 * SWA_KV_WIDTH
    s_ckv = s_cq + MLA_Q_RANK
    s_kr = s_ckv + MLA_KV_RANK
    kr = w_in[:, :, s_kr:s_kr + MLA_ROPE]
    w1 = jnp.concatenate([
        w_in[:, :, :s_ckv], z(C_CKV - C_CQ - MLA_Q_RANK),
        w_in[:, :, s_ckv:s_kr], kr, _rot_half_cols(kr), z(C_END - C_KR - 2 * MLA_ROPE)], axis=2)
    uq = w_uq.reshape(depth, MLA_Q_RANK, MLA_HEADS, MLA_NOPE + MLA_ROPE)
    rope = uq[..., MLA_NOPE:]
    wuq = jnp.concatenate([
        uq[..., :MLA_NOPE].reshape(depth, MLA_Q_RANK, -1), rope.reshape(depth, MLA_Q_RANK, -1),
        _rot_half_cols(rope).reshape(depth, MLA_Q_RANK, -1)], axis=2)
    wuq = jnp.pad(wuq, ((0, 0), (0, 256 - MLA_Q_RANK), (0, 0)))
    uk = w_uk.reshape(depth, MLA_KV_RANK, MLA_HEADS, MLA_NOPE).transpose(0, 2, 3, 1)
    wuk_bd = _group_block_diag(uk)
    uv = w_uv.reshape(depth, MLA_KV_RANK, MLA_HEADS, MLA_V).transpose(0, 2, 1, 3)
    wuv_heads = _group_block_diag(uv).reshape(depth, MLA_HEADS, MLA_KV_RANK, MLA_WIDTH)
    return w1, wuq, wuk_bd, wuv_heads


def _constant_tables(seq):
    half = MLA_ROPE // 2
    inv_freq = ROPE_THETA ** (-np.arange(half, dtype=np.float64) / half)
    ang = (np.arange(seq) - PAD)[:, None] * inv_freq[None, :]
    cos_t = np.tile(np.cos(ang), (1, 2 * MLA_HEADS)).astype(np.float32)
    sin_t = np.tile(np.sin(ang), (1, 2 * MLA_HEADS)).astype(np.float32)
    seq_k = -(-seq // MLA_KEY_ROWS) * MLA_KEY_ROWS
    kblk = np.arange(seq_k)[:, None] // BLOCK
    bcol = np.arange(MLA_BLOCKS)[None, :]
    one = np.ones((seq_k, 1))
    fill = np.zeros((seq_k, 128 - 2 - MLA_BLOCKS))
    kx_t = np.concatenate([one, np.where(np.arange(seq_k)[:, None] < PAD, NEG, 0.0), kblk == bcol, fill], axis=1)
    qx_t = np.concatenate([0.0 * one, one, np.where(bcol >= kblk, NEG, 0.0), fill], axis=1)[:seq]
    key_tail = np.zeros((seq_k - seq, MLA_QK))
    key_tail[:, MLA_KV_RANK:MLA_KV_RANK + MLA_EXT] = kx_t[seq:, :MLA_EXT]
    slopes = 2.0 ** (-8.0 * np.arange(1, SWA_HEADS + 1) / SWA_HEADS)
    slope_rows = np.repeat(slopes.reshape(SWA_KV_HEADS, SWA_GROUP), BLOCK, axis=1)[:, None, :]
    dist = (np.arange(BLOCK)[None, :] + BLOCK) - np.arange(2 * BLOCK)[:, None]
    in_window = np.tile((dist >= 0) & (dist < BLOCK), (1, SWA_GROUP))[None, None]
    alibi = (-LOG2E * slope_rows * np.tile(dist, (1, SWA_GROUP))[None])[None]
    key_pos = (np.arange(3)[:, None] - 1) * BLOCK + np.arange(2 * BLOCK)[None, :]
    key_ok = ((key_pos >= PAD) | (np.arange(3)[:, None] >= 2))[:, None, :, None]
    swa_bias = np.where(in_window & key_ok, alibi, NEG).astype(np.float32)
    bf16 = lambda t: jnp.asarray(t, dtype=BF16)
    return (jnp.asarray(cos_t), jnp.asarray(sin_t), bf16(qx_t), bf16(kx_t[:seq]), bf16(key_tail),
            jnp.asarray(swa_bias))


def kernel(x, meta_tokens, norm_pre_mix, norm_post_mix, norm_pre_mlp, norm_post_mlp, w_in, w_out, norm_heads,
           ssm_a_re, ssm_a_im, ssm_log_step, ssm_b_re, ssm_b_im, ssm_c_re, ssm_c_im, ssm_d, ssm_w_glu,
           ssm_b_glu, swa_sinks, mla_q_norm, mla_kv_norm, mla_w_uq, mla_w_uk, mla_w_uv, w_mlp_up, w_mlp_down):
    bsz, seq_real, d = x.shape
    depth = w_in.shape[0]
    head = jnp.concatenate([jnp.zeros((PAD, d), x.dtype), meta_tokens.astype(x.dtype)], axis=0)
    seq = BLOCK + seq_real
    h = x

    cos_t, sin_t, qx_t, kx_t, key_tail, swa_bias = _constant_tables(seq)
    per_row = lambda v: jnp.repeat(v.reshape(SWA_KV_HEADS, SWA_GROUP), BLOCK, axis=1)[:, None, :]

    r_mat, p_mat, q_mat, lam = _s5_operators(
        ssm_a_re, ssm_a_im, ssm_log_step, ssm_b_re, ssm_b_im, ssm_c_re, ssm_c_im)
    row = lambda v: v.reshape(1, -1).astype(F32)
    wglu, wout, wup, wdn = (w.astype(BF16) for w in (ssm_w_glu, w_out, w_mlp_up, w_mlp_down))
    w1, wuq, wuk_bd, wuv_heads = _stacked_weights(w_in, mla_w_uq, mla_w_uk, mla_w_uv)

    for l in range(depth):
        gq = jnp.concatenate([mla_q_norm[l], jnp.zeros((256 - MLA_Q_RANK,), F32)]).reshape(1, 256)
        u, q, kv, qm, kc = _mix_in(h, head, row(norm_pre_mix[l]), w1, gq, row(mla_kv_norm[l]), wuq, wuk_bd,
                                   cos_t, sin_t, qx_t, kx_t, l, from_x=l == 0)
        d_fold = jnp.tile(ssm_d[l].astype(F32), SSM_CHUNK).reshape(1, SSM_FOLD)
        y_ssm = _s5_sequence(u, r_mat, p_mat, q_mat, lam, d_fold, l)
        y_swa = _swa(q, kv, swa_bias, per_row(LOG2E * swa_sinks[l].astype(F32)))
        y_mla = _mla(qm, kc, key_tail, wuv_heads, l)
        last = l == depth - 1
        h = _mix_out(
            h, head, y_ssm, y_swa, y_mla,
            wglu, row(ssm_b_glu[l]), row(norm_heads[l]), wout, row(norm_post_mix[l]), row(norm_pre_mlp[l]),
            wup, wdn, row(norm_post_mlp[l]), l, from_x=l == 0, skip=PAD + N_META if last else 0,
            tm=1024 if last else 1056)
    return h
```
